```python
import jax, jax.numpy as jnp
from jax import lax
import numpy as np

D_MODEL = 1024
BATCH = 4
SEQ = 4096
DEPTH = 2
DEC_BATCH = 128
DEC_SEQ = 8
PAST_LEN = 2048
PAGE_SIZE = 128

MIX = D_MODEL
ATT_WIDTH = MIX // 2
CONV_WIDTH = MIX // 4
POOL_WIDTH = MIX - ATT_WIDTH - CONV_WIDTH
HEAD_DIM = 64
N_HEADS = ATT_WIDTH // HEAD_DIM
N_KV_HEADS = 2
GQA_REP = N_HEADS // N_KV_HEADS
KV_W = N_KV_HEADS * HEAD_DIM
ROT_DIM = HEAD_DIM // 4
ROPE_THETA = 500000.0
CMP_BLOCK = 32
CMP_STRIDE = 16
CMP_HIDDEN = 256
SEL_BLOCK = 64
SEL_TOP_K = 16
WINDOW = 512
N_BRANCH = 3
Q_BLOCK = 128
CONV_K = 3
POOL_WINDOWS = (2, 4, 8, 16)
N_POOL_GROUPS = len(POOL_WINDOWS)
POOL_GROUP_DIM = POOL_WIDTH // N_POOL_GROUPS
POOL_STATE = max(POOL_WINDOWS) - 1
N_EXPERT_GROUPS = 4
EXPERTS_PER_GROUP = 8
N_EXPERTS = N_EXPERT_GROUPS * EXPERTS_PER_GROUP
EXPERT_TOP_K = 2
D_EXPERT = 256
IN_COLS = ATT_WIDTH + 6 * KV_W + N_BRANCH * N_HEADS + 3 * CONV_WIDTH + POOL_WIDTH
ALPHA = (2 * DEPTH) ** 0.25
BETA = (8 * DEPTH) ** -0.25
LN_EPS = 1e-5
NEG = -1e30
FORCE_BONUS = 1e4

kernel_name = 'nsa_conv_pool_hmoe_hybrid_step'


def layer_norm(x, g, b):
    xf = x.astype(jnp.float32)
    mu = jnp.mean(xf, -1, keepdims=True)
    var = jnp.mean(jnp.square(xf - mu), -1, keepdims=True)
    return ((xf - mu) * lax.rsqrt(var + LN_EPS) * g + b).astype(x.dtype)


def rope(x, pos):
    inv = ROPE_THETA ** (-jnp.arange(0, ROT_DIM, 2, dtype=jnp.float32) / ROT_DIM)
    ang = pos.astype(jnp.float32)[:, None] * inv[None, :]
    cos = jnp.concatenate([jnp.cos(ang), jnp.cos(ang)], -1)[None, :, None, :]
    sin = jnp.concatenate([jnp.sin(ang), jnp.sin(ang)], -1)[None, :, None, :]
    xr = x[..., :ROT_DIM].astype(jnp.float32)
    half = ROT_DIM // 2
    rot = jnp.concatenate([-xr[..., half:], xr[..., :half]], -1)
    xr = xr * cos + rot * sin
    return jnp.concatenate([xr.astype(x.dtype), x[..., ROT_DIM:]], -1)


def compress(kv, pe, w1, w2):
    B, Tk = kv.shape[0], kv.shape[1]
    n_ch = Tk // CMP_STRIDE
    ch = kv[:, :n_ch * CMP_STRIDE].reshape(B, n_ch, CMP_STRIDE, N_KV_HEADS, HEAD_DIM)
    blk = jnp.concatenate([ch[:, :-1], ch[:, 1:]], axis=2) + pe[None, None, :, None, :]
    flat = jnp.moveaxis(blk, 3, 2).reshape(B, n_ch - 1, N_KV_HEADS, CMP_BLOCK * HEAD_DIM)
    return jax.nn.gelu(flat @ w1) @ w2


def nsa_attention(q, gates, q_pos0, kv_full, win_kv, win_pos0, pe_cmp, w_cmp1, w_cmp2):
    B, Tq = q.shape[0], q.shape[1]
    Tk = kv_full.shape[1]
    scale = HEAD_DIM ** -0.5
    q_pos = q_pos0 + jnp.arange(Tq, dtype=jnp.int32)
    q_rot = rope(q, q_pos)
    k_cmp = compress(kv_full[:, :, 0], pe_cmp[0], w_cmp1[0], w_cmp2[0])
    v_cmp = compress(kv_full[:, :, 1], pe_cmp[1], w_cmp1[1], w_cmp2[1])
    n_cmp = k_cmp.shape[1]
    cmp_start = jnp.arange(n_cmp, dtype=jnp.int32) * CMP_STRIDE
    cmp_end = cmp_start + (CMP_BLOCK - 1)
    n_sel = -(-Tk // SEL_BLOCK)
    kv_sel = jnp.pad(kv_full[:, :, 2:4], ((0, 0), (0, n_sel * SEL_BLOCK - Tk), (0, 0), (0, 0), (0, 0)))
    kv_sel = kv_sel.reshape(B, n_sel, SEL_BLOCK, 2, N_KV_HEADS, HEAD_DIM).transpose(3, 0, 4, 1, 2, 5)
    k_sel, v_sel = kv_sel[0], kv_sel[1]
    sel_ids = jnp.arange(n_sel, dtype=jnp.int32)
    sel_start = sel_ids * SEL_BLOCK
    overlap = ((cmp_start[:, None] < sel_start[None, :] + SEL_BLOCK)
               & (cmp_start[:, None] + CMP_BLOCK > sel_start[None, :])).astype(jnp.float32)
    top_k = min(SEL_TOP_K, n_sel)
    qb = min(Q_BLOCK, Tq)
    n_qb = Tq // qb
    band = WINDOW + qb - 1
    win_pad = jnp.pad(win_kv, ((0, 0), (WINDOW, 0), (0, 0), (0, 0), (0, 0)))
    win_offset = q_pos0 - win_pos0
    gather_blocks = jax.vmap(jax.vmap(lambda blocks, ids: blocks[ids]))

    def attend_block(args):
        j, qr, qs, g = args
        pos = q_pos0 + j * qb + jnp.arange(qb, dtype=jnp.int32)
        qr = qr.reshape(B, qb, N_KV_HEADS, GQA_REP, HEAD_DIM)
        qs = qs.reshape(B, qb, N_KV_HEADS, GQA_REP, HEAD_DIM)
        s = jnp.einsum('bqgrd,bcgd->bgrqc', qr, k_cmp).astype(jnp.float32) * scale
        m = cmp_end[None, :] <= pos[:, None]
        p = jnp.where(m, jax.nn.softmax(jnp.where(m, s, NEG), -1), 0.0)
        o_cmp = jnp.einsum('bgrqc,bcgd->bqgrd', p.astype(v_cmp.dtype), v_cmp)
        imp = jnp.einsum('bgrqc,cs->bgqs', p, overlap)
        cur = pos // SEL_BLOCK
        forced = (sel_ids[None, :] == 0) | (sel_ids[None, :] == cur[:, None]) | (sel_ids[None, :] == cur[:, None] - 1)
        valid = sel_start[None, :] <= pos[:, None]
        imp = jnp.where(valid, imp + jnp.where(forced, FORCE_BONUS, 0.0), NEG)
        _, idx = lax.top_k(imp, top_k)
        flat = idx.reshape(B, N_KV_HEADS, qb * top_k)
        ks = gather_blocks(k_sel, flat).reshape(B, N_KV_HEADS, qb, top_k * SEL_BLOCK, HEAD_DIM)
        vs = gather_blocks(v_sel, flat).reshape(B, N_KV_HEADS, qb, top_k * SEL_BLOCK, HEAD_DIM)
        kpos = (idx[..., None] * SEL_BLOCK + jnp.arange(SEL_BLOCK, dtype=jnp.int32)).reshape(B, N_KV_HEADS, qb, top_k * SEL_BLOCK)
        s = jnp.einsum('bqgrd,bgqkd->bgrqk', qs, ks).astype(jnp.float32) * scale
        m = (kpos <= pos[None, None, :, None])[:, :, None]
        p = jax.nn.softmax(jnp.where(m, s, NEG), -1)
        o_slc = jnp.einsum('bgrqk,bgqkd->bqgrd', p.astype(vs.dtype), vs)
        w = lax.dynamic_slice_in_dim(win_pad, win_offset + j * qb + 1, band, axis=1)
        kpos_w = q_pos0 + j * qb - (WINDOW - 1) + jnp.arange(band, dtype=jnp.int32)
        d = pos[:, None] - kpos_w[None, :]
        m = (kpos_w[None, :] >= win_pos0) & (d >= 0) & (d < WINDOW)
        s = jnp.einsum('bqgrd,bkgd->bgrqk', qs, w[:, :, 0]).astype(jnp.float32) * scale
        p = jax.nn.softmax(jnp.where(m, s, NEG), -1)
        o_win = jnp.einsum('bgrqk,bkgd->bqgrd', p.astype(w.dtype), w[:, :, 1])
        o = jnp.stack([o_cmp, o_slc, o_win], -1).reshape(B, qb, N_HEADS, HEAD_DIM, N_BRANCH)
        return jnp.einsum('bqhdn,bqhn->bqhd', o, g)

    blocks = lambda a: a.reshape((B, n_qb, qb) + a.shape[2:]).swapaxes(0, 1)
    out = lax.map(attend_block, (jnp.arange(n_qb, dtype=jnp.int32), blocks(q), blocks(q_rot), blocks(gates)))
    return out.swapaxes(0, 1).reshape(B, Tq, ATT_WIDTH)


def short_conv(bg, cg, v, conv_state, conv_w):
    u = cg * v
    ext = jnp.concatenate([conv_state.astype(u.dtype), u], 1)
    T = u.shape[1]
    y = ext[:, 0:T] * conv_w[0] + ext[:, 1:T + 1] * conv_w[1] + ext[:, 2:T + 2] * conv_w[2]
    return bg * y, ext[:, -(CONV_K - 1):]


def pool_mixer(p, pool_state, pos0, pool_w, pool_scale):
    B, T, _ = p.shape
    ext = jnp.concatenate([pool_state.astype(p.dtype), p], 1)
    cs = jnp.pad(jnp.cumsum(ext.astype(jnp.float32), axis=1), ((0, 0), (1, 0), (0, 0)))
    pos = pos0 + jnp.arange(T, dtype=jnp.int32)
    upper = cs[:, POOL_STATE + 1:]
    means = []
    for gi, wlen in enumerate(POOL_WINDOWS):
        lo, hi = gi * POOL_GROUP_DIM, (gi + 1) * POOL_GROUP_DIM
        lower = cs[:, POOL_STATE + 1 - wlen:POOL_STATE + 1 - wlen + T, lo:hi]
        cnt = jnp.minimum(wlen, pos + 1).astype(jnp.float32)[None, :, None]
        means.append((upper[:, :, lo:hi] - lower) / cnt)
    d = jnp.stack(means, 2) - p.reshape(B, T, N_POOL_GROUPS, POOL_GROUP_DIM).astype(jnp.float32)
    y = jnp.einsum('btgc,gce->btge', d.astype(p.dtype), pool_w).reshape(B, T, POOL_WIDTH) * pool_scale
    return y, ext[:, -POOL_STATE:]


def hier_moe(x, w_rg, b_rg, w_re, b_re, w_gate, w_up, w_down):
    B, T, D = x.shape
    xt = x.reshape(B * T, D)
    g_prob = jax.nn.softmax((xt @ w_rg).astype(jnp.float32) + b_rg, -1)
    g_w, g_idx = lax.top_k(g_prob, 1)
    e_logits = ((xt @ w_re).astype(jnp.float32) + b_re).reshape(-1, N_EXPERT_GROUPS, EXPERTS_PER_GROUP)
    e_logits = jnp.take_along_axis(e_logits, g_idx[:, :, None], axis=1)[:, 0]
    e_w, e_idx = lax.top_k(jax.nn.softmax(e_logits, -1), EXPERT_TOP_K)
    e_w = e_w / jnp.sum(e_w, -1, keepdims=True)
    eid = g_idx * EXPERTS_PER_GROUP + e_idx
    comb = jnp.sum(jax.nn.one_hot(eid, N_EXPERTS, dtype=jnp.float32) * (g_w * e_w)[..., None], axis=1)
    hid = jax.nn.silu(jnp.einsum('nd,edf->nef', xt, w_gate)) * jnp.einsum('nd,edf->nef', xt, w_up)
    y = jnp.einsum('nef,efd->nd', hid * comb[..., None].astype(hid.dtype), w_down)
    return y.reshape(B, T, D)


def trunk_layer(x, pos0, past_kv, win_buf, win_pos0, conv_state, pool_state,
                ln1_g, ln1_b, w_in, pe_cmp, w_cmp1, w_cmp2, conv_w, pool_w, pool_scale, w_o,
                ln2_g, ln2_b, w_rg, b_rg, w_re, b_re, w_gate, w_up, w_down):
    B, T, _ = x.shape
    pos = pos0 + jnp.arange(T, dtype=jnp.int32)
    h = x @ w_in
    splits = np.cumsum([ATT_WIDTH, 6 * KV_W, N_BRANCH * N_HEADS, CONV_WIDTH, CONV_WIDTH, CONV_WIDTH]).tolist()
    q, kv, gl, bg, cg, vc, pin = jnp.split(h, splits, axis=-1)
    q = q.reshape(B, T, N_HEADS, HEAD_DIM)
    kv = kv.reshape(B, T, 6, N_KV_HEADS, HEAD_DIM)
    new_kv = jnp.stack([kv[:, :, 0], kv[:, :, 1], rope(kv[:, :, 2], pos), kv[:, :, 3]], axis=2)
    new_win = jnp.stack([rope(kv[:, :, 4], pos), kv[:, :, 5]], axis=2)
    kv_full = jnp.concatenate([past_kv.astype(x.dtype), new_kv], 1)
    win_all = jnp.concatenate([win_buf.astype(x.dtype), new_win], 1)
    gates = jax.nn.sigmoid(gl.astype(jnp.float32)).astype(x.dtype).reshape(B, T, N_HEADS, N_BRANCH)
    o_att = nsa_attention(q, gates, pos0, kv_full, win_all, win_pos0, pe_cmp, w_cmp1, w_cmp2)
    o_conv, conv_new = short_conv(bg, cg, vc, conv_state, conv_w)
    o_pool, pool_new = pool_mixer(pin, pool_state, pos0, pool_w, pool_scale)
    mix = jnp.concatenate([o_att, o_conv, o_pool], -1) @ w_o
    x = layer_norm(ALPHA * x + mix, ln1_g, ln1_b)
    x = layer_norm(ALPHA * x + hier_moe(x, w_rg, b_rg, w_re, b_re, w_gate, w_up, w_down), ln2_g, ln2_b)
    keep = min(WINDOW, win_all.shape[1])
    return x, new_kv, win_all[:, -keep:], conv_new, pool_new


def setup_inputs(seed: int = 0) -> dict:
    key = jax.random.key(seed)
    ks = jax.random.split(key, 32)
    nrm = lambda k, shape, s: s * jax.random.normal(k, shape, jnp.float32)
    n_pages = PAST_LEN // PAGE_SIZE
    n_used = DEC_BATCH * n_pages
    n_phys = n_used + n_used // 4
    win_rows = min(WINDOW, PAST_LEN)
    page_table = jax.random.permutation(ks[0], n_phys)[:n_used].reshape(DEC_BATCH, n_pages).astype(jnp.int32)
    return {
        'x_prompt': nrm(ks[1], (BATCH, SEQ, D_MODEL), 1.0),
        'x_sample': nrm(ks[2], (DEC_BATCH, DEC_SEQ, D_MODEL), 1.0),
        'cache_kv': nrm(ks[3], (DEPTH, n_phys, PAGE_SIZE, 4, N_KV_HEADS, HEAD_DIM), 1.0),
        'cache_win': nrm(ks[4], (DEPTH, DEC_BATCH, win_rows, 2, N_KV_HEADS, HEAD_DIM), 1.0),
        'state_conv': nrm(ks[5], (DEPTH, DEC_BATCH, CONV_K - 1, CONV_WIDTH), 1.0),
        'state_pool': nrm(ks[6], (DEPTH, DEC_BATCH, POOL_STATE, POOL_WIDTH), 1.0),
        'page_table': page_table,
        'ln1_g': 1.0 + nrm(ks[7], (DEPTH, D_MODEL), 0.05),
        'ln1_b': nrm(ks[8], (DEPTH, D_MODEL), 0.02),
        'w_in': nrm(ks[9], (DEPTH, D_MODEL, IN_COLS), D_MODEL ** -0.5),
        'pe_cmp': nrm(ks[10], (DEPTH, 2, CMP_BLOCK, HEAD_DIM), 0.1),
        'w_cmp1': nrm(ks[11], (DEPTH, 2, CMP_BLOCK * HEAD_DIM, CMP_HIDDEN), (CMP_BLOCK * HEAD_DIM) ** -0.5),
        'w_cmp2': nrm(ks[12], (DEPTH, 2, CMP_HIDDEN, HEAD_DIM), CMP_HIDDEN ** -0.5),
        'conv_w': nrm(ks[13], (DEPTH, CONV_K, CONV_WIDTH), CONV_K ** -0.5),
        'pool_w': nrm(ks[14], (DEPTH, N_POOL_GROUPS, POOL_GROUP_DIM, POOL_GROUP_DIM), POOL_GROUP_DIM ** -0.5),
        'pool_scale': 1.0 + nrm(ks[15], (DEPTH, POOL_WIDTH), 0.05),
        'w_o': nrm(ks[16], (DEPTH, MIX, D_MODEL), BETA * MIX ** -0.5),
        'ln2_g': 1.0 + nrm(ks[17], (DEPTH, D_MODEL), 0.05),
        'ln2_b': nrm(ks[18], (DEPTH, D_MODEL), 0.02),
        'w_rg': nrm(ks[19], (DEPTH, D_MODEL, N_EXPERT_GROUPS), D_MODEL ** -0.5),
        'b_rg': nrm(ks[20], (DEPTH, N_EXPERT_GROUPS), 0.01),
        'w_re': nrm(ks[21], (DEPTH, D_MODEL, N_EXPERTS), D_MODEL ** -0.5),
        'b_re': nrm(ks[22], (DEPTH, N_EXPERTS), 0.01),
        'w_gate': nrm(ks[23], (DEPTH, N_EXPERTS, D_MODEL, D_EXPERT), D_MODEL ** -0.5),
        'w_up': nrm(ks[24], (DEPTH, N_EXPERTS, D_MODEL, D_EXPERT), D_MODEL ** -0.5),
        'w_down': nrm(ks[25], (DEPTH, N_EXPERTS, D_EXPERT, D_MODEL), BETA * D_EXPERT ** -0.5),
    }


def reference(x_prompt, x_sample, cache_kv, cache_win, state_conv, state_pool, page_table,
              ln1_g, ln1_b, w_in, pe_cmp, w_cmp1, w_cmp2, conv_w, pool_w, pool_scale, w_o,
              ln2_g, ln2_b, w_rg, b_rg, w_re, b_re, w_gate, w_up, w_down):
    n_pages = page_table.shape[1]
    B, Bd = x_prompt.shape[0], x_sample.shape[0]
    dt = x_prompt.dtype
    xp, xs = x_prompt, x_sample
    kv_p, kv_s, win_p, win_s, conv_p, conv_s, pool_p, pool_s = [], [], [], [], [], [], [], []
    for l in range(DEPTH):
        lp = (ln1_g[l], ln1_b[l], w_in[l], pe_cmp[l], w_cmp1[l], w_cmp2[l], conv_w[l], pool_w[l], pool_scale[l],
              w_o[l], ln2_g[l], ln2_b[l], w_rg[l], b_rg[l], w_re[l], b_re[l], w_gate[l], w_up[l], w_down[l])
        xp, a, b, c, d = trunk_layer(
            xp, 0, jnp.zeros((B, 0, 4, N_KV_HEADS, HEAD_DIM), dt), jnp.zeros((B, 0, 2, N_KV_HEADS, HEAD_DIM), dt), 0,
            jnp.zeros((B, CONV_K - 1, CONV_WIDTH), dt), jnp.zeros((B, POOL_STATE, POOL_WIDTH), dt), *lp)
        kv_p.append(a); win_p.append(b); conv_p.append(c); pool_p.append(d)
        past = cache_kv[l][page_table].reshape(Bd, n_pages * PAGE_SIZE, 4, N_KV_HEADS, HEAD_DIM)
        xs, a, b, c, d = trunk_layer(
            xs, PAST_LEN, past, cache_win[l], PAST_LEN - cache_win.shape[2], state_conv[l], state_pool[l], *lp)
        kv_s.append(a); win_s.append(b); conv_s.append(c); pool_s.append(d)
    return (xp, xs, jnp.stack(kv_p), jnp.stack(kv_s), jnp.stack(win_p), jnp.stack(win_s),
            jnp.stack(conv_p), jnp.stack(conv_s), jnp.stack(pool_p), jnp.stack(pool_s))
```

```python
import functools

import numpy as np
import jax
import jax.numpy as jnp
from jax import lax
from jax.experimental import pallas as pl
from jax.experimental.pallas import tpu as pltpu

F32 = jnp.float32
BF16 = jnp.bfloat16

HEAD_DIM = 64
N_HEADS = 8
N_KV_HEADS = 2
GQA_REP = N_HEADS // N_KV_HEADS
ATT_WIDTH = N_HEADS * HEAD_DIM
KV_W = N_KV_HEADS * HEAD_DIM
ROT_DIM = HEAD_DIM // 4
ROPE_THETA = 500000.0
CMP_BLOCK = 32
CMP_STRIDE = 16
CMP_HIDDEN = 256
SEL_BLOCK = 64
SEL_TOP_K = 16
WINDOW = 512
Q_BLOCK = 128
CONV_K = 3
POOL_WINDOWS = (2, 4, 8, 16)
POOL_STATE = max(POOL_WINDOWS) - 1
CARRY_STEPS = 16
N_EXPERT_GROUPS = 4
EXPERTS_PER_GROUP = 8
N_EXPERTS = N_EXPERT_GROUPS * EXPERTS_PER_GROUP
D_EXPERT = 256
LN_EPS = 1e-5
NEG = -1e30
FORCE_BONUS = 1e4
LANES = 128
VMEM_LIMIT = 56 * 1024 * 1024

_C_Q, _C_KV, _C_GL, _C_BG = 0, 512, 1280, 1304
_MAIN_W = 1664


def _cparams(sem):
    return pltpu.CompilerParams(dimension_semantics=sem, vmem_limit_bytes=VMEM_LIMIT)


def _dot(a, b):
    return jnp.dot(a, b, preferred_element_type=F32)


def _dot_nt(a, b):
    return lax.dot_general(a, b, (((1,), (1,)), ((), ())), preferred_element_type=F32)


def _layer_norm(y, g, b):
    mu = jnp.mean(y, axis=-1, keepdims=True)
    d = y - mu
    var = jnp.mean(d * d, axis=-1, keepdims=True)
    return d * lax.rsqrt(var + LN_EPS) * g + b


def _rope_rows(v, tab):
    return (v * tab[:, 0:128] + pltpu.roll(v, 120, 1) * tab[:, 128:256]
            + pltpu.roll(v, 8, 1) * tab[:, 256:384])


def _rope_cols(v, tab):
    return (v * tab[0:128] + pltpu.roll(v, 120, 0) * tab[128:256]
            + pltpu.roll(v, 8, 0) * tab[256:384])


def _proj_kernel(x_ref, wm_ref, wk_ref, tabr_ref, tabt_ref, cw_ref, pw_ref, ps_ref, cst_ref, pst_ref,
                 qq_ref, gate_ref, ocp_ref, rows_ref, kvt_ref, wint_ref, kvtb_ref, cnew_ref, pnew_ref,
                 cu_sc, cp_sc, *, tm, rs, pos0, rows_rope):
    ti = pl.program_id(1)
    carry = CARRY_STEPS * rs

    @pl.when(ti == 0)
    def _():
        cu_sc[...] = cst_ref[0]
        cp_sc[...] = pst_ref[0]

    xb = x_ref[0].astype(BF16)
    h = _dot(xb, wm_ref[...])
    kvt = _dot_nt(wk_ref[...], xb)
    tabr = tabr_ref[...]
    tabt = tabt_ref[...]

    scale = HEAD_DIM ** -0.5
    for c in range(4):
        qc = h[:, c * 128:(c + 1) * 128]
        qq_ref[0, :, c * 128:(c + 1) * 128] = (qc * scale).astype(BF16)
        qq_ref[0, :, 512 + c * 128:512 + (c + 1) * 128] = (_rope_rows(qc, tabr) * scale).astype(BF16)
    gate_ref[0] = 1.0 / (1.0 + jnp.exp(-h[:, 1536:1664]))

    for c in range(6):
        blk = kvt[c * 128:(c + 1) * 128]
        if c in (2, 4):
            blk = _rope_cols(blk, tabt)
        if c < 4:
            kvt_ref[0, c * 128:(c + 1) * 128, :] = blk
        else:
            wint_ref[0, (c - 4) * 128:(c - 3) * 128, :] = blk
        kvtb_ref[0, c * 128:(c + 1) * 128, :] = blk.astype(BF16)

    n_rows = rows_ref.shape[2]
    for c in range(n_rows // 128):
        blk = h[:, _MAIN_W + c * 128:_MAIN_W + (c + 1) * 128]
        if c in rows_rope:
            blk = _rope_rows(blk, tabr)
        rows_ref[0, :, c * 128:(c + 1) * 128] = blk.astype(rows_ref.dtype)

    bg = h[:, 512:768]
    cg = h[:, 768:1024]
    vc = h[:, 1024:1280]
    pin = h[:, 1280:1536]

    u = cg * vc
    eu = jnp.concatenate([cu_sc[...], u], axis=0)
    cw = cw_ref[...]
    y = (eu[carry - 2 * rs:carry - 2 * rs + tm] * cw[0:1] + eu[carry - rs:carry - rs + tm] * cw[1:2]
         + u * cw[2:3])
    o_conv = bg * y
    new_u = eu[tm:tm + carry]
    cu_sc[...] = new_u
    cnew_ref[0] = new_u

    ep = jnp.concatenate([cp_sc[...], pin], axis=0)
    s2 = ep[rs:] + ep[:-rs]
    s4 = s2[2 * rs:] + s2[:-2 * rs]
    s8 = s4[4 * rs:] + s4[:-4 * rs]
    s16 = s8[8 * rs:] + s8[:-8 * rs]
    take = lambda a: a[a.shape[0] - tm:]
    lane = lax.broadcasted_iota(jnp.int32, (tm, 256), 1)
    row = lax.broadcasted_iota(jnp.int32, (tm, 256), 0)
    dsum = jnp.where(lane < 64, take(s2), jnp.where(lane < 128, take(s4), jnp.where(lane < 192, take(s8), take(s16))))
    wlen = jnp.where(lane < 64, 2, jnp.where(lane < 128, 4, jnp.where(lane < 192, 8, 16)))
    step = pos0 + (ti * tm + row) // rs
    cnt = jnp.minimum(wlen, step + 1).astype(F32)
    d = dsum / cnt - pin
    o_pool = _dot(d.astype(BF16), pw_ref[...]) * ps_ref[...]
    new_p = ep[tm:tm + carry]
    cp_sc[...] = new_p
    pnew_ref[0] = new_p

    ocp_ref[0, :, 0:256] = o_conv.astype(BF16)
    ocp_ref[0, :, 256:512] = o_pool.astype(BF16)


def _proj_call(x3, wm, wk, tabr, tabt, cw, pw, ps, cst, pst, *, l, tm, rs, pos0, rows_dtype, rows_rope):
    n_sg, n_rows, d_model = x3.shape
    tiles = n_rows // tm
    n_main = wm.shape[2]
    n_rowcols = n_main - _MAIN_W
    carry = CARRY_STEPS * rs
    kern = functools.partial(_proj_kernel, tm=tm, rs=rs, pos0=pos0, rows_rope=rows_rope)
    out_shape = (
        jax.ShapeDtypeStruct((n_sg, n_rows, 1024), BF16),
        jax.ShapeDtypeStruct((n_sg, n_rows, 128), F32),
        jax.ShapeDtypeStruct((n_sg, n_rows, 512), BF16),
        jax.ShapeDtypeStruct((n_sg, n_rows, n_rowcols), rows_dtype),
        jax.ShapeDtypeStruct((n_sg, 512, n_rows), F32),
        jax.ShapeDtypeStruct((n_sg, 256, n_rows), F32),
        jax.ShapeDtypeStruct((n_sg, 768, n_rows), BF16),
        jax.ShapeDtypeStruct((n_sg, carry, 256), F32),
        jax.ShapeDtypeStruct((n_sg, carry, 256), F32),
    )
    row_blk = lambda w: pl.BlockSpec((1, tm, w), lambda s, t: (s, t, 0))
    col_blk = lambda w: pl.BlockSpec((1, w, tm), lambda s, t: (s, 0, t))
    st_blk = pl.BlockSpec((1, carry, 256), lambda s, t: (s, 0, 0))
    return pl.pallas_call(
        kern,
        grid=(n_sg, tiles),
        in_specs=[
            row_blk(d_model),
            pl.BlockSpec((None, d_model, n_main), lambda s, t: (l, 0, 0)),
            pl.BlockSpec((None, 768, d_model), lambda s, t: (l, 0, 0)),
            pl.BlockSpec((tm, 384), lambda s, t: (t, 0)),
            pl.BlockSpec((384, tm), lambda s, t: (0, t)),
            pl.BlockSpec((None, 8, 256), lambda s, t: (l, 0, 0)),
            pl.BlockSpec((None, 256, 256), lambda s, t: (l, 0, 0)),
            pl.BlockSpec((None, 1, 256), lambda s, t: (l, 0, 0)),
            st_blk, st_blk,
        ],
        out_specs=(row_blk(1024), row_blk(128), row_blk(512), row_blk(n_rowcols),
                   col_blk(512), col_blk(256), col_blk(768), st_blk, st_blk),
        out_shape=out_shape,
        scratch_shapes=[pltpu.VMEM((carry, 256), F32), pltpu.VMEM((carry, 256), F32)],
        compiler_params=_cparams(("arbitrary", "arbitrary")),
        name="proj",
    )(x3, wm, wk, tabr, tabt, cw, pw, ps, cst, pst)


def _gelu_tanh(x):
    return 0.5 * x * (1.0 + jnp.tanh(0.7978845608028654 * (x + 0.044715 * x * x * x)))


def _compress_core(read_rows, n, pe_ref, wa_ref, wb_ref, w2_ref, kc_ref, vc_ref):
    for kv, out_ref in ((0, kc_ref), (1, vc_ref)):
        cols = [read_rows(kv, r) for r in range(CMP_STRIDE)]
        xa = jnp.concatenate([cols[r] + pe_ref[kv, 0, r:r + 1, :] for r in range(CMP_STRIDE)], axis=1)
        xb = jnp.concatenate([cols[r] + pe_ref[kv, 1, r:r + 1, :] for r in range(CMP_STRIDE)], axis=1)
        a = _dot(xa.astype(BF16), wa_ref[kv])
        b = _dot(xb.astype(BF16), wb_ref[kv])
        pre = a + pltpu.roll(b, n - 1, 0)
        hid = _gelu_tanh(pre)
        out_ref[0] = _dot(hid.astype(BF16), w2_ref[kv]).astype(BF16)


def _compress_rows_kernel(k_ref, v_ref, pe_ref, wa_ref, wb_ref, w2_ref, kc_ref, vc_ref, *, n):
    srcs = (k_ref, v_ref)
    read = lambda kv, r: srcs[kv][0, pl.ds(r, n, stride=CMP_STRIDE), :]
    _compress_core(read, n, pe_ref, wa_ref, wb_ref, w2_ref, kc_ref, vc_ref)


def _compress_pages_kernel(pt_ref, *refs, n_pages_step, page, n):
    pages = refs[:n_pages_step]
    pe_ref, wa_ref, wb_ref, w2_ref, kc_ref, vc_ref, kbuf, vbuf = refs[n_pages_step:]
    for i, pg in enumerate(pages):
        kbuf[i * page:(i + 1) * page, :] = pg[0:128, :].T
        vbuf[i * page:(i + 1) * page, :] = pg[128:256, :].T
    bufs = (kbuf, vbuf)
    read = lambda kv, r: bufs[kv][pl.ds(r, n, stride=CMP_STRIDE), :]
    _compress_core(read, n, pe_ref, wa_ref, wb_ref, w2_ref, kc_ref, vc_ref)


def _cmp_weight_specs(l, nidx):
    im4 = (lambda *a: (l, 0, 0, 0))
    im5 = (lambda *a: (l, 0, 0, 0, 0))
    return [
        pl.BlockSpec((None, 2, 2, CMP_STRIDE, 128), im5),
        pl.BlockSpec((None, 2, 2048, 512), im4),
        pl.BlockSpec((None, 2, 2048, 512), im4),
        pl.BlockSpec((None, 2, 512, 128), im4),
    ]


def _compress_rows_call(rows, pe, wa, wb, w2, *, l):
    b, t, _ = rows.shape
    n = t // CMP_STRIDE
    out = jax.ShapeDtypeStruct((b, n, 128), BF16)
    return pl.pallas_call(
        functools.partial(_compress_rows_kernel, n=n),
        grid=(b,),
        in_specs=[pl.BlockSpec((1, t, 128), lambda i: (i, 0, 0)),
                  pl.BlockSpec((1, t, 128), lambda i: (i, 0, 1))] + _cmp_weight_specs(l, 1),
        out_specs=(pl.BlockSpec((1, n, 128), lambda i: (i, 0, 0)),) * 2,
        out_shape=(out, out),
        compiler_params=_cparams(("arbitrary",)),
        name="compress_rows",
    )(rows, rows, pe, wa, wb, w2)


def _compress_pages_call(pt_flat, cache_t, pe, wa, wb, w2, *, l, n_seq, n_pages, n_phys, seqs_step):
    page = cache_t.shape[2]
    n_pages_step = seqs_step * n_pages
    n = n_pages_step * page // CMP_STRIDE
    steps = n_seq // seqs_step

    def page_spec(k):
        s, p = divmod(k, n_pages)
        return pl.BlockSpec((None, 256, page),
                            lambda i, pt: (l * n_phys + pt[(i * seqs_step + s) * n_pages + p], 0, 0))

    out = jax.ShapeDtypeStruct((steps, n, 128), BF16)
    return pl.pallas_call(
        functools.partial(_compress_pages_kernel, n_pages_step=n_pages_step, page=page, n=n),
        grid_spec=pltpu.PrefetchScalarGridSpec(
            num_scalar_prefetch=1,
            grid=(steps,),
            in_specs=[page_spec(k) for k in range(n_pages_step)] + _cmp_weight_specs(l, 2),
            out_specs=(pl.BlockSpec((1, n, 128), lambda i, pt: (i, 0, 0)),) * 2,
            scratch_shapes=[pltpu.VMEM((n_pages_step * page, 128), F32)] * 2,
        ),
        out_shape=(out, out),
        compiler_params=_cparams(("arbitrary",)),
        name="compress_pages",
    )(pt_flat, *([cache_t] * n_pages_step), pe, wa, wb, w2)


def _q_rows(q, tq):
    lo = lax.broadcasted_iota(jnp.int32, (tq, 128), 1) < 64
    cols = [q[:, c * 128:(c + 1) * 128] for c in range(4)]
    return jnp.concatenate([jnp.where(lo, c, 0.0) for c in cols] + [jnp.where(lo, 0.0, c) for c in cols], axis=0)


def _pair_cols(o, tq):
    lo = lax.broadcasted_iota(jnp.int32, (tq, 128), 1) < 64
    return [jnp.where(lo, o[c * tq:(c + 1) * tq], o[(4 + c) * tq:(5 + c) * tq]) for c in range(4)]


def _row_pos(tq, n, pos_base):
    row = lax.broadcasted_iota(jnp.int32, (8 * tq, n), 0)
    return pos_base + (row & (tq - 1))


def _compressed_branch(qc, kc, vc, ov, tq, pos_base):
    nc = kc.shape[0]
    s = _dot_nt(qc, kc)
    pos = _row_pos(tq, nc, pos_base)
    cend = lax.broadcasted_iota(jnp.int32, (8 * tq, nc), 1) * CMP_STRIDE + (CMP_BLOCK - 1)
    vis = cend <= pos
    s = jnp.where(vis, s, NEG)
    e = jnp.where(vis, jnp.exp(s - jnp.max(s, axis=-1, keepdims=True)), 0.0)
    den = jnp.sum(e, axis=-1, keepdims=True)
    p = (e / jnp.where(den > 0.0, den, 1.0)).astype(BF16)
    o = _dot(p, vc)
    imp8 = _dot(p, ov)
    imp = jnp.concatenate([imp8[(4 * g) * tq:(4 * g + 1) * tq] + imp8[(4 * g + 1) * tq:(4 * g + 2) * tq]
                           + imp8[(4 * g + 2) * tq:(4 * g + 3) * tq] + imp8[(4 * g + 3) * tq:(4 * g + 4) * tq]
                           for g in range(2)], axis=0)
    return o, imp


def _select_blocks(imp, tq, pos_base):
    row = lax.broadcasted_iota(jnp.int32, (2 * tq, 128), 0)
    blk = lax.broadcasted_iota(jnp.int32, (2 * tq, 128), 1)
    pos = pos_base + (row & (tq - 1))
    cur = pos // SEL_BLOCK
    forced = (blk == 0) | (blk == cur) | (blk == cur - 1)
    valid = blk * SEL_BLOCK <= pos
    score = jnp.where(valid, imp + jnp.where(forced, FORCE_BONUS, 0.0), NEG)
    blk_f = blk.astype(F32)

    def body(_, c):
        sc, sel = c
        mx = jnp.max(sc, axis=-1, keepdims=True)
        first = jnp.min(jnp.where(sc == mx, blk_f, 1e9), axis=-1, keepdims=True)
        pick = blk_f == first
        return jnp.where(pick, -jnp.inf, sc), jnp.where(pick, 1.0, sel)

    _, sel = lax.fori_loop(0, SEL_TOP_K, body, (score, jnp.zeros((2 * tq, 128), F32)))
    return sel


def _softmax_pv(pieces):
    m = None
    for s, _, _ in pieces:
        mi = jnp.max(s, axis=-1, keepdims=True)
        m = mi if m is None else jnp.maximum(m, mi)
    den = 0.0
    acc = 0.0
    for s, v, fm in pieces:
        p = jnp.exp(s - m)
        den = den + jnp.sum(p, axis=-1, keepdims=True)
        pb = p.astype(BF16)
        acc = acc + (_dot_nt(pb, v) if fm else _dot(pb, v))
    return acc / den


def _gated_sum(branches, gates, tq):
    lo = lax.broadcasted_iota(jnp.int32, (tq, 128), 1) < 64
    cols = [_pair_cols(o, tq) for o in branches]
    out = []
    for c in range(4):
        acc = 0.0
        for n in range(3):
            ga = gates[:, 3 * c + n:3 * c + n + 1]
            gb = gates[:, 3 * (c + 4) + n:3 * (c + 4) + n + 1]
            acc = acc + cols[n][c] * jnp.where(lo, ga, gb)
        out.append(acc)
    return out


def _attn_prompt_kernel(qq_ref, gate_ref, kc_ref, vc_ref, kst_ref, vst_ref, kwt_ref, vwt_ref, et_ref, ov_ref,
                        o_ref, kaug_sc, m_sc, l_sc, acc_sc, *, tq, t_len, tk):
    j = pl.program_id(1)
    rows = 8 * tq

    @pl.when(j == 0)
    def _():
        kaug_sc[0:128, :] = kst_ref[0]
        kaug_sc[128:256, :] = et_ref[...]

    pos_base = j * tq
    qq = qq_ref[0].astype(F32)
    qc = _q_rows(qq[:, 0:512], tq).astype(BF16)
    qs = _q_rows(qq[:, 512:1024], tq).astype(BF16)

    o_cmp, imp = _compressed_branch(qc, kc_ref[0], vc_ref[0], ov_ref[...], tq, pos_base)
    sel = _select_blocks(imp, tq, pos_base)
    bias = jnp.where(sel > 0.5, 0.0, NEG).astype(BF16)
    bias_rows = jnp.concatenate([bias[0:tq]] * 4 + [bias[tq:2 * tq]] * 4, axis=0)
    q_aug = jnp.concatenate([qs, bias_rows], axis=1)

    m_sc[...] = jnp.full((rows, 128), NEG, F32)
    l_sc[...] = jnp.zeros((rows, 128), F32)
    acc_sc[...] = jnp.zeros((rows, 128), F32)

    def update(k0, causal):
        s = _dot(q_aug, kaug_sc[:, pl.ds(k0, tk)])
        if causal:
            kpos = k0 + lax.broadcasted_iota(jnp.int32, (rows, tk), 1)
            s = jnp.where(kpos <= _row_pos(tq, tk, pos_base), s, NEG)
        m_old = m_sc[...]
        m_new = jnp.maximum(m_old, jnp.max(s, axis=-1, keepdims=True))
        alpha = jnp.exp(m_old - m_new)
        p = jnp.exp(s - jnp.concatenate([m_new] * (tk // 128), axis=1))
        l_sc[...] = alpha * l_sc[...] + jnp.sum(p, axis=-1, keepdims=True)
        acc_sc[...] = alpha * acc_sc[...] + _dot_nt(p.astype(BF16), vst_ref[0, :, pl.ds(k0, tk)])
        m_sc[...] = m_new

    n_bulk = (j * tq) // tk

    def bulk(kt, c):
        update(pl.multiple_of(kt * tk, tk), False)
        return c

    lax.fori_loop(0, n_bulk, bulk, 0)
    update(pl.multiple_of(n_bulk * tk, tk), True)
    o_slc = acc_sc[...] / l_sc[...]

    wk = WINDOW + tq
    k0 = pl.multiple_of(jnp.maximum(j * tq - WINDOW, 0), 128)
    s = _dot(qs, kwt_ref[0, :, pl.ds(k0, wk)])
    dist = _row_pos(tq, wk, pos_base) - (k0 + lax.broadcasted_iota(jnp.int32, (rows, wk), 1))
    s = jnp.where((dist >= 0) & (dist < WINDOW), s, NEG)
    o_win = _softmax_pv([(s, vwt_ref[0, :, pl.ds(k0, wk)], True)])

    cols = _gated_sum([o_cmp, o_slc, o_win], gate_ref[0], tq)
    for c in range(4):
        o_ref[0, :, c * 128:(c + 1) * 128] = cols[c].astype(BF16)


def _attn_prompt_call(qq, gates, kc, vc, kvtb, et, ov):
    b, t_len, _ = qq.shape
    tq = Q_BLOCK
    tk = 256
    nc = kc.shape[1]
    kern = functools.partial(_attn_prompt_kernel, tq=tq, t_len=t_len, tk=tk)
    kv_blk = lambda c: pl.BlockSpec((1, 128, t_len), lambda i, j: (i, c, 0))
    return pl.pallas_call(
        kern,
        grid=(b, t_len // tq),
        in_specs=[
            pl.BlockSpec((1, tq, 1024), lambda i, j: (i, j, 0)),
            pl.BlockSpec((1, tq, 128), lambda i, j: (i, j, 0)),
            pl.BlockSpec((1, nc, 128), lambda i, j: (i, 0, 0)),
            pl.BlockSpec((1, nc, 128), lambda i, j: (i, 0, 0)),
            kv_blk(2), kv_blk(3), kv_blk(4), kv_blk(5),
            pl.BlockSpec((128, t_len), lambda i, j: (0, 0)),
            pl.BlockSpec((nc, 128), lambda i, j: (0, 0)),
        ],
        out_specs=pl.BlockSpec((1, tq, 512), lambda i, j: (i, j, 0)),
        out_shape=jax.ShapeDtypeStruct((b, t_len, 512), BF16),
        scratch_shapes=[pltpu.VMEM((256, t_len), BF16), pltpu.VMEM((8 * tq, 128), F32),
                        pltpu.VMEM((8 * tq, 128), F32), pltpu.VMEM((8 * tq, 128), F32)],
        compiler_params=_cparams(("arbitrary", "arbitrary")),
        name="attn_prompt",
    )(qq, gates, kc, vc, kvtb, kvtb, kvtb, kvtb, et, ov)


def _attn_sample_kernel(pt_ref, *refs, n_pages, page, tq, past, win_rows):
    pages = refs[:n_pages]
    qq_ref, gate_ref, kc_ref, vc_ref, new_ref, cwin_ref, es_ref, ov_ref, o_ref = refs[n_pages:]
    rows = 8 * tq
    pos_base = past

    qq = qq_ref[0].astype(F32)
    qc = _q_rows(qq[:, 0:512], tq).astype(BF16)
    qs = _q_rows(qq[:, 512:1024], tq).astype(BF16)

    o_cmp, imp = _compressed_branch(qc, kc_ref[0], vc_ref[0], ov_ref[...], tq, pos_base)
    sel = _select_blocks(imp, tq, pos_base)

    new = jnp.concatenate([new_ref[0].astype(F32), jnp.zeros((128 - tq, 512), F32)], axis=0).astype(BF16)

    kst = jnp.concatenate([pg[0:128, :].astype(BF16) for pg in pages], axis=1)
    vst = jnp.concatenate([pg[128:256, :].astype(BF16) for pg in pages], axis=1)
    n_keys = past + 128
    member = _dot(sel.astype(BF16), es_ref[...])
    member = jnp.concatenate([member[0:tq]] * 4 + [member[tq:2 * tq]] * 4, axis=0)
    kpos = lax.broadcasted_iota(jnp.int32, (rows, n_keys), 1)
    ok = (member > 0.5) & (kpos <= _row_pos(tq, n_keys, pos_base))
    s_past = jnp.where(ok[:, 0:past], _dot(qs, kst), NEG)
    s_new = jnp.where(ok[:, past:], _dot_nt(qs, new[:, 0:128]), NEG)
    o_slc = _softmax_pv([(s_past, vst, True), (s_new, new[:, 128:256], False)])

    cw = cwin_ref[0]
    n_wk = win_rows + 128
    kpos_w = (past - win_rows) + lax.broadcasted_iota(jnp.int32, (rows, n_wk), 1)
    dist = _row_pos(tq, n_wk, pos_base) - kpos_w
    okw = (dist >= 0) & (dist < WINDOW)
    s_old = jnp.where(okw[:, 0:win_rows], _dot(qs, cw[0:128].astype(BF16)), NEG)
    s_nw = jnp.where(okw[:, win_rows:], _dot_nt(qs, new[:, 256:384]), NEG)
    o_win = _softmax_pv([(s_old, cw[128:256].astype(BF16), True), (s_nw, new[:, 384:512], False)])

    cols = _gated_sum([o_cmp, o_slc, o_win], gate_ref[0], tq)
    for c in range(4):
        o_ref[0, :, c * 128:(c + 1) * 128] = cols[c].astype(BF16)


def _attn_sample_call(pt_flat, cache_t, qq, gates, kc, vc, new_rows, cwin_t, es, ov, *, l, n_pages, n_phys, past):
    n_seq, tq, _ = qq.shape
    page = cache_t.shape[2]
    win_rows = cwin_t.shape[2]
    nc = kc.shape[1]
    kern = functools.partial(_attn_sample_kernel, n_pages=n_pages, page=page, tq=tq, past=past, win_rows=win_rows)

    def page_spec(p):
        return pl.BlockSpec((None, 256, page), lambda i, pt: (l * n_phys + pt[i * n_pages + p], 1, 0))

    seq_blk = lambda w: pl.BlockSpec((1, tq, w), lambda i, pt: (i, 0, 0))
    return pl.pallas_call(
        kern,
        grid_spec=pltpu.PrefetchScalarGridSpec(
            num_scalar_prefetch=1,
            grid=(n_seq,),
            in_specs=[page_spec(p) for p in range(n_pages)] + [
                seq_blk(1024), seq_blk(128),
                pl.BlockSpec((1, nc, 128), lambda i, pt: (i, 0, 0)),
                pl.BlockSpec((1, nc, 128), lambda i, pt: (i, 0, 0)),
                seq_blk(512),
                pl.BlockSpec((1, 256, win_rows), lambda i, pt: (l * n_seq + i, 0, 0)),
                pl.BlockSpec((128, past + 128), lambda i, pt: (0, 0)),
                pl.BlockSpec((nc, 128), lambda i, pt: (0, 0)),
            ],
            out_specs=pl.BlockSpec((1, tq, 512), lambda i, pt: (i, 0, 0)),
        ),
        out_shape=jax.ShapeDtypeStruct((n_seq, tq, 512), BF16),
        compiler_params=_cparams(("arbitrary",)),
        name="attn_sample",
    )(pt_flat, *([cache_t] * n_pages), qq, gates, kc, vc, new_rows, cwin_t, es, ov)


def _mix_kernel(oa_ref, ocp_ref, x_ref, wo_ref, g_ref, b_ref, wr_ref, br_ref, x1_ref, comb_ref, *, alpha):
    mix = _dot(oa_ref[...], wo_ref[0:512, :]) + _dot(ocp_ref[...], wo_ref[512:1024, :])
    x1 = _layer_norm(alpha * x_ref[...] + mix, g_ref[...], b_ref[...])
    x1_ref[...] = x1

    tm = x1.shape[0]
    xh = x1.astype(BF16)
    xl = (x1 - xh.astype(F32)).astype(BF16)
    hw = _dot(xh, wr_ref[...])
    logits = hw[:, 0:128] + hw[:, 128:256] + _dot(xl, wr_ref[:, 0:128]) + br_ref[...]
    lane = lax.broadcasted_iota(jnp.int32, (tm, 128), 1).astype(F32)
    big = 1e9
    is_g = (lane >= N_EXPERTS) & (lane < N_EXPERTS + N_EXPERT_GROUPS)
    lg = jnp.where(is_g, logits, NEG)
    ge = jnp.where(is_g, jnp.exp(lg - jnp.max(lg, axis=-1, keepdims=True)), 0.0)
    gp = ge / jnp.sum(ge, axis=-1, keepdims=True)
    gw = jnp.max(gp, axis=-1, keepdims=True)
    gidx = jnp.min(jnp.where(is_g & (gp == gw), lane - N_EXPERTS, big), axis=-1, keepdims=True)
    in_g = (lane >= gidx * EXPERTS_PER_GROUP) & (lane < (gidx + 1.0) * EXPERTS_PER_GROUP)
    le = jnp.where(in_g, logits, NEG)
    ee = jnp.where(in_g, jnp.exp(le - jnp.max(le, axis=-1, keepdims=True)), 0.0)
    ep = ee / jnp.sum(ee, axis=-1, keepdims=True)
    w1 = jnp.max(jnp.where(in_g, ep, -1.0), axis=-1, keepdims=True)
    i1 = jnp.min(jnp.where(in_g & (ep == w1), lane, big), axis=-1, keepdims=True)
    rest = in_g & (lane != i1)
    w2 = jnp.max(jnp.where(rest, ep, -1.0), axis=-1, keepdims=True)
    i2 = jnp.min(jnp.where(rest & (ep == w2), lane, big), axis=-1, keepdims=True)
    den = w1 + w2
    comb_ref[...] = jnp.where(lane == i1, gw * (w1 / den), jnp.where(lane == i2, gw * (w2 / den), 0.0))


def _mix_call(oa, ocp, x, wo, g, b, wr, br, *, l, alpha, tm):
    n, d = x.shape
    row = lambda w: pl.BlockSpec((tm, w), lambda i: (i, 0))
    vec = lambda w: pl.BlockSpec((None, 1, w), lambda i: (l, 0, 0))
    return pl.pallas_call(
        functools.partial(_mix_kernel, alpha=alpha),
        grid=(n // tm,),
        in_specs=[row(512), row(512), row(d),
                  pl.BlockSpec((None, 1024, d), lambda i: (l, 0, 0)), vec(d), vec(d),
                  pl.BlockSpec((None, d, 256), lambda i: (l, 0, 0)), vec(128)],
        out_specs=(row(d), row(128)),
        out_shape=(jax.ShapeDtypeStruct((n, d), F32), jax.ShapeDtypeStruct((n, 128), F32)),
        compiler_params=_cparams(("arbitrary",)),
        name="mix_router",
    )(oa, ocp, x, wo, g, b, wr, br)


def _moe_kernel(x1_ref, comb_ref, wgu_ref, wd_ref, g_ref, b_ref, o_ref, xb_sc, acc_sc, *, alpha):
    e = pl.program_id(1)

    @pl.when(e == 0)
    def _():
        xb_sc[...] = x1_ref[...].astype(BF16)
        acc_sc[...] = jnp.zeros_like(acc_sc)

    hgu = _dot(xb_sc[...], wgu_ref[...])
    hg = hgu[:, 0:D_EXPERT]
    hid = hg * (1.0 / (1.0 + jnp.exp(-hg))) * hgu[:, D_EXPERT:2 * D_EXPERT]
    tm = hid.shape[0]
    lane = lax.broadcasted_iota(jnp.int32, (tm, 128), 1)
    ce = jnp.sum(jnp.where(lane == e, comb_ref[...], 0.0), axis=-1, keepdims=True)
    acc_sc[...] += _dot((hid * ce).astype(BF16), wd_ref[...])

    @pl.when(e == pl.num_programs(1) - 1)
    def _():
        o_ref[...] = _layer_norm(alpha * x1_ref[...] + acc_sc[...], g_ref[...], b_ref[...])


def _moe_call(x1, comb, wgu, wd, g, b, *, l, alpha, tm):
    n, d = x1.shape
    vec = pl.BlockSpec((None, 1, d), lambda i, e: (l, 0, 0))
    return pl.pallas_call(
        functools.partial(_moe_kernel, alpha=alpha),
        grid=(n // tm, N_EXPERTS),
        in_specs=[pl.BlockSpec((tm, d), lambda i, e: (i, 0)),
                  pl.BlockSpec((tm, 128), lambda i, e: (i, 0)),
                  pl.BlockSpec((None, None, d, 2 * D_EXPERT), lambda i, e: (l, e, 0, 0)),
                  pl.BlockSpec((None, None, D_EXPERT, d), lambda i, e: (l, e, 0, 0)),
                  vec, vec],
        out_specs=pl.BlockSpec((tm, d), lambda i, e: (i, 0)),
        out_shape=jax.ShapeDtypeStruct((n, d), F32),
        scratch_shapes=[pltpu.VMEM((tm, d), BF16), pltpu.VMEM((tm, d), F32)],
        compiler_params=_cparams(("arbitrary", "arbitrary")),
        name="moe",
    )(x1, comb, wgu, wd, g, b)


def _rope_table(pos):
    half = ROT_DIM // 2
    inv = ROPE_THETA ** (-jnp.arange(0, ROT_DIM, 2, dtype=F32) / ROT_DIM)
    ang = jnp.asarray(pos, F32)[:, None] * inv[None, :]
    cos, sin = jnp.cos(ang), jnp.sin(ang)
    n = ang.shape[0]
    z = jnp.zeros((n, HEAD_DIM - ROT_DIM), F32)
    zh = jnp.zeros((n, half), F32)
    c64 = jnp.concatenate([cos, cos, jnp.ones_like(z)], axis=1)
    s1 = jnp.concatenate([-sin, zh, z], axis=1)
    s2 = jnp.concatenate([zh, sin, z], axis=1)
    return jnp.concatenate([c64, c64, s1, s1, s2, s2], axis=1)


def _overlap_table(n_rows):
    c = np.arange(n_rows)[:, None] * CMP_STRIDE
    s = np.arange(128)[None, :] * SEL_BLOCK
    return jnp.asarray(((c < s + SEL_BLOCK) & (c + CMP_BLOCK > s)).astype(np.float32), BF16)


def _block_indicator(n_keys):
    k = np.arange(n_keys)[None, :] // SEL_BLOCK
    return jnp.asarray((np.arange(128)[:, None] == k).astype(np.float32), BF16)


def _block_diag2(w):
    z = jnp.zeros_like(w)
    return jnp.concatenate([jnp.concatenate([w, z], axis=-1), jnp.concatenate([z, w], axis=-1)], axis=-2)


def kernel(x_prompt, x_sample, cache_kv, cache_win, state_conv, state_pool, page_table, ln1_g, ln1_b, w_in, pe_cmp, w_cmp1, w_cmp2, conv_w, pool_w, pool_scale, w_o, ln2_g, ln2_b, w_rg, b_rg, w_re, b_re, w_gate, w_up, w_down):
    n_b, t_len, d_model = x_prompt.shape
    n_seq, t_dec, _ = x_sample.shape
    depth = w_in.shape[0]
    n_phys, page = cache_kv.shape[1], cache_kv.shape[2]
    n_pages = page_table.shape[1]
    past = n_pages * page
    win_rows = cache_win.shape[2]
    alpha = float((2 * depth) ** 0.25)
    assert d_model == 1024 and t_len % 512 == 0 and t_dec == 8 and past % SEL_BLOCK == 0 and win_rows == WINDOW

    q_perm = np.concatenate([np.r_[c * 64:(c + 1) * 64, (c + 4) * 64:(c + 5) * 64] for c in range(4)])
    w_main = jnp.concatenate([w_in[:, :, q_perm], w_in[:, :, _C_BG:], w_in[:, :, _C_GL:_C_BG],
                              jnp.zeros((depth, d_model, 128 - (_C_BG - _C_GL)), F32)], axis=2)
    wm_p = jnp.concatenate([w_main, w_in[:, :, _C_KV:_C_KV + 256]], axis=2).astype(BF16)
    wm_s = jnp.concatenate([w_main, w_in[:, :, _C_KV + 256:_C_KV + 768]], axis=2).astype(BF16)
    wk_t = jnp.swapaxes(w_in[:, :, _C_KV:_C_GL], 1, 2).astype(BF16)
    cw8 = jnp.concatenate([conv_w, jnp.zeros((depth, 8 - CONV_K, 256), F32)], axis=1)
    pw_bd = jnp.zeros((depth, 256, 256), F32)
    for g in range(4):
        pw_bd = pw_bd.at[:, g * 64:(g + 1) * 64, g * 64:(g + 1) * 64].set(pool_w[:, g])
    pw_bd = pw_bd.astype(BF16)
    ps3 = pool_scale[:, None, :]

    w1r = w_cmp1.reshape(depth, 2, CMP_BLOCK, HEAD_DIM, CMP_HIDDEN)
    bd_half = lambda w: _block_diag2(w).reshape(depth, 2, CMP_STRIDE * 128, 2 * CMP_HIDDEN).astype(BF16)
    wa = bd_half(w1r[:, :, :CMP_STRIDE])
    wb = bd_half(w1r[:, :, CMP_STRIDE:])
    w2 = _block_diag2(w_cmp2).astype(BF16)
    pe2 = jnp.concatenate([pe_cmp, pe_cmp], axis=-1).reshape(depth, 2, 2, CMP_STRIDE, 128)

    wo_perm = jnp.concatenate([w_o[:, q_perm], w_o[:, ATT_WIDTH:]], axis=1).astype(BF16)
    wr32 = jnp.concatenate([w_re, w_rg, jnp.zeros((depth, d_model, 128 - N_EXPERTS - N_EXPERT_GROUPS), F32)], axis=2)
    wr_hi = wr32.astype(BF16)
    wr = jnp.concatenate([wr_hi, (wr32 - wr_hi.astype(F32)).astype(BF16)], axis=2)
    br = jnp.concatenate([b_re, b_rg, jnp.zeros((depth, 128 - N_EXPERTS - N_EXPERT_GROUPS), F32)], axis=1)[:, None, :]
    wgu = jnp.concatenate([w_gate, w_up], axis=-1).astype(BF16)
    wd = w_down.astype(BF16)
    g1, b1, g2, b2 = ln1_g[:, None, :], ln1_b[:, None, :], ln2_g[:, None, :], ln2_b[:, None, :]

    tab_p = _rope_table(np.arange(t_len))
    tab_s = _rope_table(past + np.repeat(np.arange(t_dec), n_seq))
    tab_pt, tab_st = tab_p.T, tab_s.T
    nc_p = t_len // CMP_STRIDE
    nc_s = past // CMP_STRIDE
    ov_p, ov_s = _overlap_table(nc_p), _overlap_table(nc_s)
    et_p = _block_indicator(t_len)
    es_s = _block_indicator(past + 128)

    cache_t = jnp.transpose(cache_kv, (0, 1, 3, 4, 5, 2)).reshape(depth * n_phys, 512, page)
    cwin_t = jnp.transpose(cache_win, (0, 1, 3, 4, 5, 2)).reshape(depth * n_seq, 256, win_rows)
    pt_flat = page_table.reshape(-1)

    def tm_state(st):
        k = st.shape[2]
        st = jnp.swapaxes(st, 1, 2)
        st = jnp.concatenate([jnp.zeros((depth, CARRY_STEPS - k, n_seq, 256), F32), st], axis=1)
        return st.reshape(depth, CARRY_STEPS * n_seq, 256)

    cst_s, pst_s = tm_state(state_conv), tm_state(state_pool)
    zst_p = jnp.zeros((n_b, CARRY_STEPS, 256), F32)

    tm_p = 512
    tm_row = 512
    seqs_step = 4 if n_seq % 4 == 0 else 1

    xp = x_prompt
    xs = x_sample.reshape(n_seq * t_dec, d_model)
    outs = {k: [] for k in ("kv_p", "kv_s", "win_p", "win_s", "conv_p", "conv_s", "pool_p", "pool_s")}
    for l in range(depth):
        qq, gates, ocp, rows, kvt, wint, kvtb, cnew, pnew = _proj_call(
            xp, wm_p, wk_t, tab_p, tab_pt, cw8, pw_bd, ps3, zst_p, zst_p,
            l=l, tm=tm_p, rs=1, pos0=0, rows_dtype=F32, rows_rope=())
        kc, vc = _compress_rows_call(rows, pe2, wa, wb, w2, l=l)
        oa = _attn_prompt_call(qq, gates, kc, vc, kvtb, et_p, ov_p)
        n_p = n_b * t_len
        x1, comb = _mix_call(oa.reshape(n_p, 512), ocp.reshape(n_p, 512), xp.reshape(n_p, d_model),
                             wo_perm, g1, b1, wr, br, l=l, alpha=alpha, tm=tm_row)
        xp = _moe_call(x1, comb, wgu, wd, g2, b2, l=l, alpha=alpha, tm=tm_row).reshape(n_b, t_len, d_model)
        outs["kv_p"].append(jnp.transpose(kvt.reshape(n_b, 4, 2, HEAD_DIM, t_len), (0, 4, 1, 2, 3)))
        outs["win_p"].append(jnp.transpose(wint[:, :, t_len - WINDOW:].reshape(n_b, 2, 2, HEAD_DIM, WINDOW),
                                           (0, 4, 1, 2, 3)))
        outs["conv_p"].append(cnew[:, CARRY_STEPS - (CONV_K - 1):])
        outs["pool_p"].append(pnew[:, CARRY_STEPS - POOL_STATE:])

        xs_tm = jnp.swapaxes(xs.reshape(n_seq, t_dec, d_model), 0, 1).reshape(1, t_dec * n_seq, d_model)
        qq, gates, ocp, rows, kvt, wint, kvtb, cnew, pnew = _proj_call(
            xs_tm, wm_s, wk_t, tab_s, tab_st, cw8, pw_bd, ps3, cst_s[l][None], pst_s[l][None],
            l=l, tm=t_dec * n_seq, rs=n_seq, pos0=past, rows_dtype=BF16, rows_rope=(0, 2))
        seq_major = lambda a: jnp.swapaxes(a.reshape(t_dec, n_seq, a.shape[-1]), 0, 1)
        kc, vc = _compress_pages_call(pt_flat, cache_t, pe2, wa, wb, w2, l=l, n_seq=n_seq, n_pages=n_pages,
                                      n_phys=n_phys, seqs_step=seqs_step)
        kc = kc.reshape(n_seq, nc_s, 128)
        vc = vc.reshape(n_seq, nc_s, 128)
        oa = _attn_sample_call(pt_flat, cache_t, seq_major(qq), seq_major(gates), kc, vc, seq_major(rows),
                               cwin_t, es_s, ov_s, l=l, n_pages=n_pages, n_phys=n_phys, past=past)
        n_s = n_seq * t_dec
        tm_s = min(tm_row, n_s)
        x1, comb = _mix_call(oa.reshape(n_s, 512), seq_major(ocp).reshape(n_s, 512), xs,
                             wo_perm, g1, b1, wr, br, l=l, alpha=alpha, tm=tm_s)
        xs = _moe_call(x1, comb, wgu, wd, g2, b2, l=l, alpha=alpha, tm=tm_s)
        kvt5 = kvt.reshape(4, 2, HEAD_DIM, t_dec, n_seq)
        outs["kv_s"].append(jnp.transpose(kvt5, (4, 3, 0, 1, 2)))
        new_w = jnp.transpose(wint.reshape(2, 2, HEAD_DIM, t_dec, n_seq), (4, 3, 0, 1, 2))
        outs["win_s"].append(jnp.concatenate([cache_win[l], new_w], axis=1)[:, -WINDOW:])
        st_sm = lambda a, k: jnp.swapaxes(a.reshape(CARRY_STEPS, n_seq, 256)[CARRY_STEPS - k:], 0, 1)
        outs["conv_s"].append(st_sm(cnew, CONV_K - 1))
        outs["pool_s"].append(st_sm(pnew, POOL_STATE))

    st = lambda k: jnp.stack(outs[k])
    return (xp, xs.reshape(n_seq, t_dec, d_model), st("kv_p"), st("kv_s"), st("win_p"), st("win_s"),
            st("conv_p"), st("conv_s"), st("pool_p"), st("pool_s"))
```

```python
import functools

import numpy as np
import jax
import jax.numpy as jnp
from jax import lax
from jax.experimental import pallas as pl
from jax.experimental.pallas import tpu as pltpu

F32 = jnp.float32
BF16 = jnp.bfloat16

HEAD_DIM = 64
N_HEADS = 8
N_KV_HEADS = 2
GQA_REP = N_HEADS // N_KV_HEADS
ATT_WIDTH = N_HEADS * HEAD_DIM
KV_W = N_KV_HEADS * HEAD_DIM
ROT_DIM = HEAD_DIM // 4
ROPE_THETA = 500000.0
CMP_BLOCK = 32
CMP_STRIDE = 16
CMP_HIDDEN = 256
SEL_BLOCK = 64
SEL_TOP_K = 16
WINDOW = 512
Q_BLOCK = 128
CONV_K = 3
POOL_WINDOWS = (2, 4, 8, 16)
POOL_STATE = max(POOL_WINDOWS) - 1
CARRY_STEPS = 16
N_EXPERT_GROUPS = 4
EXPERTS_PER_GROUP = 8
N_EXPERTS = N_EXPERT_GROUPS * EXPERTS_PER_GROUP
D_EXPERT = 256
LN_EPS = 1e-5
NEG = -1e30
FORCE_BONUS = 1e4
LANES = 128
VMEM_LIMIT = 56 * 1024 * 1024

_C_Q, _C_KV, _C_GL, _C_BG = 0, 512, 1280, 1304
_MAIN_W = 1664


def _cparams(sem):
    return pltpu.CompilerParams(dimension_semantics=sem, vmem_limit_bytes=VMEM_LIMIT)


def _dot(a, b):
    return jnp.dot(a, b, preferred_element_type=F32)


def _dot_nt(a, b):
    return lax.dot_general(a, b, (((1,), (1,)), ((), ())), preferred_element_type=F32)


def _layer_norm(y, g, b):
    mu = jnp.mean(y, axis=-1, keepdims=True)
    d = y - mu
    var = jnp.mean(d * d, axis=-1, keepdims=True)
    return d * lax.rsqrt(var + LN_EPS) * g + b


def _rope_rows(v, tab):
    return (v * tab[:, 0:128] + pltpu.roll(v, 120, 1) * tab[:, 128:256]
            + pltpu.roll(v, 8, 1) * tab[:, 256:384])


def _rope_cols(v, tab):
    return (v * tab[0:128] + pltpu.roll(v, 120, 0) * tab[128:256]
            + pltpu.roll(v, 8, 0) * tab[256:384])


def _proj_kernel(x_ref, wm_ref, wk_ref, tabr_ref, tabt_ref, cw_ref, pw_ref, ps_ref, cst_ref, pst_ref,
                 qq_ref, gate_ref, ocp_ref, rows_ref, kvt_ref, wint_ref, kvtb_ref, cnew_ref, pnew_ref,
                 cu_sc, cp_sc, *, tm, rs, pos0, rows_rope):
    ti = pl.program_id(1)
    carry = CARRY_STEPS * rs

    @pl.when(ti == 0)
    def _():
        cu_sc[...] = cst_ref[0]
        cp_sc[...] = pst_ref[0]

    xb = x_ref[0].astype(BF16)
    h = _dot(xb, wm_ref[...])
    kvt = _dot_nt(wk_ref[...], xb)
    tabr = tabr_ref[...]
    tabt = tabt_ref[...]

    scale = HEAD_DIM ** -0.5
    for c in range(4):
        qc = h[:, c * 128:(c + 1) * 128]
        qq_ref[0, :, c * 128:(c + 1) * 128] = (qc * scale).astype(BF16)
        qq_ref[0, :, 512 + c * 128:512 + (c + 1) * 128] = (_rope_rows(qc, tabr) * scale).astype(BF16)
    gate_ref[0] = 1.0 / (1.0 + jnp.exp(-h[:, 1536:1664]))

    for c in range(6):
        blk = kvt[c * 128:(c + 1) * 128]
        if c in (2, 4):
            blk = _rope_cols(blk, tabt)
        if c < 4:
            kvt_ref[0, c * 128:(c + 1) * 128, :] = blk
        else:
            wint_ref[0, (c - 4) * 128:(c - 3) * 128, :] = blk
        kvtb_ref[0, c * 128:(c + 1) * 128, :] = blk.astype(BF16)

    n_rows = rows_ref.shape[2]
    for c in range(n_rows // 128):
        blk = h[:, _MAIN_W + c * 128:_MAIN_W + (c + 1) * 128]
        if c in rows_rope:
            blk = _rope_rows(blk, tabr)
        rows_ref[0, :, c * 128:(c + 1) * 128] = blk.astype(rows_ref.dtype)

    bg = h[:, 512:768]
    cg = h[:, 768:1024]
    vc = h[:, 1024:1280]
    pin = h[:, 1280:1536]

    u = cg * vc
    eu = jnp.concatenate([cu_sc[...], u], axis=0)
    cw = cw_ref[...]
    y = (eu[carry - 2 * rs:carry - 2 * rs + tm] * cw[0:1] + eu[carry - rs:carry - rs + tm] * cw[1:2]
         + u * cw[2:3])
    o_conv = bg * y
    new_u = eu[tm:tm + carry]
    cu_sc[...] = new_u
    cnew_ref[0] = new_u

    ep = jnp.concatenate([cp_sc[...], pin], axis=0)
    s2 = ep[rs:] + ep[:-rs]
    s4 = s2[2 * rs:] + s2[:-2 * rs]
    s8 = s4[4 * rs:] + s4[:-4 * rs]
    s16 = s8[8 * rs:] + s8[:-8 * rs]
    take = lambda a: a[a.shape[0] - tm:]
    lane = lax.broadcasted_iota(jnp.int32, (tm, 256), 1)
    row = lax.broadcasted_iota(jnp.int32, (tm, 256), 0)
    dsum = jnp.where(lane < 64, take(s2), jnp.where(lane < 128, take(s4), jnp.where(lane < 192, take(s8), take(s16))))
    wlen = jnp.where(lane < 64, 2, jnp.where(lane < 128, 4, jnp.where(lane < 192, 8, 16)))
    step = pos0 + (ti * tm + row) // rs
    cnt = jnp.minimum(wlen, step + 1).astype(F32)
    d = dsum / cnt - pin
    o_pool = _dot(d.astype(BF16), pw_ref[...]) * ps_ref[...]
    new_p = ep[tm:tm + carry]
    cp_sc[...] = new_p
    pnew_ref[0] = new_p

    ocp_ref[0, :, 0:256] = o_conv.astype(BF16)
    ocp_ref[0, :, 256:512] = o_pool.astype(BF16)


def _proj_call(x3, wm, wk, tabr, tabt, cw, pw, ps, cst, pst, *, l, tm, rs, pos0, rows_dtype, rows_rope):
    n_sg, n_rows, d_model = x3.shape
    tiles = n_rows // tm
    n_main = wm.shape[2]
    n_rowcols = n_main - _MAIN_W
    carry = CARRY_STEPS * rs
    kern = functools.partial(_proj_kernel, tm=tm, rs=rs, pos0=pos0, rows_rope=rows_rope)
    out_shape = (
        jax.ShapeDtypeStruct((n_sg, n_rows, 1024), BF16),
        jax.ShapeDtypeStruct((n_sg, n_rows, 128), F32),
        jax.ShapeDtypeStruct((n_sg, n_rows, 512), BF16),
        jax.ShapeDtypeStruct((n_sg, n_rows, n_rowcols), rows_dtype),
        jax.ShapeDtypeStruct((n_sg, 512, n_rows), F32),
        jax.ShapeDtypeStruct((n_sg, 256, n_rows), F32),
        jax.ShapeDtypeStruct((n_sg, 768, n_rows), BF16),
        jax.ShapeDtypeStruct((n_sg, carry, 256), F32),
        jax.ShapeDtypeStruct((n_sg, carry, 256), F32),
    )
    row_blk = lambda w: pl.BlockSpec((1, tm, w), lambda s, t: (s, t, 0))
    col_blk = lambda w: pl.BlockSpec((1, w, tm), lambda s, t: (s, 0, t))
    st_blk = pl.BlockSpec((1, carry, 256), lambda s, t: (s, 0, 0))
    return pl.pallas_call(
        kern,
        grid=(n_sg, tiles),
        in_specs=[
            row_blk(d_model),
            pl.BlockSpec((None, d_model, n_main), lambda s, t: (l, 0, 0)),
            pl.BlockSpec((None, 768, d_model), lambda s, t: (l, 0, 0)),
            pl.BlockSpec((tm, 384), lambda s, t: (t, 0)),
            pl.BlockSpec((384, tm), lambda s, t: (0, t)),
            pl.BlockSpec((None, 8, 256), lambda s, t: (l, 0, 0)),
            pl.BlockSpec((None, 256, 256), lambda s, t: (l, 0, 0)),
            pl.BlockSpec((None, 1, 256), lambda s, t: (l, 0, 0)),
            st_blk, st_blk,
        ],
        out_specs=(row_blk(1024), row_blk(128), row_blk(512), row_blk(n_rowcols),
                   col_blk(512), col_blk(256), col_blk(768), st_blk, st_blk),
        out_shape=out_shape,
        scratch_shapes=[pltpu.VMEM((carry, 256), F32), pltpu.VMEM((carry, 256), F32)],
        compiler_params=_cparams(("arbitrary", "arbitrary")),
        name="proj",
    )(x3, wm, wk, tabr, tabt, cw, pw, ps, cst, pst)


def _gelu_tanh(x):
    return 0.5 * x * (1.0 + jnp.tanh(0.7978845608028654 * (x + 0.044715 * x * x * x)))


def _compress_core(read_rows, n, pe_ref, wa_ref, wb_ref, w2_ref, kc_ref, vc_ref):
    for kv, out_ref in ((0, kc_ref), (1, vc_ref)):
        cols = [read_rows(kv, r) for r in range(CMP_STRIDE)]
        xa = jnp.concatenate([cols[r] + pe_ref[kv, 0, r:r + 1, :] for r in range(CMP_STRIDE)], axis=1)
        xb = jnp.concatenate([cols[r] + pe_ref[kv, 1, r:r + 1, :] for r in range(CMP_STRIDE)], axis=1)
        a = _dot(xa.astype(BF16), wa_ref[kv])
        b = _dot(xb.astype(BF16), wb_ref[kv])
        pre = a + pltpu.roll(b, n - 1, 0)
        hid = _gelu_tanh(pre)
        out_ref[0] = _dot(hid.astype(BF16), w2_ref[kv]).astype(BF16)


def _compress_rows_kernel(k_ref, v_ref, pe_ref, wa_ref, wb_ref, w2_ref, kc_ref, vc_ref, *, n):
    srcs = (k_ref, v_ref)
    read = lambda kv, r: srcs[kv][0, pl.ds(r, n, stride=CMP_STRIDE), :]
    _compress_core(read, n, pe_ref, wa_ref, wb_ref, w2_ref, kc_ref, vc_ref)


def _compress_pages_kernel(pt_ref, *refs, n_pages_step, page, n):
    pages = refs[:n_pages_step]
    pe_ref, wa_ref, wb_ref, w2_ref, kc_ref, vc_ref, kbuf, vbuf = refs[n_pages_step:]
    for i, pg in enumerate(pages):
        kbuf[i * page:(i + 1) * page, :] = pg[0:128, :].T
        vbuf[i * page:(i + 1) * page, :] = pg[128:256, :].T
    bufs = (kbuf, vbuf)
    read = lambda kv, r: bufs[kv][pl.ds(r, n, stride=CMP_STRIDE), :]
    _compress_core(read, n, pe_ref, wa_ref, wb_ref, w2_ref, kc_ref, vc_ref)


def _cmp_weight_specs(l, nidx):
    im4 = (lambda *a: (l, 0, 0, 0))
    im5 = (lambda *a: (l, 0, 0, 0, 0))
    return [
        pl.BlockSpec((None, 2, 2, CMP_STRIDE, 128), im5),
        pl.BlockSpec((None, 2, 2048, 512), im4),
        pl.BlockSpec((None, 2, 2048, 512), im4),
        pl.BlockSpec((None, 2, 512, 128), im4),
    ]


def _compress_rows_call(rows, pe, wa, wb, w2, *, l):
    b, t, _ = rows.shape
    n = t // CMP_STRIDE
    out = jax.ShapeDtypeStruct((b, n, 128), BF16)
    return pl.pallas_call(
        functools.partial(_compress_rows_kernel, n=n),
        grid=(b,),
        in_specs=[pl.BlockSpec((1, t, 128), lambda i: (i, 0, 0)),
                  pl.BlockSpec((1, t, 128), lambda i: (i, 0, 1))] + _cmp_weight_specs(l, 1),
        out_specs=(pl.BlockSpec((1, n, 128), lambda i: (i, 0, 0)),) * 2,
        out_shape=(out, out),
        compiler_params=_cparams(("arbitrary",)),
        name="compress_rows",
    )(rows, rows, pe, wa, wb, w2)


def _compress_pages_call(pt_flat, cache_t, pe, wa, wb, w2, *, l, n_seq, n_pages, n_phys, seqs_step):
    page = cache_t.shape[2]
    n_pages_step = seqs_step * n_pages
    n = n_pages_step * page // CMP_STRIDE
    steps = n_seq // seqs_step

    def page_spec(k):
        s, p = divmod(k, n_pages)
        return pl.BlockSpec((None, 256, page),
                            lambda i, pt: (l * n_phys + pt[(i * seqs_step + s) * n_pages + p], 0, 0))

    out = jax.ShapeDtypeStruct((steps, n, 128), BF16)
    return pl.pallas_call(
        functools.partial(_compress_pages_kernel, n_pages_step=n_pages_step, page=page, n=n),
        grid_spec=pltpu.PrefetchScalarGridSpec(
            num_scalar_prefetch=1,
            grid=(steps,),
            in_specs=[page_spec(k) for k in range(n_pages_step)] + _cmp_weight_specs(l, 2),
            out_specs=(pl.BlockSpec((1, n, 128), lambda i, pt: (i, 0, 0)),) * 2,
            scratch_shapes=[pltpu.VMEM((n_pages_step * page, 128), F32)] * 2,
        ),
        out_shape=(out, out),
        compiler_params=_cparams(("arbitrary",)),
        name="compress_pages",
    )(pt_flat, *([cache_t] * n_pages_step), pe, wa, wb, w2)


def _q_rows(q, tq):
    lo = lax.broadcasted_iota(jnp.int32, (tq, 128), 1) < 64
    cols = [q[:, c * 128:(c + 1) * 128] for c in range(4)]
    return jnp.concatenate([jnp.where(lo, c, 0.0) for c in cols] + [jnp.where(lo, 0.0, c) for c in cols], axis=0)


def _pair_cols(o, tq):
    lo = lax.broadcasted_iota(jnp.int32, (tq, 128), 1) < 64
    return [jnp.where(lo, o[c * tq:(c + 1) * tq], o[(4 + c) * tq:(5 + c) * tq]) for c in range(4)]


def _row_pos(tq, n, pos_base):
    row = lax.broadcasted_iota(jnp.int32, (8 * tq, n), 0)
    return pos_base + (row & (tq - 1))


def _compressed_branch(qc, kc, vc, ov, tq, pos_base):
    nc = kc.shape[0]
    s = _dot_nt(qc, kc)
    pos = _row_pos(tq, nc, pos_base)
    cend = lax.broadcasted_iota(jnp.int32, (8 * tq, nc), 1) * CMP_STRIDE + (CMP_BLOCK - 1)
    vis = cend <= pos
    s = jnp.where(vis, s, NEG)
    e = jnp.where(vis, jnp.exp(s - jnp.max(s, axis=-1, keepdims=True)), 0.0)
    den = jnp.sum(e, axis=-1, keepdims=True)
    p = (e / jnp.where(den > 0.0, den, 1.0)).astype(BF16)
    o = _dot(p, vc)
    imp8 = _dot(p, ov)
    imp = jnp.concatenate([imp8[(4 * g) * tq:(4 * g + 1) * tq] + imp8[(4 * g + 1) * tq:(4 * g + 2) * tq]
                           + imp8[(4 * g + 2) * tq:(4 * g + 3) * tq] + imp8[(4 * g + 3) * tq:(4 * g + 4) * tq]
                           for g in range(2)], axis=0)
    return o, imp


def _select_blocks(imp, tq, pos_base, n_sel):
    row = lax.broadcasted_iota(jnp.int32, (2 * tq, 128), 0)
    blk = lax.broadcasted_iota(jnp.int32, (2 * tq, 128), 1)
    pos = pos_base + (row & (tq - 1))
    cur = pos // SEL_BLOCK
    forced = (blk == 0) | (blk == cur) | (blk == cur - 1)
    valid = blk * SEL_BLOCK <= pos
    score = jnp.where(valid, imp + jnp.where(forced, FORCE_BONUS, 0.0), NEG)

    if 2 * tq >= 128:
        ns8 = -(-n_sel // 8) * 8
        st = score.T[0:ns8]
        sub = lax.broadcasted_iota(jnp.int32, st.shape, 0)
        rank = jnp.zeros(st.shape, F32)
        for b in range(n_sel):
            other = st[b:b + 1, :]
            rank = rank + jnp.where((other > st) | ((other == st) & (sub > b)), 1.0, 0.0)
        keep = jnp.where((rank < SEL_TOP_K) & (sub < n_sel), 1.0, 0.0)
        keep = jnp.concatenate([keep, jnp.zeros((128 - ns8, 2 * tq), F32)], axis=0)
        return keep.T
    rank = jnp.zeros(score.shape, F32)
    for b in range(n_sel):
        other = score[:, b:b + 1]
        rank = rank + jnp.where((other > score) | ((other == score) & (blk > b)), 1.0, 0.0)
    return jnp.where((rank < SEL_TOP_K) & (blk < n_sel), 1.0, 0.0)


def _softmax_pv(pieces):
    m = None
    for s, _, _ in pieces:
        mi = jnp.max(s, axis=-1, keepdims=True)
        m = mi if m is None else jnp.maximum(m, mi)
    den = 0.0
    acc = 0.0
    for s, v, fm in pieces:
        p = jnp.exp(s - m)
        den = den + jnp.sum(p, axis=-1, keepdims=True)
        pb = p.astype(BF16)
        acc = acc + (_dot_nt(pb, v) if fm else _dot(pb, v))
    return acc / den


def _gated_sum(branches, gates, tq):
    lo = lax.broadcasted_iota(jnp.int32, (tq, 128), 1) < 64
    cols = [_pair_cols(o, tq) for o in branches]
    out = []
    for c in range(4):
        acc = 0.0
        for n in range(3):
            ga = gates[:, 3 * c + n:3 * c + n + 1]
            gb = gates[:, 3 * (c + 4) + n:3 * (c + 4) + n + 1]
            acc = acc + cols[n][c] * jnp.where(lo, ga, gb)
        out.append(acc)
    return out


def _attn_prompt_kernel(qq_ref, gate_ref, kc_ref, vc_ref, kst_ref, vst_ref, kwt_ref, vwt_ref, et_ref, ov_ref,
                        o_ref, kaug_sc, m_sc, l_sc, acc_sc, *, tq, t_len, tk):
    j = pl.program_id(1)
    rows = 8 * tq

    @pl.when(j == 0)
    def _():
        kaug_sc[0:128, :] = kst_ref[0]
        kaug_sc[128:256, :] = et_ref[...]

    pos_base = j * tq
    qq = qq_ref[0].astype(F32)
    qc = _q_rows(qq[:, 0:512], tq).astype(BF16)
    qs = _q_rows(qq[:, 512:1024], tq).astype(BF16)

    o_cmp, imp = _compressed_branch(qc, kc_ref[0], vc_ref[0], ov_ref[...], tq, pos_base)
    sel = _select_blocks(imp, tq, pos_base, -(-t_len // SEL_BLOCK))
    bias = jnp.where(sel > 0.5, 0.0, NEG).astype(BF16)
    bias_rows = jnp.concatenate([bias[0:tq]] * 4 + [bias[tq:2 * tq]] * 4, axis=0)
    q_aug = jnp.concatenate([qs, bias_rows], axis=1)

    m_sc[...] = jnp.full((rows, 128), NEG, F32)
    l_sc[...] = jnp.zeros((rows, 128), F32)
    acc_sc[...] = jnp.zeros((rows, 128), F32)

    def update(k0, causal):
        s = _dot(q_aug, kaug_sc[:, pl.ds(k0, tk)])
        if causal:
            kpos = k0 + lax.broadcasted_iota(jnp.int32, (rows, tk), 1)
            s = jnp.where(kpos <= _row_pos(tq, tk, pos_base), s, NEG)
        m_old = m_sc[...]
        m_new = jnp.maximum(m_old, jnp.max(s, axis=-1, keepdims=True))
        alpha = jnp.exp(m_old - m_new)
        p = jnp.exp(s - jnp.concatenate([m_new] * (tk // 128), axis=1))
        l_sc[...] = alpha * l_sc[...] + jnp.sum(p, axis=-1, keepdims=True)
        acc_sc[...] = alpha * acc_sc[...] + _dot_nt(p.astype(BF16), vst_ref[0, :, pl.ds(k0, tk)])
        m_sc[...] = m_new

    n_bulk = (j * tq) // tk

    def bulk(kt, c):
        update(pl.multiple_of(kt * tk, tk), False)
        return c

    lax.fori_loop(0, n_bulk, bulk, 0)
    update(pl.multiple_of(n_bulk * tk, tk), True)
    o_slc = acc_sc[...] / l_sc[...]

    wk = WINDOW + tq
    k0 = pl.multiple_of(jnp.maximum(j * tq - WINDOW, 0), 128)
    s = _dot(qs, kwt_ref[0, :, pl.ds(k0, wk)])
    dist = _row_pos(tq, wk, pos_base) - (k0 + lax.broadcasted_iota(jnp.int32, (rows, wk), 1))
    s = jnp.where((dist >= 0) & (dist < WINDOW), s, NEG)
    o_win = _softmax_pv([(s, vwt_ref[0, :, pl.ds(k0, wk)], True)])

    cols = _gated_sum([o_cmp, o_slc, o_win], gate_ref[0], tq)
    for c in range(4):
        o_ref[0, :, c * 128:(c + 1) * 128] = cols[c].astype(BF16)


def _attn_prompt_call(qq, gates, kc, vc, kvtb, et, ov):
    b, t_len, _ = qq.shape
    tq = Q_BLOCK
    tk = 256
    nc = kc.shape[1]
    kern = functools.partial(_attn_prompt_kernel, tq=tq, t_len=t_len, tk=tk)
    kv_blk = lambda c: pl.BlockSpec((1, 128, t_len), lambda i, j: (i, c, 0))
    return pl.pallas_call(
        kern,
        grid=(b, t_len // tq),
        in_specs=[
            pl.BlockSpec((1, tq, 1024), lambda i, j: (i, j, 0)),
            pl.BlockSpec((1, tq, 128), lambda i, j: (i, j, 0)),
            pl.BlockSpec((1, nc, 128), lambda i, j: (i, 0, 0)),
            pl.BlockSpec((1, nc, 128), lambda i, j: (i, 0, 0)),
            kv_blk(2), kv_blk(3), kv_blk(4), kv_blk(5),
            pl.BlockSpec((128, t_len), lambda i, j: (0, 0)),
            pl.BlockSpec((nc, 128), lambda i, j: (0, 0)),
        ],
        out_specs=pl.BlockSpec((1, tq, 512), lambda i, j: (i, j, 0)),
        out_shape=jax.ShapeDtypeStruct((b, t_len, 512), BF16),
        scratch_shapes=[pltpu.VMEM((256, t_len), BF16), pltpu.VMEM((8 * tq, 128), F32),
                        pltpu.VMEM((8 * tq, 128), F32), pltpu.VMEM((8 * tq, 128), F32)],
        compiler_params=_cparams(("arbitrary", "arbitrary")),
        name="attn_prompt",
    )(qq, gates, kc, vc, kvtb, kvtb, kvtb, kvtb, et, ov)


def _attn_sample_kernel(pt_ref, *refs, n_pages, page, tq, past, win_rows):
    pages = refs[:n_pages]
    qq_ref, gate_ref, kc_ref, vc_ref, new_ref, cwin_ref, neww_ref, es_ref, ov_ref = refs[n_pages:n_pages + 9]
    o_ref, wout_ref = refs[-2:]
    rows = 8 * tq
    pos_base = past

    qq = qq_ref[0].astype(F32)
    qc = _q_rows(qq[:, 0:512], tq).astype(BF16)
    qs = _q_rows(qq[:, 512:1024], tq).astype(BF16)

    o_cmp, imp = _compressed_branch(qc, kc_ref[0], vc_ref[0], ov_ref[...], tq, pos_base)
    sel = _select_blocks(imp, tq, pos_base, -(-(past + tq) // SEL_BLOCK))

    new = jnp.concatenate([new_ref[0].astype(F32), jnp.zeros((128 - tq, 512), F32)], axis=0).astype(BF16)

    kst = jnp.concatenate([pg[0:128, :].astype(BF16) for pg in pages], axis=1)
    vst = jnp.concatenate([pg[128:256, :].astype(BF16) for pg in pages], axis=1)
    n_keys = past + 128
    member = _dot(sel.astype(BF16), es_ref[...])
    member = jnp.concatenate([member[0:tq]] * 4 + [member[tq:2 * tq]] * 4, axis=0)
    kpos = lax.broadcasted_iota(jnp.int32, (rows, n_keys), 1)
    ok = (member > 0.5) & (kpos <= _row_pos(tq, n_keys, pos_base))
    s_past = jnp.where(ok[:, 0:past], _dot(qs, kst), NEG)
    s_new = jnp.where(ok[:, past:], _dot_nt(qs, new[:, 0:128]), NEG)
    o_slc = _softmax_pv([(s_past, vst, True), (s_new, new[:, 128:256], False)])

    cw = cwin_ref[0]
    n_wk = win_rows + 128
    kpos_w = (past - win_rows) + lax.broadcasted_iota(jnp.int32, (rows, n_wk), 1)
    dist = _row_pos(tq, n_wk, pos_base) - kpos_w
    okw = (dist >= 0) & (dist < WINDOW)
    s_old = jnp.where(okw[:, 0:win_rows], _dot(qs, cw[0:128].astype(BF16)), NEG)
    s_nw = jnp.where(okw[:, win_rows:], _dot_nt(qs, new[:, 256:384]), NEG)
    o_win = _softmax_pv([(s_old, cw[128:256].astype(BF16), True), (s_nw, new[:, 384:512], False)])

    cols = _gated_sum([o_cmp, o_slc, o_win], gate_ref[0], tq)
    for c in range(4):
        o_ref[0, :, c * 128:(c + 1) * 128] = cols[c].astype(BF16)

    wout_ref[0] = pltpu.roll(cw, win_rows - tq, 1)
    wout_ref[0, :, win_rows - tq:] = neww_ref[0]


def _attn_sample_call(pt_flat, cache_t, qq, gates, kc, vc, new_rows, cwin_t, new_win_t, es, ov, win_prev, *,
                      l, depth, n_pages, n_phys, past):
    n_seq, tq, _ = qq.shape
    page = cache_t.shape[2]
    win_rows = cwin_t.shape[2]
    nc = kc.shape[1]
    kern = functools.partial(_attn_sample_kernel, n_pages=n_pages, page=page, tq=tq, past=past, win_rows=win_rows)

    def page_spec(p):
        return pl.BlockSpec((None, 256, page), lambda i, pt: (l * n_phys + pt[i * n_pages + p], 1, 0))

    seq_blk = lambda w: pl.BlockSpec((1, tq, w), lambda i, pt: (i, 0, 0))
    win_blk = pl.BlockSpec((1, 256, win_rows), lambda i, pt: (l * n_seq + i, 0, 0))
    in_specs = [page_spec(p) for p in range(n_pages)] + [
        seq_blk(1024), seq_blk(128),
        pl.BlockSpec((1, nc, 128), lambda i, pt: (i, 0, 0)),
        pl.BlockSpec((1, nc, 128), lambda i, pt: (i, 0, 0)),
        seq_blk(512),
        win_blk,
        pl.BlockSpec((1, 256, tq), lambda i, pt: (i, 0, 0)),
        pl.BlockSpec((128, past + 128), lambda i, pt: (0, 0)),
        pl.BlockSpec((nc, 128), lambda i, pt: (0, 0)),
    ]
    operands = [pt_flat] + [cache_t] * n_pages + [qq, gates, kc, vc, new_rows, cwin_t, new_win_t, es, ov]
    aliases = {}
    if win_prev is not None:
        in_specs.append(pl.BlockSpec(memory_space=pl.ANY))
        aliases = {len(operands): 1}
        operands.append(win_prev)
    return pl.pallas_call(
        kern,
        grid_spec=pltpu.PrefetchScalarGridSpec(
            num_scalar_prefetch=1,
            grid=(n_seq,),
            in_specs=in_specs,
            out_specs=(pl.BlockSpec((1, tq, 512), lambda i, pt: (i, 0, 0)), win_blk),
        ),
        out_shape=(jax.ShapeDtypeStruct((n_seq, tq, 512), BF16),
                   jax.ShapeDtypeStruct((depth * n_seq, 256, win_rows), F32)),
        input_output_aliases=aliases,
        compiler_params=_cparams(("arbitrary",)),
        name="attn_sample",
    )(*operands)


def _mix_kernel(oa_ref, ocp_ref, x_ref, wo_ref, g_ref, b_ref, wr_ref, br_ref, x1_ref, comb_ref, *, alpha):
    mix = _dot(oa_ref[...], wo_ref[0:512, :]) + _dot(ocp_ref[...], wo_ref[512:1024, :])
    x1 = _layer_norm(alpha * x_ref[...] + mix, g_ref[...], b_ref[...])
    x1_ref[...] = x1

    tm = x1.shape[0]
    xh = x1.astype(BF16)
    xl = (x1 - xh.astype(F32)).astype(BF16)
    hw = _dot(xh, wr_ref[...])
    logits = hw[:, 0:128] + hw[:, 128:256] + _dot(xl, wr_ref[:, 0:128]) + br_ref[...]
    lane = lax.broadcasted_iota(jnp.int32, (tm, 128), 1).astype(F32)
    big = 1e9
    is_g = (lane >= N_EXPERTS) & (lane < N_EXPERTS + N_EXPERT_GROUPS)
    lg = jnp.where(is_g, logits, NEG)
    ge = jnp.where(is_g, jnp.exp(lg - jnp.max(lg, axis=-1, keepdims=True)), 0.0)
    gp = ge / jnp.sum(ge, axis=-1, keepdims=True)
    gw = jnp.max(gp, axis=-1, keepdims=True)
    gidx = jnp.min(jnp.where(is_g & (gp == gw), lane - N_EXPERTS, big), axis=-1, keepdims=True)
    in_g = (lane >= gidx * EXPERTS_PER_GROUP) & (lane < (gidx + 1.0) * EXPERTS_PER_GROUP)
    le = jnp.where(in_g, logits, NEG)
    ee = jnp.where(in_g, jnp.exp(le - jnp.max(le, axis=-1, keepdims=True)), 0.0)
    ep = ee / jnp.sum(ee, axis=-1, keepdims=True)
    w1 = jnp.max(jnp.where(in_g, ep, -1.0), axis=-1, keepdims=True)
    i1 = jnp.min(jnp.where(in_g & (ep == w1), lane, big), axis=-1, keepdims=True)
    rest = in_g & (lane != i1)
    w2 = jnp.max(jnp.where(rest, ep, -1.0), axis=-1, keepdims=True)
    i2 = jnp.min(jnp.where(rest & (ep == w2), lane, big), axis=-1, keepdims=True)
    den = w1 + w2
    comb_ref[...] = jnp.where(lane == i1, gw * (w1 / den), jnp.where(lane == i2, gw * (w2 / den), 0.0))


def _mix_call(oa, ocp, x, wo, g, b, wr, br, *, l, alpha, tm):
    n, d = x.shape
    row = lambda w: pl.BlockSpec((tm, w), lambda i: (i, 0))
    vec = lambda w: pl.BlockSpec((None, 1, w), lambda i: (l, 0, 0))
    return pl.pallas_call(
        functools.partial(_mix_kernel, alpha=alpha),
        grid=(n // tm,),
        in_specs=[row(512), row(512), row(d),
                  pl.BlockSpec((None, 1024, d), lambda i: (l, 0, 0)), vec(d), vec(d),
                  pl.BlockSpec((None, d, 256), lambda i: (l, 0, 0)), vec(128)],
        out_specs=(row(d), row(128)),
        out_shape=(jax.ShapeDtypeStruct((n, d), F32), jax.ShapeDtypeStruct((n, 128), F32)),
        compiler_params=_cparams(("arbitrary",)),
        name="mix_router",
    )(oa, ocp, x, wo, g, b, wr, br)


def _moe_kernel(x1_ref, comb_ref, wgu_ref, wd_ref, g_ref, b_ref, o_ref, xb_sc, acc_sc, *, alpha):
    e = pl.program_id(1)

    @pl.when(e == 0)
    def _():
        xb_sc[...] = x1_ref[...].astype(BF16)
        acc_sc[...] = jnp.zeros_like(acc_sc)

    hgu = _dot(xb_sc[...], wgu_ref[...])
    hg = hgu[:, 0:D_EXPERT]
    hid = hg * (1.0 / (1.0 + jnp.exp(-hg))) * hgu[:, D_EXPERT:2 * D_EXPERT]
    tm = hid.shape[0]
    lane = lax.broadcasted_iota(jnp.int32, (tm, 128), 1)
    ce = jnp.sum(jnp.where(lane == e, comb_ref[...], 0.0), axis=-1, keepdims=True)
    acc_sc[...] += _dot((hid * ce).astype(BF16), wd_ref[...])

    @pl.when(e == pl.num_programs(1) - 1)
    def _():
        o_ref[...] = _layer_norm(alpha * x1_ref[...] + acc_sc[...], g_ref[...], b_ref[...])


def _moe_call(x1, comb, wgu, wd, g, b, *, l, alpha, tm):
    n, d = x1.shape
    vec = pl.BlockSpec((None, 1, d), lambda i, e: (l, 0, 0))
    return pl.pallas_call(
        functools.partial(_moe_kernel, alpha=alpha),
        grid=(n // tm, N_EXPERTS),
        in_specs=[pl.BlockSpec((tm, d), lambda i, e: (i, 0)),
                  pl.BlockSpec((tm, 128), lambda i, e: (i, 0)),
                  pl.BlockSpec((None, None, d, 2 * D_EXPERT), lambda i, e: (l, e, 0, 0)),
                  pl.BlockSpec((None, None, D_EXPERT, d), lambda i, e: (l, e, 0, 0)),
                  vec, vec],
        out_specs=pl.BlockSpec((tm, d), lambda i, e: (i, 0)),
        out_shape=jax.ShapeDtypeStruct((n, d), F32),
        scratch_shapes=[pltpu.VMEM((tm, d), BF16), pltpu.VMEM((tm, d), F32)],
        compiler_params=_cparams(("arbitrary", "arbitrary")),
        name="moe",
    )(x1, comb, wgu, wd, g, b)


def _rope_table(pos):
    half = ROT_DIM // 2
    inv = ROPE_THETA ** (-jnp.arange(0, ROT_DIM, 2, dtype=F32) / ROT_DIM)
    ang = jnp.asarray(pos, F32)[:, None] * inv[None, :]
    cos, sin = jnp.cos(ang), jnp.sin(ang)
    n = ang.shape[0]
    z = jnp.zeros((n, HEAD_DIM - ROT_DIM), F32)
    zh = jnp.zeros((n, half), F32)
    c64 = jnp.concatenate([cos, cos, jnp.ones_like(z)], axis=1)
    s1 = jnp.concatenate([-sin, zh, z], axis=1)
    s2 = jnp.concatenate([zh, sin, z], axis=1)
    return jnp.concatenate([c64, c64, s1, s1, s2, s2], axis=1)


def _overlap_table(n_rows):
    c = np.arange(n_rows)[:, None] * CMP_STRIDE
    s = np.arange(128)[None, :] * SEL_BLOCK
    return jnp.asarray(((c < s + SEL_BLOCK) & (c + CMP_BLOCK > s)).astype(np.float32), BF16)


def _block_indicator(n_keys):
    k = np.arange(n_keys)[None, :] // SEL_BLOCK
    return jnp.asarray((np.arange(128)[:, None] == k).astype(np.float32), BF16)


def _block_diag2(w):
    z = jnp.zeros_like(w)
    return jnp.concatenate([jnp.concatenate([w, z], axis=-1), jnp.concatenate([z, w], axis=-1)], axis=-2)


def kernel(x_prompt, x_sample, cache_kv, cache_win, state_conv, state_pool, page_table, ln1_g, ln1_b, w_in, pe_cmp, w_cmp1, w_cmp2, conv_w, pool_w, pool_scale, w_o, ln2_g, ln2_b, w_rg, b_rg, w_re, b_re, w_gate, w_up, w_down):
    n_b, t_len, d_model = x_prompt.shape
    n_seq, t_dec, _ = x_sample.shape
    depth = w_in.shape[0]
    n_phys, page = cache_kv.shape[1], cache_kv.shape[2]
    n_pages = page_table.shape[1]
    past = n_pages * page
    win_rows = cache_win.shape[2]
    alpha = float((2 * depth) ** 0.25)
    assert d_model == 1024 and t_len % 512 == 0 and t_dec == 8 and past % SEL_BLOCK == 0 and win_rows == WINDOW

    pair_order = [h for c in range(4) for h in (c, c + 4)]
    w_main = jnp.concatenate([w_in[:, :, h * 64:(h + 1) * 64] for h in pair_order]
                             + [w_in[:, :, _C_BG:], w_in[:, :, _C_GL:_C_BG],
                              jnp.zeros((depth, d_model, 128 - (_C_BG - _C_GL)), F32)], axis=2)
    wm_p = jnp.concatenate([w_main, w_in[:, :, _C_KV:_C_KV + 256]], axis=2).astype(BF16)
    wm_s = jnp.concatenate([w_main, w_in[:, :, _C_KV + 256:_C_KV + 768]], axis=2).astype(BF16)
    wk_t = jnp.swapaxes(w_in[:, :, _C_KV:_C_GL], 1, 2).astype(BF16)
    cw8 = jnp.concatenate([conv_w, jnp.zeros((depth, 8 - CONV_K, 256), F32)], axis=1)
    pw_bd = jnp.zeros((depth, 256, 256), F32)
    for g in range(4):
        pw_bd = pw_bd.at[:, g * 64:(g + 1) * 64, g * 64:(g + 1) * 64].set(pool_w[:, g])
    pw_bd = pw_bd.astype(BF16)
    ps3 = pool_scale[:, None, :]

    w1r = w_cmp1.reshape(depth, 2, CMP_BLOCK, HEAD_DIM, CMP_HIDDEN)
    bd_half = lambda w: _block_diag2(w).reshape(depth, 2, CMP_STRIDE * 128, 2 * CMP_HIDDEN).astype(BF16)
    wa = bd_half(w1r[:, :, :CMP_STRIDE])
    wb = bd_half(w1r[:, :, CMP_STRIDE:])
    w2 = _block_diag2(w_cmp2).astype(BF16)
    pe2 = jnp.concatenate([pe_cmp, pe_cmp], axis=-1).reshape(depth, 2, 2, CMP_STRIDE, 128)

    wo_perm = jnp.concatenate([w_o[:, h * 64:(h + 1) * 64] for h in pair_order] + [w_o[:, ATT_WIDTH:]],
                              axis=1).astype(BF16)
    wr32 = jnp.concatenate([w_re, w_rg, jnp.zeros((depth, d_model, 128 - N_EXPERTS - N_EXPERT_GROUPS), F32)], axis=2)
    wr_hi = wr32.astype(BF16)
    wr = jnp.concatenate([wr_hi, (wr32 - wr_hi.astype(F32)).astype(BF16)], axis=2)
    br = jnp.concatenate([b_re, b_rg, jnp.zeros((depth, 128 - N_EXPERTS - N_EXPERT_GROUPS), F32)], axis=1)[:, None, :]
    wgu = jnp.concatenate([w_gate, w_up], axis=-1).astype(BF16)
    wd = w_down.astype(BF16)
    g1, b1, g2, b2 = ln1_g[:, None, :], ln1_b[:, None, :], ln2_g[:, None, :], ln2_b[:, None, :]

    tab_p = _rope_table(np.arange(t_len))
    tab_s = _rope_table(past + np.repeat(np.arange(t_dec), n_seq))
    tab_pt, tab_st = tab_p.T, tab_s.T
    nc_p = t_len // CMP_STRIDE
    nc_s = past // CMP_STRIDE
    ov_p, ov_s = _overlap_table(nc_p), _overlap_table(nc_s)
    et_p = _block_indicator(t_len)
    es_s = _block_indicator(past + 128)

    cache_t = jnp.transpose(cache_kv, (0, 1, 3, 4, 5, 2)).reshape(depth * n_phys, 512, page)
    cwin_t = jnp.transpose(cache_win, (0, 1, 3, 4, 5, 2)).reshape(depth * n_seq, 256, win_rows)
    pt_flat = page_table.reshape(-1)

    def tm_state(st):
        k = st.shape[2]
        st = jnp.swapaxes(st, 1, 2)
        st = jnp.concatenate([jnp.zeros((depth, CARRY_STEPS - k, n_seq, 256), F32), st], axis=1)
        return st.reshape(depth, CARRY_STEPS * n_seq, 256)

    cst_s, pst_s = tm_state(state_conv), tm_state(state_pool)
    zst_p = jnp.zeros((n_b, CARRY_STEPS, 256), F32)

    tm_p = 512
    tm_row = 512
    seqs_step = 4 if n_seq % 4 == 0 else 1

    xp = x_prompt
    xs = x_sample.reshape(n_seq * t_dec, d_model)
    outs = {k: [] for k in ("kv_p", "kv_s", "win_p", "conv_p", "conv_s", "pool_p", "pool_s")}
    win_s_all = None
    for l in range(depth):
        qq, gates, ocp, rows, kvt, wint, kvtb, cnew, pnew = _proj_call(
            xp, wm_p, wk_t, tab_p, tab_pt, cw8, pw_bd, ps3, zst_p, zst_p,
            l=l, tm=tm_p, rs=1, pos0=0, rows_dtype=F32, rows_rope=())
        kc, vc = _compress_rows_call(rows, pe2, wa, wb, w2, l=l)
        oa = _attn_prompt_call(qq, gates, kc, vc, kvtb, et_p, ov_p)
        n_p = n_b * t_len
        x1, comb = _mix_call(oa.reshape(n_p, 512), ocp.reshape(n_p, 512), xp.reshape(n_p, d_model),
                             wo_perm, g1, b1, wr, br, l=l, alpha=alpha, tm=tm_row)
        xp = _moe_call(x1, comb, wgu, wd, g2, b2, l=l, alpha=alpha, tm=tm_row).reshape(n_b, t_len, d_model)
        outs["kv_p"].append(jnp.transpose(kvt.reshape(n_b, 4, 2, HEAD_DIM, t_len), (0, 4, 1, 2, 3)))
        outs["win_p"].append(jnp.transpose(wint[:, :, t_len - WINDOW:].reshape(n_b, 2, 2, HEAD_DIM, WINDOW),
                                           (0, 4, 1, 2, 3)))
        outs["conv_p"].append(cnew[:, CARRY_STEPS - (CONV_K - 1):])
        outs["pool_p"].append(pnew[:, CARRY_STEPS - POOL_STATE:])

        xs_tm = jnp.swapaxes(xs.reshape(n_seq, t_dec, d_model), 0, 1).reshape(1, t_dec * n_seq, d_model)
        qq, gates, ocp, rows, kvt, wint, kvtb, cnew, pnew = _proj_call(
            xs_tm, wm_s, wk_t, tab_s, tab_st, cw8, pw_bd, ps3, cst_s[l][None], pst_s[l][None],
            l=l, tm=t_dec * n_seq, rs=n_seq, pos0=past, rows_dtype=BF16, rows_rope=(0, 2))
        seq_major = lambda a: jnp.swapaxes(a.reshape(t_dec, n_seq, a.shape[-1]), 0, 1)
        kc, vc = _compress_pages_call(pt_flat, cache_t, pe2, wa, wb, w2, l=l, n_seq=n_seq, n_pages=n_pages,
                                      n_phys=n_phys, seqs_step=seqs_step)
        kc = kc.reshape(n_seq, nc_s, 128)
        vc = vc.reshape(n_seq, nc_s, 128)
        new_win_t = jnp.transpose(wint.reshape(256, t_dec, n_seq), (2, 0, 1))
        oa, win_s_all = _attn_sample_call(pt_flat, cache_t, seq_major(qq), seq_major(gates), kc, vc,
                                          seq_major(rows), cwin_t, new_win_t, es_s, ov_s, win_s_all,
                                          l=l, depth=depth, n_pages=n_pages, n_phys=n_phys, past=past)
        n_s = n_seq * t_dec
        tm_s = min(tm_row, n_s)
        x1, comb = _mix_call(oa.reshape(n_s, 512), seq_major(ocp).reshape(n_s, 512), xs,
                             wo_perm, g1, b1, wr, br, l=l, alpha=alpha, tm=tm_s)
        xs = _moe_call(x1, comb, wgu, wd, g2, b2, l=l, alpha=alpha, tm=tm_s)
        kvt5 = kvt.reshape(4, 2, HEAD_DIM, t_dec, n_seq)
        outs["kv_s"].append(jnp.transpose(kvt5, (4, 3, 0, 1, 2)))
        st_sm = lambda a, k: jnp.swapaxes(a.reshape(CARRY_STEPS, n_seq, 256)[CARRY_STEPS - k:], 0, 1)
        outs["conv_s"].append(st_sm(cnew, CONV_K - 1))
        outs["pool_s"].append(st_sm(pnew, POOL_STATE))

    st = lambda k: jnp.stack(outs[k])
    win_s = jnp.transpose(win_s_all.reshape(depth, n_seq, 2, 2, HEAD_DIM, win_rows), (0, 1, 5, 2, 3, 4))
    return (xp, xs.reshape(n_seq, t_dec, d_model), st("kv_p"), st("kv_s"), st("win_p"), win_s,
            st("conv_p"), st("conv_s"), st("pool_p"), st("pool_s"))
```

```python
import functools

import numpy as np
import jax
import jax.numpy as jnp
from jax import lax
from jax.experimental import pallas as pl
from jax.experimental.pallas import tpu as pltpu

F32 = jnp.float32
BF16 = jnp.bfloat16

HEAD_DIM = 64
N_HEADS = 8
N_KV_HEADS = 2
GQA_REP = N_HEADS // N_KV_HEADS
ATT_WIDTH = N_HEADS * HEAD_DIM
KV_W = N_KV_HEADS * HEAD_DIM
ROT_DIM = HEAD_DIM // 4
ROPE_THETA = 500000.0
CMP_BLOCK = 32
CMP_STRIDE = 16
CMP_HIDDEN = 256
SEL_BLOCK = 64
SEL_TOP_K = 16
WINDOW = 512
Q_BLOCK = 128
CONV_K = 3
POOL_WINDOWS = (2, 4, 8, 16)
POOL_STATE = max(POOL_WINDOWS) - 1
CARRY_STEPS = 16
N_EXPERT_GROUPS = 4
EXPERTS_PER_GROUP = 8
N_EXPERTS = N_EXPERT_GROUPS * EXPERTS_PER_GROUP
D_EXPERT = 256
LN_EPS = 1e-5
NEG = -1e30
FORCE_BONUS = 1e4
LANES = 128
VMEM_LIMIT = 56 * 1024 * 1024

_C_Q, _C_KV, _C_GL, _C_BG = 0, 512, 1280, 1304
_MAIN_W = 1664


def _cparams(sem):
    return pltpu.CompilerParams(dimension_semantics=sem, vmem_limit_bytes=VMEM_LIMIT)


def _dot(a, b):
    return jnp.dot(a, b, preferred_element_type=F32)


def _dot_nt(a, b):
    return lax.dot_general(a, b, (((1,), (1,)), ((), ())), preferred_element_type=F32)


def _layer_norm(y, g, b):
    mu = jnp.mean(y, axis=-1, keepdims=True)
    d = y - mu
    var = jnp.mean(d * d, axis=-1, keepdims=True)
    return d * lax.rsqrt(var + LN_EPS) * g + b


def _rope_rows(v, tab):
    return (v * tab[:, 0:128] + pltpu.roll(v, 120, 1) * tab[:, 128:256]
            + pltpu.roll(v, 8, 1) * tab[:, 256:384])


def _rope_cols(v, tab):
    return (v * tab[0:128] + pltpu.roll(v, 120, 0) * tab[128:256]
            + pltpu.roll(v, 8, 0) * tab[256:384])


def _proj_kernel(x_ref, wm_ref, wk_ref, tabr_ref, tabt_ref, cw_ref, pw_ref, ps_ref, cst_ref, pst_ref,
                 qq_ref, gate_ref, ocp_ref, rows_ref, kvt_ref, wint_ref, kvtb_ref, cnew_ref, pnew_ref,
                 cu_sc, cp_sc, *, tm, rs, pos0, rows_rope):
    ti = pl.program_id(1)
    carry = CARRY_STEPS * rs

    @pl.when(ti == 0)
    def _():
        cu_sc[...] = cst_ref[0]
        cp_sc[...] = pst_ref[0]

    xb = x_ref[0].astype(BF16)
    h = _dot(xb, wm_ref[...])
    kvt = _dot_nt(wk_ref[...], xb)
    tabr = tabr_ref[...]
    tabt = tabt_ref[...]

    scale = HEAD_DIM ** -0.5
    for c in range(4):
        qc = h[:, c * 128:(c + 1) * 128]
        qq_ref[0, :, c * 128:(c + 1) * 128] = (qc * scale).astype(BF16)
        qq_ref[0, :, 512 + c * 128:512 + (c + 1) * 128] = (_rope_rows(qc, tabr) * scale).astype(BF16)
    gate_ref[0] = 1.0 / (1.0 + jnp.exp(-h[:, 1536:1664]))

    for c in range(6):
        blk = kvt[c * 128:(c + 1) * 128]
        if c in (2, 4):
            blk = _rope_cols(blk, tabt)
        if c < 4:
            kvt_ref[0, c * 128:(c + 1) * 128, :] = blk
        else:
            wint_ref[0, (c - 4) * 128:(c - 3) * 128, :] = blk
        kvtb_ref[0, c * 128:(c + 1) * 128, :] = blk.astype(BF16)

    n_rows = rows_ref.shape[2]
    for c in range(n_rows // 128):
        blk = h[:, _MAIN_W + c * 128:_MAIN_W + (c + 1) * 128]
        if c in rows_rope:
            blk = _rope_rows(blk, tabr)
        rows_ref[0, :, c * 128:(c + 1) * 128] = blk.astype(rows_ref.dtype)

    bg = h[:, 512:768]
    cg = h[:, 768:1024]
    vc = h[:, 1024:1280]
    pin = h[:, 1280:1536]

    u = cg * vc
    eu = jnp.concatenate([cu_sc[...], u], axis=0)
    cw = cw_ref[...]
    y = (eu[carry - 2 * rs:carry - 2 * rs + tm] * cw[0:1] + eu[carry - rs:carry - rs + tm] * cw[1:2]
         + u * cw[2:3])
    o_conv = bg * y
    new_u = eu[tm:tm + carry]
    cu_sc[...] = new_u
    cnew_ref[0] = new_u

    ep = jnp.concatenate([cp_sc[...], pin], axis=0)
    s2 = ep[rs:] + ep[:-rs]
    s4 = s2[2 * rs:] + s2[:-2 * rs]
    s8 = s4[4 * rs:] + s4[:-4 * rs]
    s16 = s8[8 * rs:] + s8[:-8 * rs]
    take = lambda a: a[a.shape[0] - tm:]
    lane = lax.broadcasted_iota(jnp.int32, (tm, 256), 1)
    row = lax.broadcasted_iota(jnp.int32, (tm, 256), 0)
    dsum = jnp.where(lane < 64, take(s2), jnp.where(lane < 128, take(s4), jnp.where(lane < 192, take(s8), take(s16))))
    wlen = jnp.where(lane < 64, 2, jnp.where(lane < 128, 4, jnp.where(lane < 192, 8, 16)))
    step = pos0 + (ti * tm + row) // rs
    cnt = jnp.minimum(wlen, step + 1).astype(F32)
    d = dsum / cnt - pin
    o_pool = _dot(d.astype(BF16), pw_ref[...]) * ps_ref[...]
    new_p = ep[tm:tm + carry]
    cp_sc[...] = new_p
    pnew_ref[0] = new_p

    ocp_ref[0, :, 0:256] = o_conv.astype(BF16)
    ocp_ref[0, :, 256:512] = o_pool.astype(BF16)


def _proj_call(x3, wm, wk, tabr, tabt, cw, pw, ps, cst, pst, *, l, tm, rs, pos0, rows_dtype, rows_rope):
    n_sg, n_rows, d_model = x3.shape
    tiles = n_rows // tm
    n_main = wm.shape[2]
    n_rowcols = n_main - _MAIN_W
    carry = CARRY_STEPS * rs
    kern = functools.partial(_proj_kernel, tm=tm, rs=rs, pos0=pos0, rows_rope=rows_rope)
    out_shape = (
        jax.ShapeDtypeStruct((n_sg, n_rows, 1024), BF16),
        jax.ShapeDtypeStruct((n_sg, n_rows, 128), F32),
        jax.ShapeDtypeStruct((n_sg, n_rows, 512), BF16),
        jax.ShapeDtypeStruct((n_sg, n_rows, n_rowcols), rows_dtype),
        jax.ShapeDtypeStruct((n_sg, 512, n_rows), F32),
        jax.ShapeDtypeStruct((n_sg, 256, n_rows), F32),
        jax.ShapeDtypeStruct((n_sg, 768, n_rows), BF16),
        jax.ShapeDtypeStruct((n_sg, carry, 256), F32),
        jax.ShapeDtypeStruct((n_sg, carry, 256), F32),
    )
    row_blk = lambda w: pl.BlockSpec((1, tm, w), lambda s, t: (s, t, 0))
    col_blk = lambda w: pl.BlockSpec((1, w, tm), lambda s, t: (s, 0, t))
    st_blk = pl.BlockSpec((1, carry, 256), lambda s, t: (s, 0, 0))
    return pl.pallas_call(
        kern,
        grid=(n_sg, tiles),
        in_specs=[
            row_blk(d_model),
            pl.BlockSpec((None, d_model, n_main), lambda s, t: (l, 0, 0)),
            pl.BlockSpec((None, 768, d_model), lambda s, t: (l, 0, 0)),
            pl.BlockSpec((tm, 384), lambda s, t: (t, 0)),
            pl.BlockSpec((384, tm), lambda s, t: (0, t)),
            pl.BlockSpec((None, 8, 256), lambda s, t: (l, 0, 0)),
            pl.BlockSpec((None, 256, 256), lambda s, t: (l, 0, 0)),
            pl.BlockSpec((None, 1, 256), lambda s, t: (l, 0, 0)),
            st_blk, st_blk,
        ],
        out_specs=(row_blk(1024), row_blk(128), row_blk(512), row_blk(n_rowcols),
                   col_blk(512), col_blk(256), col_blk(768), st_blk, st_blk),
        out_shape=out_shape,
        scratch_shapes=[pltpu.VMEM((carry, 256), F32), pltpu.VMEM((carry, 256), F32)],
        compiler_params=_cparams(("arbitrary", "arbitrary")),
        name="proj",
    )(x3, wm, wk, tabr, tabt, cw, pw, ps, cst, pst)


def _gelu_tanh(x):
    return 0.5 * x * (1.0 + jnp.tanh(0.7978845608028654 * (x + 0.044715 * x * x * x)))


def _compress_core(read_rows, n, pe_ref, wa_ref, wb_ref, w2_ref, kc_ref, vc_ref):
    for kv, out_ref in ((0, kc_ref), (1, vc_ref)):
        cols = [read_rows(kv, r) for r in range(CMP_STRIDE)]
        xa = jnp.concatenate([cols[r] + pe_ref[kv, 0, r:r + 1, :] for r in range(CMP_STRIDE)], axis=1)
        xb = jnp.concatenate([cols[r] + pe_ref[kv, 1, r:r + 1, :] for r in range(CMP_STRIDE)], axis=1)
        a = _dot(xa.astype(BF16), wa_ref[kv])
        b = _dot(xb.astype(BF16), wb_ref[kv])
        pre = a + pltpu.roll(b, n - 1, 0)
        hid = _gelu_tanh(pre)
        out_ref[0] = _dot(hid.astype(BF16), w2_ref[kv]).astype(BF16)


def _compress_rows_kernel(k_ref, v_ref, pe_ref, wa_ref, wb_ref, w2_ref, kc_ref, vc_ref, *, n):
    srcs = (k_ref, v_ref)
    read = lambda kv, r: srcs[kv][0, pl.ds(r, n, stride=CMP_STRIDE), :]
    _compress_core(read, n, pe_ref, wa_ref, wb_ref, w2_ref, kc_ref, vc_ref)


def _compress_pages_kernel(pt_ref, *refs, n_pages_step, page, n):
    pages = refs[:n_pages_step]
    pe_ref, wa_ref, wb_ref, w2_ref, kc_ref, vc_ref, kbuf, vbuf = refs[n_pages_step:]
    for i, pg in enumerate(pages):
        kbuf[i * page:(i + 1) * page, :] = pg[0:128, :].T
        vbuf[i * page:(i + 1) * page, :] = pg[128:256, :].T
    bufs = (kbuf, vbuf)
    read = lambda kv, r: bufs[kv][pl.ds(r, n, stride=CMP_STRIDE), :]
    _compress_core(read, n, pe_ref, wa_ref, wb_ref, w2_ref, kc_ref, vc_ref)


def _cmp_weight_specs(l, nidx):
    im4 = (lambda *a: (l, 0, 0, 0))
    im5 = (lambda *a: (l, 0, 0, 0, 0))
    return [
        pl.BlockSpec((None, 2, 2, CMP_STRIDE, 128), im5),
        pl.BlockSpec((None, 2, 2048, 512), im4),
        pl.BlockSpec((None, 2, 2048, 512), im4),
        pl.BlockSpec((None, 2, 512, 128), im4),
    ]


def _compress_rows_call(rows, pe, wa, wb, w2, *, l):
    b, t, _ = rows.shape
    n = t // CMP_STRIDE
    out = jax.ShapeDtypeStruct((b, n, 128), BF16)
    return pl.pallas_call(
        functools.partial(_compress_rows_kernel, n=n),
        grid=(b,),
        in_specs=[pl.BlockSpec((1, t, 128), lambda i: (i, 0, 0)),
                  pl.BlockSpec((1, t, 128), lambda i: (i, 0, 1))] + _cmp_weight_specs(l, 1),
        out_specs=(pl.BlockSpec((1, n, 128), lambda i: (i, 0, 0)),) * 2,
        out_shape=(out, out),
        compiler_params=_cparams(("arbitrary",)),
        name="compress_rows",
    )(rows, rows, pe, wa, wb, w2)


def _compress_pages_call(pt_flat, cache_t, pe, wa, wb, w2, *, l, n_seq, n_pages, n_phys, seqs_step):
    page = cache_t.shape[2]
    n_pages_step = seqs_step * n_pages
    n = n_pages_step * page // CMP_STRIDE
    steps = n_seq // seqs_step

    def page_spec(k):
        s, p = divmod(k, n_pages)
        return pl.BlockSpec((None, 256, page),
                            lambda i, pt: (l * n_phys + pt[(i * seqs_step + s) * n_pages + p], 0, 0))

    out = jax.ShapeDtypeStruct((steps, n, 128), BF16)
    return pl.pallas_call(
        functools.partial(_compress_pages_kernel, n_pages_step=n_pages_step, page=page, n=n),
        grid_spec=pltpu.PrefetchScalarGridSpec(
            num_scalar_prefetch=1,
            grid=(steps,),
            in_specs=[page_spec(k) for k in range(n_pages_step)] + _cmp_weight_specs(l, 2),
            out_specs=(pl.BlockSpec((1, n, 128), lambda i, pt: (i, 0, 0)),) * 2,
            scratch_shapes=[pltpu.VMEM((n_pages_step * page, 128), F32)] * 2,
        ),
        out_shape=(out, out),
        compiler_params=_cparams(("arbitrary",)),
        name="compress_pages",
    )(pt_flat, *([cache_t] * n_pages_step), pe, wa, wb, w2)


def _q_rows(q, tq):
    lo = lax.broadcasted_iota(jnp.int32, (tq, 128), 1) < 64
    cols = [q[:, c * 128:(c + 1) * 128] for c in range(4)]
    return jnp.concatenate([jnp.where(lo, c, 0.0) for c in cols] + [jnp.where(lo, 0.0, c) for c in cols], axis=0)


def _pair_cols(o, tq):
    lo = lax.broadcasted_iota(jnp.int32, (tq, 128), 1) < 64
    return [jnp.where(lo, o[c * tq:(c + 1) * tq], o[(4 + c) * tq:(5 + c) * tq]) for c in range(4)]


def _row_pos(tq, n, pos_base):
    row = lax.broadcasted_iota(jnp.int32, (8 * tq, n), 0)
    return pos_base + (row & (tq - 1))


def _compressed_branch(qc, kc, vc, ov, tq, pos_base):
    nc = kc.shape[0]
    s = _dot_nt(qc, kc)
    pos = _row_pos(tq, nc, pos_base)
    cend = lax.broadcasted_iota(jnp.int32, (8 * tq, nc), 1) * CMP_STRIDE + (CMP_BLOCK - 1)
    vis = cend <= pos
    s = jnp.where(vis, s, NEG)
    e = jnp.where(vis, jnp.exp(s - jnp.max(s, axis=-1, keepdims=True)), 0.0)
    den = jnp.sum(e, axis=-1, keepdims=True)
    p = (e / jnp.where(den > 0.0, den, 1.0)).astype(BF16)
    o = _dot(p, vc)
    imp8 = _dot(p, ov)
    imp = jnp.concatenate([imp8[(4 * g) * tq:(4 * g + 1) * tq] + imp8[(4 * g + 1) * tq:(4 * g + 2) * tq]
                           + imp8[(4 * g + 2) * tq:(4 * g + 3) * tq] + imp8[(4 * g + 3) * tq:(4 * g + 4) * tq]
                           for g in range(2)], axis=0)
    return o, imp


def _select_blocks(imp, tq, pos_base, n_sel):
    row = lax.broadcasted_iota(jnp.int32, (2 * tq, 128), 0)
    blk = lax.broadcasted_iota(jnp.int32, (2 * tq, 128), 1)
    pos = pos_base + (row & (tq - 1))
    cur = pos // SEL_BLOCK
    forced = (blk == 0) | (blk == cur) | (blk == cur - 1)
    valid = blk * SEL_BLOCK <= pos
    score = jnp.where(valid, imp + jnp.where(forced, FORCE_BONUS, 0.0), NEG)

    if 2 * tq >= 128:
        ns8 = -(-n_sel // 8) * 8
        st = score.T[0:ns8]
        sub = lax.broadcasted_iota(jnp.int32, st.shape, 0)
        rank = jnp.zeros(st.shape, F32)
        for b in range(n_sel):
            other = st[b:b + 1, :]
            rank = rank + jnp.where((other > st) | ((other == st) & (sub > b)), 1.0, 0.0)
        keep = jnp.where((rank < SEL_TOP_K) & (sub < n_sel), 1.0, 0.0)
        keep = jnp.concatenate([keep, jnp.zeros((128 - ns8, 2 * tq), F32)], axis=0)
        return keep.T
    rank = jnp.zeros(score.shape, F32)
    for b in range(n_sel):
        other = score[:, b:b + 1]
        rank = rank + jnp.where((other > score) | ((other == score) & (blk > b)), 1.0, 0.0)
    return jnp.where((rank < SEL_TOP_K) & (blk < n_sel), 1.0, 0.0)


def _softmax_pv(pieces):
    m = None
    for s, _, _ in pieces:
        mi = jnp.max(s, axis=-1, keepdims=True)
        m = mi if m is None else jnp.maximum(m, mi)
    den = 0.0
    acc = 0.0
    for s, v, fm in pieces:
        p = jnp.exp(s - m)
        den = den + jnp.sum(p, axis=-1, keepdims=True)
        pb = p.astype(BF16)
        acc = acc + (_dot_nt(pb, v) if fm else _dot(pb, v))
    return acc / den


def _gated_sum(branches, gates, tq):
    lo = lax.broadcasted_iota(jnp.int32, (tq, 128), 1) < 64
    cols = [_pair_cols(o, tq) for o in branches]
    out = []
    for c in range(4):
        acc = 0.0
        for n in range(3):
            ga = gates[:, 3 * c + n:3 * c + n + 1]
            gb = gates[:, 3 * (c + 4) + n:3 * (c + 4) + n + 1]
            acc = acc + cols[n][c] * jnp.where(lo, ga, gb)
        out.append(acc)
    return out


def _attn_prompt_kernel(qq_ref, gate_ref, kc_ref, vc_ref, kst_ref, vst_ref, kwt_ref, vwt_ref, et_ref, ov_ref,
                        o_ref, kaug_sc, m_sc, l_sc, acc_sc, *, tq, t_len, tk):
    j = pl.program_id(1)
    rows = 8 * tq

    @pl.when(j == 0)
    def _():
        kaug_sc[0:128, :] = kst_ref[0]
        kaug_sc[128:256, :] = et_ref[...]

    pos_base = j * tq
    qq = qq_ref[0].astype(F32)
    qc = _q_rows(qq[:, 0:512], tq).astype(BF16)
    qs = _q_rows(qq[:, 512:1024], tq).astype(BF16)

    o_cmp, imp = _compressed_branch(qc, kc_ref[0], vc_ref[0], ov_ref[...], tq, pos_base)
    sel = _select_blocks(imp, tq, pos_base, -(-t_len // SEL_BLOCK))
    bias = jnp.where(sel > 0.5, 0.0, NEG).astype(BF16)
    bias_rows = jnp.concatenate([bias[0:tq]] * 4 + [bias[tq:2 * tq]] * 4, axis=0)
    q_aug = jnp.concatenate([qs, bias_rows], axis=1)

    m_sc[...] = jnp.full((rows, 128), NEG, F32)
    l_sc[...] = jnp.zeros((rows, 128), F32)
    acc_sc[...] = jnp.zeros((rows, 128), F32)

    def update(k0, causal):
        s = _dot(q_aug, kaug_sc[:, pl.ds(k0, tk)])
        if causal:
            kpos = k0 + lax.broadcasted_iota(jnp.int32, (rows, tk), 1)
            s = jnp.where(kpos <= _row_pos(tq, tk, pos_base), s, NEG)
        m_old = m_sc[...]
        m_new = jnp.maximum(m_old, jnp.max(s, axis=-1, keepdims=True))
        alpha = jnp.exp(m_old - m_new)
        p = jnp.exp(s - jnp.concatenate([m_new] * (tk // 128), axis=1))
        l_sc[...] = alpha * l_sc[...] + jnp.sum(p, axis=-1, keepdims=True)
        acc_sc[...] = alpha * acc_sc[...] + _dot_nt(p.astype(BF16), vst_ref[0, :, pl.ds(k0, tk)])
        m_sc[...] = m_new

    n_bulk = (j * tq) // tk

    def bulk(kt, c):
        update(pl.multiple_of(kt * tk, tk), False)
        return c

    lax.fori_loop(0, n_bulk, bulk, 0)
    update(pl.multiple_of(n_bulk * tk, tk), True)
    o_slc = acc_sc[...] / l_sc[...]

    wk = WINDOW + tq
    k0 = pl.multiple_of(jnp.maximum(j * tq - WINDOW, 0), 128)
    s = _dot(qs, kwt_ref[0, :, pl.ds(k0, wk)])
    dist = _row_pos(tq, wk, pos_base) - (k0 + lax.broadcasted_iota(jnp.int32, (rows, wk), 1))
    s = jnp.where((dist >= 0) & (dist < WINDOW), s, NEG)
    o_win = _softmax_pv([(s, vwt_ref[0, :, pl.ds(k0, wk)], True)])

    cols = _gated_sum([o_cmp, o_slc, o_win], gate_ref[0], tq)
    for c in range(4):
        o_ref[0, :, c * 128:(c + 1) * 128] = cols[c].astype(BF16)


def _attn_prompt_call(qq, gates, kc, vc, kvtb, et, ov):
    b, t_len, _ = qq.shape
    tq = Q_BLOCK
    tk = 256
    nc = kc.shape[1]
    kern = functools.partial(_attn_prompt_kernel, tq=tq, t_len=t_len, tk=tk)
    kv_blk = lambda c: pl.BlockSpec((1, 128, t_len), lambda i, j: (i, c, 0))
    return pl.pallas_call(
        kern,
        grid=(b, t_len // tq),
        in_specs=[
            pl.BlockSpec((1, tq, 1024), lambda i, j: (i, j, 0)),
            pl.BlockSpec((1, tq, 128), lambda i, j: (i, j, 0)),
            pl.BlockSpec((1, nc, 128), lambda i, j: (i, 0, 0)),
            pl.BlockSpec((1, nc, 128), lambda i, j: (i, 0, 0)),
            kv_blk(2), kv_blk(3), kv_blk(4), kv_blk(5),
            pl.BlockSpec((128, t_len), lambda i, j: (0, 0)),
            pl.BlockSpec((nc, 128), lambda i, j: (0, 0)),
        ],
        out_specs=pl.BlockSpec((1, tq, 512), lambda i, j: (i, j, 0)),
        out_shape=jax.ShapeDtypeStruct((b, t_len, 512), BF16),
        scratch_shapes=[pltpu.VMEM((256, t_len), BF16), pltpu.VMEM((8 * tq, 128), F32),
                        pltpu.VMEM((8 * tq, 128), F32), pltpu.VMEM((8 * tq, 128), F32)],
        compiler_params=_cparams(("arbitrary", "arbitrary")),
        name="attn_prompt",
    )(qq, gates, kc, vc, kvtb, kvtb, kvtb, kvtb, et, ov)


def _attn_sample_kernel(pt_ref, *refs, n_pages, page, tq, past, win_rows):
    pages = refs[:n_pages]
    qq_ref, gate_ref, kc_ref, vc_ref, new_ref, cwin_ref, neww_ref, es_ref, ov_ref = refs[n_pages:n_pages + 9]
    o_ref, wout_ref = refs[-2:]
    rows = 8 * tq
    pos_base = past

    qq = qq_ref[0].astype(F32)
    qc = _q_rows(qq[:, 0:512], tq).astype(BF16)
    qs = _q_rows(qq[:, 512:1024], tq).astype(BF16)

    o_cmp, imp = _compressed_branch(qc, kc_ref[0], vc_ref[0], ov_ref[...], tq, pos_base)
    sel = _select_blocks(imp, tq, pos_base, -(-(past + tq) // SEL_BLOCK))

    new = jnp.concatenate([new_ref[0].astype(F32), jnp.zeros((128 - tq, 512), F32)], axis=0).astype(BF16)

    kst = jnp.concatenate([pg[0:128, :].astype(BF16) for pg in pages], axis=1)
    vst = jnp.concatenate([pg[128:256, :].astype(BF16) for pg in pages], axis=1)
    n_keys = past + 128
    member = _dot(sel.astype(BF16), es_ref[...])
    member = jnp.concatenate([member[0:tq]] * 4 + [member[tq:2 * tq]] * 4, axis=0)
    kpos = lax.broadcasted_iota(jnp.int32, (rows, n_keys), 1)
    ok = (member > 0.5) & (kpos <= _row_pos(tq, n_keys, pos_base))
    s_past = jnp.where(ok[:, 0:past], _dot(qs, kst), NEG)
    s_new = jnp.where(ok[:, past:], _dot_nt(qs, new[:, 0:128]), NEG)
    o_slc = _softmax_pv([(s_past, vst, True), (s_new, new[:, 128:256], False)])

    cw = cwin_ref[0]
    n_wk = win_rows + 128
    kpos_w = (past - win_rows) + lax.broadcasted_iota(jnp.int32, (rows, n_wk), 1)
    dist = _row_pos(tq, n_wk, pos_base) - kpos_w
    okw = (dist >= 0) & (dist < WINDOW)
    s_old = jnp.where(okw[:, 0:win_rows], _dot(qs, cw[0:128].astype(BF16)), NEG)
    s_nw = jnp.where(okw[:, win_rows:], _dot_nt(qs, new[:, 256:384]), NEG)
    o_win = _softmax_pv([(s_old, cw[128:256].astype(BF16), True), (s_nw, new[:, 384:512], False)])

    cols = _gated_sum([o_cmp, o_slc, o_win], gate_ref[0], tq)
    for c in range(4):
        o_ref[0, :, c * 128:(c + 1) * 128] = cols[c].astype(BF16)

    wout_ref[0] = pltpu.roll(cw, win_rows - tq, 1)
    wout_ref[0, :, win_rows - tq:] = neww_ref[0]


def _attn_sample_call(pt_flat, cache_t, qq, gates, kc, vc, new_rows, cwin_t, new_win_t, es, ov, win_prev, *,
                      l, depth, n_pages, n_phys, past):
    n_seq, tq, _ = qq.shape
    page = cache_t.shape[2]
    win_rows = cwin_t.shape[2]
    nc = kc.shape[1]
    kern = functools.partial(_attn_sample_kernel, n_pages=n_pages, page=page, tq=tq, past=past, win_rows=win_rows)

    def page_spec(p):
        return pl.BlockSpec((None, 256, page), lambda i, pt: (l * n_phys + pt[i * n_pages + p], 1, 0))

    seq_blk = lambda w: pl.BlockSpec((1, tq, w), lambda i, pt: (i, 0, 0))
    win_blk = pl.BlockSpec((1, 256, win_rows), lambda i, pt: (l * n_seq + i, 0, 0))
    in_specs = [page_spec(p) for p in range(n_pages)] + [
        seq_blk(1024), seq_blk(128),
        pl.BlockSpec((1, nc, 128), lambda i, pt: (i, 0, 0)),
        pl.BlockSpec((1, nc, 128), lambda i, pt: (i, 0, 0)),
        seq_blk(512),
        win_blk,
        pl.BlockSpec((1, 256, tq), lambda i, pt: (i, 0, 0)),
        pl.BlockSpec((128, past + 128), lambda i, pt: (0, 0)),
        pl.BlockSpec((nc, 128), lambda i, pt: (0, 0)),
    ]
    operands = [pt_flat] + [cache_t] * n_pages + [qq, gates, kc, vc, new_rows, cwin_t, new_win_t, es, ov]
    aliases = {}
    if win_prev is not None:
        in_specs.append(pl.BlockSpec(memory_space=pl.ANY))
        aliases = {len(operands): 1}
        operands.append(win_prev)
    return pl.pallas_call(
        kern,
        grid_spec=pltpu.PrefetchScalarGridSpec(
            num_scalar_prefetch=1,
            grid=(n_seq,),
            in_specs=in_specs,
            out_specs=(pl.BlockSpec((1, tq, 512), lambda i, pt: (i, 0, 0)), win_blk),
        ),
        out_shape=(jax.ShapeDtypeStruct((n_seq, tq, 512), BF16),
                   jax.ShapeDtypeStruct((depth * n_seq, 256, win_rows), F32)),
        input_output_aliases=aliases,
        compiler_params=_cparams(("arbitrary",)),
        name="attn_sample",
    )(*operands)


def _mix_kernel(oa_ref, ocp_ref, x_ref, wo_ref, g_ref, b_ref, wr_ref, br_ref, x1_ref, route_ref, cnt_ref, *,
                alpha):
    mix = _dot(oa_ref[...], wo_ref[0:512, :]) + _dot(ocp_ref[...], wo_ref[512:1024, :])
    x1 = _layer_norm(alpha * x_ref[...] + mix, g_ref[...], b_ref[...])
    x1_ref[...] = x1

    tm = x1.shape[0]
    xh = x1.astype(BF16)
    xl = (x1 - xh.astype(F32)).astype(BF16)
    hw = _dot(xh, wr_ref[...])
    logits = hw[:, 0:128] + hw[:, 128:256] + _dot(xl, wr_ref[:, 0:128]) + br_ref[...]
    lane = lax.broadcasted_iota(jnp.int32, (tm, 128), 1).astype(F32)
    big = 1e9
    is_g = (lane >= N_EXPERTS) & (lane < N_EXPERTS + N_EXPERT_GROUPS)
    lg = jnp.where(is_g, logits, NEG)
    ge = jnp.where(is_g, jnp.exp(lg - jnp.max(lg, axis=-1, keepdims=True)), 0.0)
    gp = ge / jnp.sum(ge, axis=-1, keepdims=True)
    gw = jnp.max(gp, axis=-1, keepdims=True)
    gidx = jnp.min(jnp.where(is_g & (gp == gw), lane - N_EXPERTS, big), axis=-1, keepdims=True)
    in_g = (lane >= gidx * EXPERTS_PER_GROUP) & (lane < (gidx + 1.0) * EXPERTS_PER_GROUP)
    le = jnp.where(in_g, logits, NEG)
    ee = jnp.where(in_g, jnp.exp(le - jnp.max(le, axis=-1, keepdims=True)), 0.0)
    ep = ee / jnp.sum(ee, axis=-1, keepdims=True)
    w1 = jnp.max(jnp.where(in_g, ep, -1.0), axis=-1, keepdims=True)
    i1 = jnp.min(jnp.where(in_g & (ep == w1), lane, big), axis=-1, keepdims=True)
    rest = in_g & (lane != i1)
    w2 = jnp.max(jnp.where(rest, ep, -1.0), axis=-1, keepdims=True)
    i2 = jnp.min(jnp.where(rest & (ep == w2), lane, big), axis=-1, keepdims=True)
    den = w1 + w2
    route_ref[...] = jnp.where(lane == 0.0, i1, jnp.where(lane == 1.0, i2, jnp.where(
        lane == 2.0, gw * (w1 / den), jnp.where(lane == 3.0, gw * (w2 / den), 0.0))))
    pairs = jnp.sum(jnp.where((lane == i1) | (lane == i2), 1.0, 0.0), axis=0, keepdims=True)
    cnt_ref[0] = jnp.broadcast_to(pairs, (8, 128))


def _mix_call(oa, ocp, x, wo, g, b, wr, br, *, l, alpha, tm):
    n, d = x.shape
    row = lambda w: pl.BlockSpec((tm, w), lambda i: (i, 0))
    vec = lambda w: pl.BlockSpec((None, 1, w), lambda i: (l, 0, 0))
    return pl.pallas_call(
        functools.partial(_mix_kernel, alpha=alpha),
        grid=(n // tm,),
        in_specs=[row(512), row(512), row(d),
                  pl.BlockSpec((None, 1024, d), lambda i: (l, 0, 0)), vec(d), vec(d),
                  pl.BlockSpec((None, d, 256), lambda i: (l, 0, 0)), vec(128)],
        out_specs=(row(d), row(128), pl.BlockSpec((1, 8, 128), lambda i: (i, 0, 0))),
        out_shape=(jax.ShapeDtypeStruct((n, d), F32), jax.ShapeDtypeStruct((n, 128), F32),
                   jax.ShapeDtypeStruct((n // tm, 8, 128), F32)),
        compiler_params=_cparams(("arbitrary",)),
        name="mix_router",
    )(oa, ocp, x, wo, g, b, wr, br)


MOE_CHUNK = 16
MOE_TT = 512
MOE_TE = 256
MOE_SLOTS = 2 * MOE_TT + N_EXPERTS * MOE_CHUNK
MOE_XW = 1024 + LANES


def _chunk_copy(src, s0, dst, d0, sem):
    return pltpu.make_async_copy(src.at[pl.ds(s0, MOE_CHUNK), :], dst.at[pl.ds(d0, MOE_CHUNK), :], sem)


def _for_each_chunk(i, lo_s, dst_s, nch_s, fn):
    def per_expert(e, total):
        k = i * N_EXPERTS + e
        n = nch_s[k]

        def per_chunk(c, carry):
            fn(pl.multiple_of(lo_s[k] + c * MOE_CHUNK, MOE_CHUNK), pl.multiple_of(dst_s[k] + c * MOE_CHUNK, MOE_CHUNK))
            return carry

        lax.fori_loop(0, n, per_chunk, 0)
        return total + n

    return lax.fori_loop(0, N_EXPERTS, per_expert, 0)


def _split3(c):
    h = c.astype(BF16).astype(F32)
    m = (c - h).astype(BF16).astype(F32)
    r = (c - h - m).astype(BF16).astype(F32)
    lane = lax.broadcasted_iota(jnp.int32, (c.shape[0], 128), 1)
    return jnp.where(lane == 0, h, jnp.where(lane == 1, m, jnp.where(lane == 2, r, 0.0)))


def _dispatch_kernel(lo_s, dst_s, nch_s, fst_s, fn_s, x1_ref, route_ref, lov_ref, xs_hbm, xs_sc, z_sc, sem):
    i = pl.program_id(0)
    tt = MOE_TT

    @pl.when(i == 0)
    def _():
        z_sc[...] = jnp.zeros_like(z_sc)

        def per_expert(e, total):
            def per_chunk(c, carry):
                _chunk_copy(z_sc, 0, xs_hbm, pl.multiple_of(fst_s[e] + c * MOE_CHUNK, MOE_CHUNK), sem).start()
                return carry
            lax.fori_loop(0, fn_s[e], per_chunk, 0)
            return total + fn_s[e]

        n_fill = lax.fori_loop(0, N_EXPERTS, per_expert, 0)

        def drain(c, carry):
            _chunk_copy(z_sc, 0, xs_hbm, 0, sem).wait()
            return carry
        lax.fori_loop(0, n_fill, drain, 0)

    route = route_ref[...]
    rt = route.T
    eio = lax.broadcasted_iota(jnp.int32, (N_EXPERTS, tt), 0).astype(F32)
    m1 = eio == rt[0:1]
    m2 = eio == rt[1:2]
    before = (lax.broadcasted_iota(jnp.int32, (tt, tt), 0) < lax.broadcasted_iota(jnp.int32, (tt, tt), 1))
    rank = _dot(jnp.where(m1 | m2, 1.0, 0.0).astype(BF16), jnp.where(before, 1.0, 0.0).astype(BF16))
    slot = jnp.concatenate([lov_ref[0]] * (tt // 128), axis=1) + rank
    s1 = jnp.sum(jnp.where(m1, slot, 0.0), axis=0, keepdims=True)
    s2 = jnp.sum(jnp.where(m2, slot, 0.0), axis=0, keepdims=True)
    sio = lax.broadcasted_iota(jnp.int32, (MOE_SLOTS, tt), 0).astype(F32)
    p1 = jnp.where(sio == s1, 1.0, 0.0).astype(BF16)
    p2 = jnp.where(sio == s2, 1.0, 0.0).astype(BF16)
    xs_sc[:, 0:1024] = _dot(p1 + p2, x1_ref[...].astype(BF16)).astype(BF16)
    cw = _dot(p1, _split3(route[:, 2:3]).astype(BF16)) + _dot(p2, _split3(route[:, 3:4]).astype(BF16))
    xs_sc[:, 1024:MOE_XW] = cw.astype(BF16)

    n_out = _for_each_chunk(i, lo_s, dst_s, nch_s,
                            lambda s0, d0: _chunk_copy(xs_sc, s0, xs_hbm, d0, sem).start())

    def drain(c, carry):
        _chunk_copy(xs_sc, 0, xs_hbm, 0, sem).wait()
        return carry
    lax.fori_loop(0, n_out, drain, 0)


def _experts_kernel(te_s, blk_s, nact_s, xs_ref, wg_ref, wu_ref, wd_ref, y_ref, wgu_sc, wd_sc):
    k = pl.program_id(0)
    active = k < nact_s[0]
    fresh = (k == 0) | (te_s[k] != te_s[jnp.maximum(k - 1, 0)])

    @pl.when(active & fresh)
    def _():
        wgu_sc[:, 0:D_EXPERT] = wg_ref[...].astype(BF16)
        wgu_sc[:, D_EXPERT:2 * D_EXPERT] = wu_ref[...].astype(BF16)
        wd_sc[...] = wd_ref[...].astype(BF16)

    @pl.when(active)
    def _():
        cw = xs_ref[:, 1024:MOE_XW].astype(F32)
        c = cw[:, 0:1] + cw[:, 1:2] + cw[:, 2:3]
        hgu = _dot(xs_ref[:, 0:1024], wgu_sc[...])
        hg = hgu[:, 0:D_EXPERT]
        hid = hg * (1.0 / (1.0 + jnp.exp(-hg))) * hgu[:, D_EXPERT:2 * D_EXPERT]
        y = _dot((hid * c).astype(BF16), wd_sc[...])
        yh = y.astype(BF16)
        y_ref[:, 0:1024] = yh
        y_ref[:, 1024:2048] = (y - yh.astype(F32)).astype(BF16)


def _combine_kernel(lo_s, dst_s, nch_s, x1_ref, route_ref, lor_ref, g_ref, b_ref, ys_hbm, o_ref, ys_sc, sem, *,
                    alpha):
    i = pl.program_id(0)
    tt = MOE_TT

    @pl.when(i == 0)
    def _():
        ys_sc[...] = jnp.zeros_like(ys_sc)

    n_in = _for_each_chunk(i, lo_s, dst_s, nch_s,
                           lambda s0, d0: _chunk_copy(ys_hbm, d0, ys_sc, s0, sem).start())

    route = route_ref[...]
    lane = lax.broadcasted_iota(jnp.int32, (tt, 128), 1).astype(F32)
    m1 = lane == route[:, 0:1]
    m2 = lane == route[:, 1:2]
    before = (lax.broadcasted_iota(jnp.int32, (tt, tt), 1) < lax.broadcasted_iota(jnp.int32, (tt, tt), 0))
    rank = _dot(jnp.where(before, 1.0, 0.0).astype(BF16), jnp.where(m1 | m2, 1.0, 0.0).astype(BF16))
    slot = lor_ref[0, 0:1, :] + rank
    s1 = jnp.sum(jnp.where(m1, slot, 0.0), axis=-1, keepdims=True)
    s2 = jnp.sum(jnp.where(m2, slot, 0.0), axis=-1, keepdims=True)
    sio = lax.broadcasted_iota(jnp.int32, (tt, MOE_SLOTS), 1).astype(F32)
    place = jnp.where((sio == s1) | (sio == s2), 1.0, 0.0).astype(BF16)

    def drain(c, carry):
        _chunk_copy(ys_hbm, 0, ys_sc, 0, sem).wait()
        return carry
    lax.fori_loop(0, n_in, drain, 0)

    y2 = _dot(place, ys_sc[...])
    y = y2[:, 0:1024] + y2[:, 1024:2048]
    o_ref[...] = _layer_norm(alpha * x1_ref[...] + y, g_ref[...], b_ref[...])


def _moe_call(x1, route, cnt, w_gate, w_up, w_down, g, b, *, l, alpha):
    n, d = x1.shape
    tt, te, ch, n_e = MOE_TT, MOE_TE, MOE_CHUNK, N_EXPERTS
    nt = n // tt
    n_et = -(-(2 * n + nt * n_e * (ch - 1)) // te) + n_e
    i32 = jnp.int32

    pairs = cnt[:, 0, :n_e].astype(i32)
    pc = (pairs + ch - 1) // ch * ch
    lo = jnp.cumsum(pc, axis=1) - pc
    tot = jnp.sum(pc, axis=0)
    reg = (tot + te - 1) // te * te
    base = jnp.cumsum(reg) - reg
    dst = base[None, :] + jnp.cumsum(pc, axis=0) - pc
    tiles_e = reg // te
    ends = jnp.cumsum(tiles_e)
    n_act = ends[-1]
    k = jnp.arange(n_et, dtype=i32)
    blk = jnp.minimum(k, n_act - 1)
    tile_e = jnp.minimum(jnp.searchsorted(ends, blk, side="right").astype(i32), n_e - 1)
    flat = lambda a: a.reshape(-1).astype(i32)
    lo_s, dst_s, nch_s = flat(lo), flat(dst), flat(pc // ch)
    lov = jnp.broadcast_to(lo.astype(F32)[:, :, None], (nt, n_e, 128))
    lor = jnp.broadcast_to(jnp.pad(lo.astype(F32), ((0, 0), (0, 128 - n_e)))[:, None, :], (nt, 8, 128))

    xs = pl.pallas_call(
        _dispatch_kernel,
        grid_spec=pltpu.PrefetchScalarGridSpec(
            num_scalar_prefetch=5,
            grid=(nt,),
            in_specs=[pl.BlockSpec((tt, d), lambda i, *_: (i, 0)),
                      pl.BlockSpec((tt, 128), lambda i, *_: (i, 0)),
                      pl.BlockSpec((1, n_e, 128), lambda i, *_: (i, 0, 0))],
            out_specs=pl.BlockSpec(memory_space=pl.ANY),
            scratch_shapes=[pltpu.VMEM((MOE_SLOTS, MOE_XW), BF16), pltpu.VMEM((ch, MOE_XW), BF16),
                            pltpu.SemaphoreType.DMA(())],
        ),
        out_shape=jax.ShapeDtypeStruct((n_et * te, MOE_XW), BF16),
        compiler_params=_cparams(("arbitrary",)),
        name="moe_dispatch",
    )(lo_s, dst_s, nch_s, flat(base + tot), flat((reg - tot) // ch), x1, route, lov)

    wspec = lambda r, c: pl.BlockSpec((None, None, r, c), lambda k, te_s, blk_s, na: (l, te_s[k], 0, 0))
    ys = pl.pallas_call(
        _experts_kernel,
        grid_spec=pltpu.PrefetchScalarGridSpec(
            num_scalar_prefetch=3,
            grid=(n_et,),
            in_specs=[pl.BlockSpec((te, MOE_XW), lambda k, te_s, blk_s, na: (blk_s[k], 0)),
                      wspec(d, D_EXPERT), wspec(d, D_EXPERT), wspec(D_EXPERT, d)],
            out_specs=pl.BlockSpec((te, 2 * d), lambda k, te_s, blk_s, na: (blk_s[k], 0)),
            scratch_shapes=[pltpu.VMEM((d, 2 * D_EXPERT), BF16), pltpu.VMEM((D_EXPERT, d), BF16)],
        ),
        out_shape=jax.ShapeDtypeStruct((n_et * te, 2 * d), BF16),
        compiler_params=_cparams(("arbitrary",)),
        name="moe_experts",
    )(tile_e, blk, n_act.reshape(1).astype(i32), xs, w_gate, w_up, w_down)

    vec = pl.BlockSpec((None, 1, d), lambda i, *_: (l, 0, 0))
    return pl.pallas_call(
        functools.partial(_combine_kernel, alpha=alpha),
        grid_spec=pltpu.PrefetchScalarGridSpec(
            num_scalar_prefetch=3,
            grid=(nt,),
            in_specs=[pl.BlockSpec((tt, d), lambda i, *_: (i, 0)),
                      pl.BlockSpec((tt, 128), lambda i, *_: (i, 0)),
                      pl.BlockSpec((1, 8, 128), lambda i, *_: (i, 0, 0)),
                      vec, vec,
                      pl.BlockSpec(memory_space=pl.ANY)],
            out_specs=pl.BlockSpec((tt, d), lambda i, *_: (i, 0)),
            scratch_shapes=[pltpu.VMEM((MOE_SLOTS, 2 * d), BF16), pltpu.SemaphoreType.DMA(())],
        ),
        out_shape=jax.ShapeDtypeStruct((n, d), F32),
        compiler_params=_cparams(("arbitrary",)),
        name="moe_combine",
    )(lo_s, dst_s, nch_s, x1, route, lor, g, b, ys)


def _rope_table(pos):
    half = ROT_DIM // 2
    inv = ROPE_THETA ** (-jnp.arange(0, ROT_DIM, 2, dtype=F32) / ROT_DIM)
    ang = jnp.asarray(pos, F32)[:, None] * inv[None, :]
    cos, sin = jnp.cos(ang), jnp.sin(ang)
    n = ang.shape[0]
    z = jnp.zeros((n, HEAD_DIM - ROT_DIM), F32)
    zh = jnp.zeros((n, half), F32)
    c64 = jnp.concatenate([cos, cos, jnp.ones_like(z)], axis=1)
    s1 = jnp.concatenate([-sin, zh, z], axis=1)
    s2 = jnp.concatenate([zh, sin, z], axis=1)
    return jnp.concatenate([c64, c64, s1, s1, s2, s2], axis=1)


def _overlap_table(n_rows):
    c = np.arange(n_rows)[:, None] * CMP_STRIDE
    s = np.arange(128)[None, :] * SEL_BLOCK
    return jnp.asarray(((c < s + SEL_BLOCK) & (c + CMP_BLOCK > s)).astype(np.float32), BF16)


def _block_indicator(n_keys):
    k = np.arange(n_keys)[None, :] // SEL_BLOCK
    return jnp.asarray((np.arange(128)[:, None] == k).astype(np.float32), BF16)


def _block_diag2(w):
    z = jnp.zeros_like(w)
    return jnp.concatenate([jnp.concatenate([w, z], axis=-1), jnp.concatenate([z, w], axis=-1)], axis=-2)


def kernel(x_prompt, x_sample, cache_kv, cache_win, state_conv, state_pool, page_table, ln1_g, ln1_b, w_in, pe_cmp, w_cmp1, w_cmp2, conv_w, pool_w, pool_scale, w_o, ln2_g, ln2_b, w_rg, b_rg, w_re, b_re, w_gate, w_up, w_down):
    n_b, t_len, d_model = x_prompt.shape
    n_seq, t_dec, _ = x_sample.shape
    depth = w_in.shape[0]
    n_phys, page = cache_kv.shape[1], cache_kv.shape[2]
    n_pages = page_table.shape[1]
    past = n_pages * page
    win_rows = cache_win.shape[2]
    alpha = float((2 * depth) ** 0.25)
    assert d_model == 1024 and t_len % 512 == 0 and t_dec == 8 and past % SEL_BLOCK == 0 and win_rows == WINDOW

    pair_order = [h for c in range(4) for h in (c, c + 4)]
    w_main = jnp.concatenate([w_in[:, :, h * 64:(h + 1) * 64] for h in pair_order]
                             + [w_in[:, :, _C_BG:], w_in[:, :, _C_GL:_C_BG],
                              jnp.zeros((depth, d_model, 128 - (_C_BG - _C_GL)), F32)], axis=2)
    wm_p = jnp.concatenate([w_main, w_in[:, :, _C_KV:_C_KV + 256]], axis=2).astype(BF16)
    wm_s = jnp.concatenate([w_main, w_in[:, :, _C_KV + 256:_C_KV + 768]], axis=2).astype(BF16)
    wk_t = jnp.swapaxes(w_in[:, :, _C_KV:_C_GL], 1, 2).astype(BF16)
    cw8 = jnp.concatenate([conv_w, jnp.zeros((depth, 8 - CONV_K, 256), F32)], axis=1)
    pw_bd = jnp.zeros((depth, 256, 256), F32)
    for g in range(4):
        pw_bd = pw_bd.at[:, g * 64:(g + 1) * 64, g * 64:(g + 1) * 64].set(pool_w[:, g])
    pw_bd = pw_bd.astype(BF16)
    ps3 = pool_scale[:, None, :]

    w1r = w_cmp1.reshape(depth, 2, CMP_BLOCK, HEAD_DIM, CMP_HIDDEN)
    bd_half = lambda w: _block_diag2(w).reshape(depth, 2, CMP_STRIDE * 128, 2 * CMP_HIDDEN).astype(BF16)
    wa = bd_half(w1r[:, :, :CMP_STRIDE])
    wb = bd_half(w1r[:, :, CMP_STRIDE:])
    w2 = _block_diag2(w_cmp2).astype(BF16)
    pe2 = jnp.concatenate([pe_cmp, pe_cmp], axis=-1).reshape(depth, 2, 2, CMP_STRIDE, 128)

    wo_perm = jnp.concatenate([w_o[:, h * 64:(h + 1) * 64] for h in pair_order] + [w_o[:, ATT_WIDTH:]],
                              axis=1).astype(BF16)
    wr32 = jnp.concatenate([w_re, w_rg, jnp.zeros((depth, d_model, 128 - N_EXPERTS - N_EXPERT_GROUPS), F32)], axis=2)
    wr_hi = wr32.astype(BF16)
    wr = jnp.concatenate([wr_hi, (wr32 - wr_hi.astype(F32)).astype(BF16)], axis=2)
    br = jnp.concatenate([b_re, b_rg, jnp.zeros((depth, 128 - N_EXPERTS - N_EXPERT_GROUPS), F32)], axis=1)[:, None, :]
    g1, b1, g2, b2 = ln1_g[:, None, :], ln1_b[:, None, :], ln2_g[:, None, :], ln2_b[:, None, :]

    tab_p = _rope_table(np.arange(t_len))
    tab_s = _rope_table(past + np.repeat(np.arange(t_dec), n_seq))
    tab_pt, tab_st = tab_p.T, tab_s.T
    nc_p = t_len // CMP_STRIDE
    nc_s = past // CMP_STRIDE
    ov_p, ov_s = _overlap_table(nc_p), _overlap_table(nc_s)
    et_p = _block_indicator(t_len)
    es_s = _block_indicator(past + 128)

    cache_t = jnp.transpose(cache_kv, (0, 1, 3, 4, 5, 2)).reshape(depth * n_phys, 512, page)
    cwin_t = jnp.transpose(cache_win, (0, 1, 3, 4, 5, 2)).reshape(depth * n_seq, 256, win_rows)
    pt_flat = page_table.reshape(-1)

    def tm_state(st):
        k = st.shape[2]
        st = jnp.swapaxes(st, 1, 2)
        st = jnp.concatenate([jnp.zeros((depth, CARRY_STEPS - k, n_seq, 256), F32), st], axis=1)
        return st.reshape(depth, CARRY_STEPS * n_seq, 256)

    cst_s, pst_s = tm_state(state_conv), tm_state(state_pool)
    zst_p = jnp.zeros((n_b, CARRY_STEPS, 256), F32)

    tm_p = 512
    tm_row = 512
    seqs_step = 4 if n_seq % 4 == 0 else 1

    xp = x_prompt
    xs = x_sample.reshape(n_seq * t_dec, d_model)
    outs = {k: [] for k in ("kv_p", "kv_s", "win_p", "conv_p", "conv_s", "pool_p", "pool_s")}
    win_s_all = None
    for l in range(depth):
        qq, gates, ocp, rows, kvt, wint, kvtb, cnew, pnew = _proj_call(
            xp, wm_p, wk_t, tab_p, tab_pt, cw8, pw_bd, ps3, zst_p, zst_p,
            l=l, tm=tm_p, rs=1, pos0=0, rows_dtype=F32, rows_rope=())
        kc, vc = _compress_rows_call(rows, pe2, wa, wb, w2, l=l)
        oa = _attn_prompt_call(qq, gates, kc, vc, kvtb, et_p, ov_p)
        n_p = n_b * t_len
        x1, route, cnt = _mix_call(oa.reshape(n_p, 512), ocp.reshape(n_p, 512), xp.reshape(n_p, d_model),
                                   wo_perm, g1, b1, wr, br, l=l, alpha=alpha, tm=MOE_TT)
        xp = _moe_call(x1, route, cnt, w_gate, w_up, w_down, g2, b2, l=l, alpha=alpha).reshape(n_b, t_len, d_model)
        outs["kv_p"].append(jnp.transpose(kvt.reshape(n_b, 4, 2, HEAD_DIM, t_len), (0, 4, 1, 2, 3)))
        outs["win_p"].append(jnp.transpose(wint[:, :, t_len - WINDOW:].reshape(n_b, 2, 2, HEAD_DIM, WINDOW),
                                           (0, 4, 1, 2, 3)))
        outs["conv_p"].append(cnew[:, CARRY_STEPS - (CONV_K - 1):])
        outs["pool_p"].append(pnew[:, CARRY_STEPS - POOL_STATE:])

        xs_tm = jnp.swapaxes(xs.reshape(n_seq, t_dec, d_model), 0, 1).reshape(1, t_dec * n_seq, d_model)
        qq, gates, ocp, rows, kvt, wint, kvtb, cnew, pnew = _proj_call(
            xs_tm, wm_s, wk_t, tab_s, tab_st, cw8, pw_bd, ps3, cst_s[l][None], pst_s[l][None],
            l=l, tm=t_dec * n_seq, rs=n_seq, pos0=past, rows_dtype=BF16, rows_rope=(0, 2))
        seq_major = lambda a: jnp.swapaxes(a.reshape(t_dec, n_seq, a.shape[-1]), 0, 1)
        kc, vc = _compress_pages_call(pt_flat, cache_t, pe2, wa, wb, w2, l=l, n_seq=n_seq, n_pages=n_pages,
                                      n_phys=n_phys, seqs_step=seqs_step)
        kc = kc.reshape(n_seq, nc_s, 128)
        vc = vc.reshape(n_seq, nc_s, 128)
        new_win_t = jnp.transpose(wint.reshape(256, t_dec, n_seq), (2, 0, 1))
        oa, win_s_all = _attn_sample_call(pt_flat, cache_t, seq_major(qq), seq_major(gates), kc, vc,
                                          seq_major(rows), cwin_t, new_win_t, es_s, ov_s, win_s_all,
                                          l=l, depth=depth, n_pages=n_pages, n_phys=n_phys, past=past)
        n_s = n_seq * t_dec
        x1, route, cnt = _mix_call(oa.reshape(n_s, 512), seq_major(ocp).reshape(n_s, 512), xs,
                                   wo_perm, g1, b1, wr, br, l=l, alpha=alpha, tm=MOE_TT)
        xs = _moe_call(x1, route, cnt, w_gate, w_up, w_down, g2, b2, l=l, alpha=alpha)
        kvt5 = kvt.reshape(4, 2, HEAD_DIM, t_dec, n_seq)
        outs["kv_s"].append(jnp.transpose(kvt5, (4, 3, 0, 1, 2)))
        st_sm = lambda a, k: jnp.swapaxes(a.reshape(CARRY_STEPS, n_seq, 256)[CARRY_STEPS - k:], 0, 1)
        outs["conv_s"].append(st_sm(cnew, CONV_K - 1))
        outs["pool_s"].append(st_sm(pnew, POOL_STATE))

    st = lambda k: jnp.stack(outs[k])
    win_s = jnp.transpose(win_s_all.reshape(depth, n_seq, 2, 2, HEAD_DIM, win_rows), (0, 1, 5, 2, 3, 4))
    return (xp, xs.reshape(n_seq, t_dec, d_model), st("kv_p"), st("kv_s"), st("win_p"), win_s,
            st("conv_p"), st("conv_s"), st("pool_p"), st("pool_s"))
```

```python
import functools

import numpy as np
import jax
import jax.numpy as jnp
from jax import lax
from jax.experimental import pallas as pl
from jax.experimental.pallas import tpu as pltpu

F32 = jnp.float32
BF16 = jnp.bfloat16

HEAD_DIM = 64
N_HEADS = 8
N_KV_HEADS = 2
GQA_REP = N_HEADS // N_KV_HEADS
ATT_WIDTH = N_HEADS * HEAD_DIM
KV_W = N_KV_HEADS * HEAD_DIM
ROT_DIM = HEAD_DIM // 4
ROPE_THETA = 500000.0
CMP_BLOCK = 32
CMP_STRIDE = 16
CMP_HIDDEN = 256
SEL_BLOCK = 64
SEL_TOP_K = 16
WINDOW = 512
Q_BLOCK = 128
CONV_K = 3
POOL_WINDOWS = (2, 4, 8, 16)
POOL_STATE = max(POOL_WINDOWS) - 1
CARRY_STEPS = 16
N_EXPERT_GROUPS = 4
EXPERTS_PER_GROUP = 8
N_EXPERTS = N_EXPERT_GROUPS * EXPERTS_PER_GROUP
D_EXPERT = 256
LN_EPS = 1e-5
NEG = -1e30
FORCE_BONUS = 1e4
LANES = 128
VMEM_LIMIT = 56 * 1024 * 1024

_C_Q, _C_KV, _C_GL, _C_BG = 0, 512, 1280, 1304
_MAIN_W = 1664


def _cparams(sem):
    return pltpu.CompilerParams(dimension_semantics=sem, vmem_limit_bytes=VMEM_LIMIT)


def _dot(a, b):
    return jnp.dot(a, b, preferred_element_type=F32)


def _dot_nt(a, b):
    return lax.dot_general(a, b, (((1,), (1,)), ((), ())), preferred_element_type=F32)


def _layer_norm(y, g, b):
    mu = jnp.mean(y, axis=-1, keepdims=True)
    d = y - mu
    var = jnp.mean(d * d, axis=-1, keepdims=True)
    return d * lax.rsqrt(var + LN_EPS) * g + b


def _rope_rows(v, tab):
    return (v * tab[:, 0:128] + pltpu.roll(v, 120, 1) * tab[:, 128:256]
            + pltpu.roll(v, 8, 1) * tab[:, 256:384])


def _rope_cols(v, tab):
    return (v * tab[0:128] + pltpu.roll(v, 120, 0) * tab[128:256]
            + pltpu.roll(v, 8, 0) * tab[256:384])


def _proj_kernel(x_ref, wm_ref, wk_ref, tabr_ref, tabt_ref, cw_ref, pw_ref, ps_ref, cst_ref, pst_ref,
                 qq_ref, gate_ref, ocp_ref, rows_ref, kvt_ref, wint_ref, kvtb_ref, cnew_ref, pnew_ref,
                 cu_sc, cp_sc, *, tm, rs, pos0, rows_rope):
    ti = pl.program_id(1)
    carry = CARRY_STEPS * rs

    @pl.when(ti == 0)
    def _():
        cu_sc[...] = cst_ref[0]
        cp_sc[...] = pst_ref[0]

    xb = x_ref[0].astype(BF16)
    h = _dot(xb, wm_ref[...])
    kvt = _dot_nt(wk_ref[...], xb)
    tabr = tabr_ref[...]
    tabt = tabt_ref[...]

    scale = HEAD_DIM ** -0.5
    for c in range(4):
        qc = h[:, c * 128:(c + 1) * 128]
        qq_ref[0, :, c * 128:(c + 1) * 128] = (qc * scale).astype(BF16)
        qq_ref[0, :, 512 + c * 128:512 + (c + 1) * 128] = (_rope_rows(qc, tabr) * scale).astype(BF16)
    gate_ref[0] = 1.0 / (1.0 + jnp.exp(-h[:, 1536:1664]))

    for c in range(6):
        blk = kvt[c * 128:(c + 1) * 128]
        if c in (2, 4):
            blk = _rope_cols(blk, tabt)
        if c < 4:
            kvt_ref[0, c * 128:(c + 1) * 128, :] = blk
        else:
            wint_ref[0, (c - 4) * 128:(c - 3) * 128, :] = blk
        kvtb_ref[0, c * 128:(c + 1) * 128, :] = blk.astype(BF16)

    n_rows = rows_ref.shape[2]
    for c in range(n_rows // 128):
        blk = h[:, _MAIN_W + c * 128:_MAIN_W + (c + 1) * 128]
        if c in rows_rope:
            blk = _rope_rows(blk, tabr)
        rows_ref[0, :, c * 128:(c + 1) * 128] = blk.astype(rows_ref.dtype)

    bg = h[:, 512:768]
    cg = h[:, 768:1024]
    vc = h[:, 1024:1280]
    pin = h[:, 1280:1536]

    u = cg * vc
    eu = jnp.concatenate([cu_sc[...], u], axis=0)
    cw = cw_ref[...]
    y = (eu[carry - 2 * rs:carry - 2 * rs + tm] * cw[0:1] + eu[carry - rs:carry - rs + tm] * cw[1:2]
         + u * cw[2:3])
    o_conv = bg * y
    new_u = eu[tm:tm + carry]
    cu_sc[...] = new_u
    cnew_ref[0] = new_u

    ep = jnp.concatenate([cp_sc[...], pin], axis=0)
    s2 = ep[rs:] + ep[:-rs]
    s4 = s2[2 * rs:] + s2[:-2 * rs]
    s8 = s4[4 * rs:] + s4[:-4 * rs]
    s16 = s8[8 * rs:] + s8[:-8 * rs]
    take = lambda a: a[a.shape[0] - tm:]
    lane = lax.broadcasted_iota(jnp.int32, (tm, 256), 1)
    row = lax.broadcasted_iota(jnp.int32, (tm, 256), 0)
    dsum = jnp.where(lane < 64, take(s2), jnp.where(lane < 128, take(s4), jnp.where(lane < 192, take(s8), take(s16))))
    wlen = jnp.where(lane < 64, 2, jnp.where(lane < 128, 4, jnp.where(lane < 192, 8, 16)))
    step = pos0 + (ti * tm + row) // rs
    cnt = jnp.minimum(wlen, step + 1).astype(F32)
    d = dsum / cnt - pin
    o_pool = _dot(d.astype(BF16), pw_ref[...]) * ps_ref[...]
    new_p = ep[tm:tm + carry]
    cp_sc[...] = new_p
    pnew_ref[0] = new_p

    ocp_ref[0, :, 0:256] = o_conv.astype(BF16)
    ocp_ref[0, :, 256:512] = o_pool.astype(BF16)


def _proj_call(x3, wm, wk, tabr, tabt, cw, pw, ps, cst, pst, *, l, tm, rs, pos0, rows_dtype, rows_rope):
    n_sg, n_rows, d_model = x3.shape
    tiles = n_rows // tm
    n_main = wm.shape[2]
    n_rowcols = n_main - _MAIN_W
    carry = CARRY_STEPS * rs
    kern = functools.partial(_proj_kernel, tm=tm, rs=rs, pos0=pos0, rows_rope=rows_rope)
    out_shape = (
        jax.ShapeDtypeStruct((n_sg, n_rows, 1024), BF16),
        jax.ShapeDtypeStruct((n_sg, n_rows, 128), F32),
        jax.ShapeDtypeStruct((n_sg, n_rows, 512), BF16),
        jax.ShapeDtypeStruct((n_sg, n_rows, n_rowcols), rows_dtype),
        jax.ShapeDtypeStruct((n_sg, 512, n_rows), F32),
        jax.ShapeDtypeStruct((n_sg, 256, n_rows), F32),
        jax.ShapeDtypeStruct((n_sg, 768, n_rows), BF16),
        jax.ShapeDtypeStruct((n_sg, carry, 256), F32),
        jax.ShapeDtypeStruct((n_sg, carry, 256), F32),
    )
    row_blk = lambda w: pl.BlockSpec((1, tm, w), lambda s, t: (s, t, 0))
    col_blk = lambda w: pl.BlockSpec((1, w, tm), lambda s, t: (s, 0, t))
    st_blk = pl.BlockSpec((1, carry, 256), lambda s, t: (s, 0, 0))
    return pl.pallas_call(
        kern,
        grid=(n_sg, tiles),
        in_specs=[
            row_blk(d_model),
            pl.BlockSpec((None, d_model, n_main), lambda s, t: (l, 0, 0)),
            pl.BlockSpec((None, 768, d_model), lambda s, t: (l, 0, 0)),
            pl.BlockSpec((tm, 384), lambda s, t: (t, 0)),
            pl.BlockSpec((384, tm), lambda s, t: (0, t)),
            pl.BlockSpec((None, 8, 256), lambda s, t: (l, 0, 0)),
            pl.BlockSpec((None, 256, 256), lambda s, t: (l, 0, 0)),
            pl.BlockSpec((None, 1, 256), lambda s, t: (l, 0, 0)),
            st_blk, st_blk,
        ],
        out_specs=(row_blk(1024), row_blk(128), row_blk(512), row_blk(n_rowcols),
                   col_blk(512), col_blk(256), col_blk(768), st_blk, st_blk),
        out_shape=out_shape,
        scratch_shapes=[pltpu.VMEM((carry, 256), F32), pltpu.VMEM((carry, 256), F32)],
        compiler_params=_cparams(("arbitrary", "arbitrary")),
        name="proj",
    )(x3, wm, wk, tabr, tabt, cw, pw, ps, cst, pst)


def _gelu_tanh(x):
    return 0.5 * x * (1.0 + jnp.tanh(0.7978845608028654 * (x + 0.044715 * x * x * x)))


def _compress_core(read_rows, n, pe_ref, wa_ref, wb_ref, w2_ref, kc_ref, vc_ref):
    for kv, out_ref in ((0, kc_ref), (1, vc_ref)):
        cols = [read_rows(kv, r) for r in range(CMP_STRIDE)]
        xa = jnp.concatenate([cols[r] + pe_ref[kv, 0, r:r + 1, :] for r in range(CMP_STRIDE)], axis=1)
        xb = jnp.concatenate([cols[r] + pe_ref[kv, 1, r:r + 1, :] for r in range(CMP_STRIDE)], axis=1)
        a = _dot(xa.astype(BF16), wa_ref[kv])
        b = _dot(xb.astype(BF16), wb_ref[kv])
        pre = a + pltpu.roll(b, n - 1, 0)
        hid = _gelu_tanh(pre)
        out_ref[0] = _dot(hid.astype(BF16), w2_ref[kv]).astype(BF16)


def _compress_rows_kernel(k_ref, v_ref, pe_ref, wa_ref, wb_ref, w2_ref, kc_ref, vc_ref, *, n):
    srcs = (k_ref, v_ref)
    read = lambda kv, r: srcs[kv][0, pl.ds(r, n, stride=CMP_STRIDE), :]
    _compress_core(read, n, pe_ref, wa_ref, wb_ref, w2_ref, kc_ref, vc_ref)


def _compress_pages_kernel(pt_ref, *refs, n_pages_step, page, n):
    pages = refs[:n_pages_step]
    pe_ref, wa_ref, wb_ref, w2_ref, kc_ref, vc_ref, kbuf, vbuf = refs[n_pages_step:]
    for i, pg in enumerate(pages):
        kbuf[i * page:(i + 1) * page, :] = pg[0:128, :].T
        vbuf[i * page:(i + 1) * page, :] = pg[128:256, :].T
    bufs = (kbuf, vbuf)
    read = lambda kv, r: bufs[kv][pl.ds(r, n, stride=CMP_STRIDE), :]
    _compress_core(read, n, pe_ref, wa_ref, wb_ref, w2_ref, kc_ref, vc_ref)


def _cmp_weight_specs(l, nidx):
    im4 = (lambda *a: (l, 0, 0, 0))
    im5 = (lambda *a: (l, 0, 0, 0, 0))
    return [
        pl.BlockSpec((None, 2, 2, CMP_STRIDE, 128), im5),
        pl.BlockSpec((None, 2, 2048, 512), im4),
        pl.BlockSpec((None, 2, 2048, 512), im4),
        pl.BlockSpec((None, 2, 512, 128), im4),
    ]


def _compress_rows_call(rows, pe, wa, wb, w2, *, l):
    b, t, _ = rows.shape
    n = t // CMP_STRIDE
    out = jax.ShapeDtypeStruct((b, n, 128), BF16)
    return pl.pallas_call(
        functools.partial(_compress_rows_kernel, n=n),
        grid=(b,),
        in_specs=[pl.BlockSpec((1, t, 128), lambda i: (i, 0, 0)),
                  pl.BlockSpec((1, t, 128), lambda i: (i, 0, 1))] + _cmp_weight_specs(l, 1),
        out_specs=(pl.BlockSpec((1, n, 128), lambda i: (i, 0, 0)),) * 2,
        out_shape=(out, out),
        compiler_params=_cparams(("arbitrary",)),
        name="compress_rows",
    )(rows, rows, pe, wa, wb, w2)


def _compress_pages_call(pt_flat, cache_t, pe, wa, wb, w2, *, l, n_seq, n_pages, n_phys, seqs_step):
    page = cache_t.shape[2]
    n_pages_step = seqs_step * n_pages
    n = n_pages_step * page // CMP_STRIDE
    steps = n_seq // seqs_step

    def page_spec(k):
        s, p = divmod(k, n_pages)
        return pl.BlockSpec((None, 256, page),
                            lambda i, pt: (l * n_phys + pt[(i * seqs_step + s) * n_pages + p], 0, 0))

    out = jax.ShapeDtypeStruct((steps, n, 128), BF16)
    return pl.pallas_call(
        functools.partial(_compress_pages_kernel, n_pages_step=n_pages_step, page=page, n=n),
        grid_spec=pltpu.PrefetchScalarGridSpec(
            num_scalar_prefetch=1,
            grid=(steps,),
            in_specs=[page_spec(k) for k in range(n_pages_step)] + _cmp_weight_specs(l, 2),
            out_specs=(pl.BlockSpec((1, n, 128), lambda i, pt: (i, 0, 0)),) * 2,
            scratch_shapes=[pltpu.VMEM((n_pages_step * page, 128), F32)] * 2,
        ),
        out_shape=(out, out),
        compiler_params=_cparams(("arbitrary",)),
        name="compress_pages",
    )(pt_flat, *([cache_t] * n_pages_step), pe, wa, wb, w2)


def _q_rows(q, tq):
    lo = lax.broadcasted_iota(jnp.int32, (tq, 128), 1) < 64
    cols = [q[:, c * 128:(c + 1) * 128] for c in range(4)]
    return jnp.concatenate([jnp.where(lo, c, 0.0) for c in cols] + [jnp.where(lo, 0.0, c) for c in cols], axis=0)


def _pair_cols(o, tq):
    lo = lax.broadcasted_iota(jnp.int32, (tq, 128), 1) < 64
    return [jnp.where(lo, o[c * tq:(c + 1) * tq], o[(4 + c) * tq:(5 + c) * tq]) for c in range(4)]


def _row_pos(tq, n, pos_base):
    row = lax.broadcasted_iota(jnp.int32, (8 * tq, n), 0)
    return pos_base + (row & (tq - 1))


def _compressed_branch(qc, kc, vc, ov, tq, pos_base):
    nc = kc.shape[0]
    s = _dot_nt(qc, kc)
    pos = _row_pos(tq, nc, pos_base)
    cend = lax.broadcasted_iota(jnp.int32, (8 * tq, nc), 1) * CMP_STRIDE + (CMP_BLOCK - 1)
    vis = cend <= pos
    s = jnp.where(vis, s, NEG)
    e = jnp.where(vis, jnp.exp(s - jnp.max(s, axis=-1, keepdims=True)), 0.0)
    den = jnp.sum(e, axis=-1, keepdims=True)
    p = (e / jnp.where(den > 0.0, den, 1.0)).astype(BF16)
    o = _dot(p, vc)
    imp8 = _dot(p, ov)
    imp = jnp.concatenate([imp8[(4 * g) * tq:(4 * g + 1) * tq] + imp8[(4 * g + 1) * tq:(4 * g + 2) * tq]
                           + imp8[(4 * g + 2) * tq:(4 * g + 3) * tq] + imp8[(4 * g + 3) * tq:(4 * g + 4) * tq]
                           for g in range(2)], axis=0)
    return o, imp


def _select_blocks(imp, tq, pos_base, n_sel):
    row = lax.broadcasted_iota(jnp.int32, (2 * tq, 128), 0)
    blk = lax.broadcasted_iota(jnp.int32, (2 * tq, 128), 1)
    pos = pos_base + (row & (tq - 1))
    cur = pos // SEL_BLOCK
    forced = (blk == 0) | (blk == cur) | (blk == cur - 1)
    valid = blk * SEL_BLOCK <= pos
    score = jnp.where(valid, imp + jnp.where(forced, FORCE_BONUS, 0.0), NEG)

    if 2 * tq >= 128:
        ns8 = -(-n_sel // 8) * 8
        st = score.T[0:ns8]
        sub = lax.broadcasted_iota(jnp.int32, st.shape, 0)
        rank = jnp.zeros(st.shape, F32)
        for b in range(n_sel):
            other = st[b:b + 1, :]
            rank = rank + jnp.where((other > st) | ((other == st) & (sub > b)), 1.0, 0.0)
        keep = jnp.where((rank < SEL_TOP_K) & (sub < n_sel), 1.0, 0.0)
        keep = jnp.concatenate([keep, jnp.zeros((128 - ns8, 2 * tq), F32)], axis=0)
        return keep.T
    rank = jnp.zeros(score.shape, F32)
    for b in range(n_sel):
        other = score[:, b:b + 1]
        rank = rank + jnp.where((other > score) | ((other == score) & (blk > b)), 1.0, 0.0)
    return jnp.where((rank < SEL_TOP_K) & (blk < n_sel), 1.0, 0.0)


def _softmax_pv(pieces):
    m = None
    for s, _, _ in pieces:
        mi = jnp.max(s, axis=-1, keepdims=True)
        m = mi if m is None else jnp.maximum(m, mi)
    den = 0.0
    acc = 0.0
    for s, v, fm in pieces:
        p = jnp.exp(s - m)
        den = den + jnp.sum(p, axis=-1, keepdims=True)
        pb = p.astype(BF16)
        acc = acc + (_dot_nt(pb, v) if fm else _dot(pb, v))
    return acc / den


def _gated_sum(branches, gates, tq):
    lo = lax.broadcasted_iota(jnp.int32, (tq, 128), 1) < 64
    cols = [_pair_cols(o, tq) for o in branches]
    out = []
    for c in range(4):
        acc = 0.0
        for n in range(3):
            ga = gates[:, 3 * c + n:3 * c + n + 1]
            gb = gates[:, 3 * (c + 4) + n:3 * (c + 4) + n + 1]
            acc = acc + cols[n][c] * jnp.where(lo, ga, gb)
        out.append(acc)
    return out


def _attn_prompt_kernel(qq_ref, gate_ref, kc_ref, vc_ref, kst_ref, vst_ref, kwt_ref, vwt_ref, et_ref, ov_ref,
                        o_ref, kaug_sc, m_sc, l_sc, acc_sc, *, tq, t_len, tk):
    j = pl.program_id(1)
    rows = 8 * tq

    @pl.when(j == 0)
    def _():
        kaug_sc[0:128, :] = kst_ref[0]
        kaug_sc[128:256, :] = et_ref[...]

    pos_base = j * tq
    qq = qq_ref[0].astype(F32)
    qc = _q_rows(qq[:, 0:512], tq).astype(BF16)
    qs = _q_rows(qq[:, 512:1024], tq).astype(BF16)

    o_cmp, imp = _compressed_branch(qc, kc_ref[0], vc_ref[0], ov_ref[...], tq, pos_base)
    sel = _select_blocks(imp, tq, pos_base, -(-t_len // SEL_BLOCK))
    bias = jnp.where(sel > 0.5, 0.0, NEG).astype(BF16)
    bias_rows = jnp.concatenate([bias[0:tq]] * 4 + [bias[tq:2 * tq]] * 4, axis=0)
    q_aug = jnp.concatenate([qs, bias_rows], axis=1)

    m_sc[...] = jnp.full((rows, 128), NEG, F32)
    l_sc[...] = jnp.zeros((rows, 128), F32)
    acc_sc[...] = jnp.zeros((rows, 128), F32)

    def update(k0, causal):
        s = _dot(q_aug, kaug_sc[:, pl.ds(k0, tk)])
        if causal:
            kpos = k0 + lax.broadcasted_iota(jnp.int32, (rows, tk), 1)
            s = jnp.where(kpos <= _row_pos(tq, tk, pos_base), s, NEG)
        m_old = m_sc[...]
        m_new = jnp.maximum(m_old, jnp.max(s, axis=-1, keepdims=True))
        alpha = jnp.exp(m_old - m_new)
        p = jnp.exp(s - jnp.concatenate([m_new] * (tk // 128), axis=1))
        l_sc[...] = alpha * l_sc[...] + jnp.sum(p, axis=-1, keepdims=True)
        acc_sc[...] = alpha * acc_sc[...] + _dot_nt(p.astype(BF16), vst_ref[0, :, pl.ds(k0, tk)])
        m_sc[...] = m_new

    n_bulk = (j * tq) // tk

    def bulk(kt, c):
        update(pl.multiple_of(kt * tk, tk), False)
        return c

    lax.fori_loop(0, n_bulk, bulk, 0)
    update(pl.multiple_of(n_bulk * tk, tk), True)
    o_slc = acc_sc[...] / l_sc[...]

    wk = WINDOW + tq
    k0 = pl.multiple_of(jnp.maximum(j * tq - WINDOW, 0), 128)
    s = _dot(qs, kwt_ref[0, :, pl.ds(k0, wk)])
    dist = _row_pos(tq, wk, pos_base) - (k0 + lax.broadcasted_iota(jnp.int32, (rows, wk), 1))
    s = jnp.where((dist >= 0) & (dist < WINDOW), s, NEG)
    o_win = _softmax_pv([(s, vwt_ref[0, :, pl.ds(k0, wk)], True)])

    cols = _gated_sum([o_cmp, o_slc, o_win], gate_ref[0], tq)
    for c in range(4):
        o_ref[0, :, c * 128:(c + 1) * 128] = cols[c].astype(BF16)


def _attn_prompt_call(qq, gates, kc, vc, kvtb, et, ov):
    b, t_len, _ = qq.shape
    tq = Q_BLOCK
    tk = 512
    nc = kc.shape[1]
    kern = functools.partial(_attn_prompt_kernel, tq=tq, t_len=t_len, tk=tk)
    kv_blk = lambda c: pl.BlockSpec((1, 128, t_len), lambda i, j: (i, c, 0))
    return pl.pallas_call(
        kern,
        grid=(b, t_len // tq),
        in_specs=[
            pl.BlockSpec((1, tq, 1024), lambda i, j: (i, j, 0)),
            pl.BlockSpec((1, tq, 128), lambda i, j: (i, j, 0)),
            pl.BlockSpec((1, nc, 128), lambda i, j: (i, 0, 0)),
            pl.BlockSpec((1, nc, 128), lambda i, j: (i, 0, 0)),
            kv_blk(2), kv_blk(3), kv_blk(4), kv_blk(5),
            pl.BlockSpec((128, t_len), lambda i, j: (0, 0)),
            pl.BlockSpec((nc, 128), lambda i, j: (0, 0)),
        ],
        out_specs=pl.BlockSpec((1, tq, 512), lambda i, j: (i, j, 0)),
        out_shape=jax.ShapeDtypeStruct((b, t_len, 512), BF16),
        scratch_shapes=[pltpu.VMEM((256, t_len), BF16), pltpu.VMEM((8 * tq, 128), F32),
                        pltpu.VMEM((8 * tq, 128), F32), pltpu.VMEM((8 * tq, 128), F32)],
        compiler_params=_cparams(("arbitrary", "arbitrary")),
        name="attn_prompt",
    )(qq, gates, kc, vc, kvtb, kvtb, kvtb, kvtb, et, ov)


def _attn_sample_kernel(pt_ref, *refs, n_pages, page, tq, past, win_rows):
    pages = refs[:n_pages]
    qq_ref, gate_ref, kc_ref, vc_ref, new_ref, cwin_ref, neww_ref, es_ref, ov_ref = refs[n_pages:n_pages + 9]
    o_ref, wout_ref = refs[-2:]
    rows = 8 * tq
    pos_base = past

    qq = qq_ref[0].astype(F32)
    qc = _q_rows(qq[:, 0:512], tq).astype(BF16)
    qs = _q_rows(qq[:, 512:1024], tq).astype(BF16)

    o_cmp, imp = _compressed_branch(qc, kc_ref[0], vc_ref[0], ov_ref[...], tq, pos_base)
    sel = _select_blocks(imp, tq, pos_base, -(-(past + tq) // SEL_BLOCK))

    new = jnp.concatenate([new_ref[0].astype(F32), jnp.zeros((128 - tq, 512), F32)], axis=0).astype(BF16)

    kst = jnp.concatenate([pg[0:128, :].astype(BF16) for pg in pages], axis=1)
    vst = jnp.concatenate([pg[128:256, :].astype(BF16) for pg in pages], axis=1)
    n_keys = past + 128
    member = _dot(sel.astype(BF16), es_ref[...])
    member = jnp.concatenate([member[0:tq]] * 4 + [member[tq:2 * tq]] * 4, axis=0)
    kpos = lax.broadcasted_iota(jnp.int32, (rows, n_keys), 1)
    ok = (member > 0.5) & (kpos <= _row_pos(tq, n_keys, pos_base))
    s_past = jnp.where(ok[:, 0:past], _dot(qs, kst), NEG)
    s_new = jnp.where(ok[:, past:], _dot_nt(qs, new[:, 0:128]), NEG)
    o_slc = _softmax_pv([(s_past, vst, True), (s_new, new[:, 128:256], False)])

    cw = cwin_ref[0]
    n_wk = win_rows + 128
    kpos_w = (past - win_rows) + lax.broadcasted_iota(jnp.int32, (rows, n_wk), 1)
    dist = _row_pos(tq, n_wk, pos_base) - kpos_w
    okw = (dist >= 0) & (dist < WINDOW)
    s_old = jnp.where(okw[:, 0:win_rows], _dot(qs, cw[0:128].astype(BF16)), NEG)
    s_nw = jnp.where(okw[:, win_rows:], _dot_nt(qs, new[:, 256:384]), NEG)
    o_win = _softmax_pv([(s_old, cw[128:256].astype(BF16), True), (s_nw, new[:, 384:512], False)])

    cols = _gated_sum([o_cmp, o_slc, o_win], gate_ref[0], tq)
    for c in range(4):
        o_ref[0, :, c * 128:(c + 1) * 128] = cols[c].astype(BF16)

    wout_ref[0] = pltpu.roll(cw, win_rows - tq, 1)
    wout_ref[0, :, win_rows - tq:] = neww_ref[0]


def _attn_sample_call(pt_flat, cache_t, qq, gates, kc, vc, new_rows, cwin_t, new_win_t, es, ov, win_prev, *,
                      l, depth, n_pages, n_phys, past):
    n_seq, tq, _ = qq.shape
    page = cache_t.shape[2]
    win_rows = cwin_t.shape[2]
    nc = kc.shape[1]
    kern = functools.partial(_attn_sample_kernel, n_pages=n_pages, page=page, tq=tq, past=past, win_rows=win_rows)

    def page_spec(p):
        return pl.BlockSpec((None, 256, page), lambda i, pt: (l * n_phys + pt[i * n_pages + p], 1, 0))

    seq_blk = lambda w: pl.BlockSpec((1, tq, w), lambda i, pt: (i, 0, 0))
    win_blk = pl.BlockSpec((1, 256, win_rows), lambda i, pt: (l * n_seq + i, 0, 0))
    in_specs = [page_spec(p) for p in range(n_pages)] + [
        seq_blk(1024), seq_blk(128),
        pl.BlockSpec((1, nc, 128), lambda i, pt: (i, 0, 0)),
        pl.BlockSpec((1, nc, 128), lambda i, pt: (i, 0, 0)),
        seq_blk(512),
        win_blk,
        pl.BlockSpec((1, 256, tq), lambda i, pt: (i, 0, 0)),
        pl.BlockSpec((128, past + 128), lambda i, pt: (0, 0)),
        pl.BlockSpec((nc, 128), lambda i, pt: (0, 0)),
    ]
    operands = [pt_flat] + [cache_t] * n_pages + [qq, gates, kc, vc, new_rows, cwin_t, new_win_t, es, ov]
    aliases = {}
    if win_prev is not None:
        in_specs.append(pl.BlockSpec(memory_space=pl.ANY))
        aliases = {len(operands): 1}
        operands.append(win_prev)
    return pl.pallas_call(
        kern,
        grid_spec=pltpu.PrefetchScalarGridSpec(
            num_scalar_prefetch=1,
            grid=(n_seq,),
            in_specs=in_specs,
            out_specs=(pl.BlockSpec((1, tq, 512), lambda i, pt: (i, 0, 0)), win_blk),
        ),
        out_shape=(jax.ShapeDtypeStruct((n_seq, tq, 512), BF16),
                   jax.ShapeDtypeStruct((depth * n_seq, 256, win_rows), F32)),
        input_output_aliases=aliases,
        compiler_params=_cparams(("arbitrary",)),
        name="attn_sample",
    )(*operands)


def _mix_kernel(oa_ref, ocp_ref, x_ref, wo_ref, g_ref, b_ref, wr_ref, br_ref, x1_ref, route_ref, cnt_ref, *,
                alpha):
    mix = _dot(oa_ref[...], wo_ref[0:512, :]) + _dot(ocp_ref[...], wo_ref[512:1024, :])
    x1 = _layer_norm(alpha * x_ref[...] + mix, g_ref[...], b_ref[...])
    x1_ref[...] = x1

    tm = x1.shape[0]
    xh = x1.astype(BF16)
    xl = (x1 - xh.astype(F32)).astype(BF16)
    hw = _dot(xh, wr_ref[...])
    logits = hw[:, 0:128] + hw[:, 128:256] + _dot(xl, wr_ref[:, 0:128]) + br_ref[...]
    lane = lax.broadcasted_iota(jnp.int32, (tm, 128), 1).astype(F32)
    big = 1e9
    is_g = (lane >= N_EXPERTS) & (lane < N_EXPERTS + N_EXPERT_GROUPS)
    lg = jnp.where(is_g, logits, NEG)
    ge = jnp.where(is_g, jnp.exp(lg - jnp.max(lg, axis=-1, keepdims=True)), 0.0)
    gp = ge / jnp.sum(ge, axis=-1, keepdims=True)
    gw = jnp.max(gp, axis=-1, keepdims=True)
    gidx = jnp.min(jnp.where(is_g & (gp == gw), lane - N_EXPERTS, big), axis=-1, keepdims=True)
    in_g = (lane >= gidx * EXPERTS_PER_GROUP) & (lane < (gidx + 1.0) * EXPERTS_PER_GROUP)
    le = jnp.where(in_g, logits, NEG)
    ee = jnp.where(in_g, jnp.exp(le - jnp.max(le, axis=-1, keepdims=True)), 0.0)
    ep = ee / jnp.sum(ee, axis=-1, keepdims=True)
    w1 = jnp.max(jnp.where(in_g, ep, -1.0), axis=-1, keepdims=True)
    i1 = jnp.min(jnp.where(in_g & (ep == w1), lane, big), axis=-1, keepdims=True)
    rest = in_g & (lane != i1)
    w2 = jnp.max(jnp.where(rest, ep, -1.0), axis=-1, keepdims=True)
    i2 = jnp.min(jnp.where(rest & (ep == w2), lane, big), axis=-1, keepdims=True)
    den = w1 + w2
    route_ref[...] = jnp.where(lane == 0.0, i1, jnp.where(lane == 1.0, i2, jnp.where(
        lane == 2.0, gw * (w1 / den), jnp.where(lane == 3.0, gw * (w2 / den), 0.0))))
    pairs = jnp.sum(jnp.where((lane == i1) | (lane == i2), 1.0, 0.0), axis=0, keepdims=True)
    cnt_ref[0] = jnp.broadcast_to(pairs, (8, 128))


def _mix_call(oa, ocp, x, wo, g, b, wr, br, *, l, alpha, tm):
    n, d = x.shape
    row = lambda w: pl.BlockSpec((tm, w), lambda i: (i, 0))
    vec = lambda w: pl.BlockSpec((None, 1, w), lambda i: (l, 0, 0))
    return pl.pallas_call(
        functools.partial(_mix_kernel, alpha=alpha),
        grid=(n // tm,),
        in_specs=[row(512), row(512), row(d),
                  pl.BlockSpec((None, 1024, d), lambda i: (l, 0, 0)), vec(d), vec(d),
                  pl.BlockSpec((None, d, 256), lambda i: (l, 0, 0)), vec(128)],
        out_specs=(row(d), row(128), pl.BlockSpec((1, 8, 128), lambda i: (i, 0, 0))),
        out_shape=(jax.ShapeDtypeStruct((n, d), F32), jax.ShapeDtypeStruct((n, 128), F32),
                   jax.ShapeDtypeStruct((n // tm, 8, 128), F32)),
        compiler_params=_cparams(("arbitrary",)),
        name="mix_router",
    )(oa, ocp, x, wo, g, b, wr, br)


MOE_CHUNK = 16
MOE_TT = 512
MOE_TE = 256
MOE_SLOTS = 2 * MOE_TT + N_EXPERTS * MOE_CHUNK
MOE_XW = 1024 + LANES


def _chunk_copy(src, s0, dst, d0, sem):
    return pltpu.make_async_copy(src.at[pl.ds(s0, MOE_CHUNK), :], dst.at[pl.ds(d0, MOE_CHUNK), :], sem)


def _for_each_chunk(i, lo_s, dst_s, nch_s, fn):
    def per_expert(e, total):
        k = i * N_EXPERTS + e
        n = nch_s[k]

        def per_chunk(c, carry):
            fn(pl.multiple_of(lo_s[k] + c * MOE_CHUNK, MOE_CHUNK), pl.multiple_of(dst_s[k] + c * MOE_CHUNK, MOE_CHUNK))
            return carry

        lax.fori_loop(0, n, per_chunk, 0)
        return total + n

    return lax.fori_loop(0, N_EXPERTS, per_expert, 0)


def _split3(c):
    h = c.astype(BF16).astype(F32)
    m = (c - h).astype(BF16).astype(F32)
    r = (c - h - m).astype(BF16).astype(F32)
    lane = lax.broadcasted_iota(jnp.int32, (c.shape[0], 128), 1)
    return jnp.where(lane == 0, h, jnp.where(lane == 1, m, jnp.where(lane == 2, r, 0.0)))


def _dispatch_kernel(lo_s, dst_s, nch_s, fst_s, fn_s, x1_ref, route_ref, lov_ref, xs_hbm, xs_sc, z_sc, sem):
    i = pl.program_id(0)
    tt = MOE_TT

    @pl.when(i == 0)
    def _():
        z_sc[...] = jnp.zeros_like(z_sc)

        def per_expert(e, total):
            def per_chunk(c, carry):
                _chunk_copy(z_sc, 0, xs_hbm, pl.multiple_of(fst_s[e] + c * MOE_CHUNK, MOE_CHUNK), sem).start()
                return carry
            lax.fori_loop(0, fn_s[e], per_chunk, 0)
            return total + fn_s[e]

        n_fill = lax.fori_loop(0, N_EXPERTS, per_expert, 0)

        def drain(c, carry):
            _chunk_copy(z_sc, 0, xs_hbm, 0, sem).wait()
            return carry
        lax.fori_loop(0, n_fill, drain, 0)

    route = route_ref[...]
    rt = route.T
    eio = lax.broadcasted_iota(jnp.int32, (N_EXPERTS, tt), 0).astype(F32)
    m1 = eio == rt[0:1]
    m2 = eio == rt[1:2]
    before = (lax.broadcasted_iota(jnp.int32, (tt, tt), 0) < lax.broadcasted_iota(jnp.int32, (tt, tt), 1))
    rank = _dot(jnp.where(m1 | m2, 1.0, 0.0).astype(BF16), jnp.where(before, 1.0, 0.0).astype(BF16))
    slot = jnp.concatenate([lov_ref[0]] * (tt // 128), axis=1) + rank
    s1 = jnp.sum(jnp.where(m1, slot, 0.0), axis=0, keepdims=True)
    s2 = jnp.sum(jnp.where(m2, slot, 0.0), axis=0, keepdims=True)
    sio = lax.broadcasted_iota(jnp.int32, (MOE_SLOTS, tt), 0).astype(F32)
    p1 = jnp.where(sio == s1, 1.0, 0.0).astype(BF16)
    p2 = jnp.where(sio == s2, 1.0, 0.0).astype(BF16)
    xs_sc[:, 0:1024] = _dot(p1 + p2, x1_ref[...].astype(BF16)).astype(BF16)
    cw = _dot(p1, _split3(route[:, 2:3]).astype(BF16)) + _dot(p2, _split3(route[:, 3:4]).astype(BF16))
    xs_sc[:, 1024:MOE_XW] = cw.astype(BF16)

    n_out = _for_each_chunk(i, lo_s, dst_s, nch_s,
                            lambda s0, d0: _chunk_copy(xs_sc, s0, xs_hbm, d0, sem).start())

    def drain(c, carry):
        _chunk_copy(xs_sc, 0, xs_hbm, 0, sem).wait()
        return carry
    lax.fori_loop(0, n_out, drain, 0)


def _experts_kernel(te_s, blk_s, nact_s, xs_ref, wg_ref, wu_ref, wd_ref, y_ref, wgu_sc, wd_sc):
    k = pl.program_id(0)
    active = k < nact_s[0]
    fresh = (k == 0) | (te_s[k] != te_s[jnp.maximum(k - 1, 0)])

    @pl.when(active & fresh)
    def _():
        wgu_sc[:, 0:D_EXPERT] = wg_ref[...].astype(BF16)
        wgu_sc[:, D_EXPERT:2 * D_EXPERT] = wu_ref[...].astype(BF16)
        wd_sc[...] = wd_ref[...].astype(BF16)

    @pl.when(active)
    def _():
        cw = xs_ref[:, 1024:MOE_XW].astype(F32)
        c = cw[:, 0:1] + cw[:, 1:2] + cw[:, 2:3]
        hgu = _dot(xs_ref[:, 0:1024], wgu_sc[...])
        hg = hgu[:, 0:D_EXPERT]
        hid = hg * (1.0 / (1.0 + jnp.exp(-hg))) * hgu[:, D_EXPERT:2 * D_EXPERT]
        y_ref[...] = _dot((hid * c).astype(BF16), wd_sc[...]).astype(BF16)


def _combine_kernel(lo_s, dst_s, nch_s, x1_ref, route_ref, lor_ref, g_ref, b_ref, ys_hbm, o_ref, ys_sc, sem, *,
                    alpha):
    i = pl.program_id(0)
    tt = MOE_TT

    @pl.when(i == 0)
    def _():
        ys_sc[...] = jnp.zeros_like(ys_sc)

    n_in = _for_each_chunk(i, lo_s, dst_s, nch_s,
                           lambda s0, d0: _chunk_copy(ys_hbm, d0, ys_sc, s0, sem).start())

    route = route_ref[...]
    lane = lax.broadcasted_iota(jnp.int32, (tt, 128), 1).astype(F32)
    m1 = lane == route[:, 0:1]
    m2 = lane == route[:, 1:2]
    before = (lax.broadcasted_iota(jnp.int32, (tt, tt), 1) < lax.broadcasted_iota(jnp.int32, (tt, tt), 0))
    rank = _dot(jnp.where(before, 1.0, 0.0).astype(BF16), jnp.where(m1 | m2, 1.0, 0.0).astype(BF16))
    slot = lor_ref[0, 0:1, :] + rank
    s1 = jnp.sum(jnp.where(m1, slot, 0.0), axis=-1, keepdims=True)
    s2 = jnp.sum(jnp.where(m2, slot, 0.0), axis=-1, keepdims=True)
    sio = lax.broadcasted_iota(jnp.int32, (tt, MOE_SLOTS), 1).astype(F32)
    place = jnp.where((sio == s1) | (sio == s2), 1.0, 0.0).astype(BF16)

    def drain(c, carry):
        _chunk_copy(ys_hbm, 0, ys_sc, 0, sem).wait()
        return carry
    lax.fori_loop(0, n_in, drain, 0)

    y = _dot(place, ys_sc[...])
    o_ref[...] = _layer_norm(alpha * x1_ref[...] + y, g_ref[...], b_ref[...])


def _moe_call(x1, route, cnt, w_gate, w_up, w_down, g, b, *, l, alpha):
    n, d = x1.shape
    tt, te, ch, n_e = MOE_TT, MOE_TE, MOE_CHUNK, N_EXPERTS
    nt = n // tt
    n_et = -(-(2 * n + nt * n_e * (ch - 1)) // te) + n_e
    i32 = jnp.int32

    pairs = cnt[:, 0, :n_e].astype(i32)
    pc = (pairs + ch - 1) // ch * ch
    lo = jnp.cumsum(pc, axis=1) - pc
    tot = jnp.sum(pc, axis=0)
    reg = (tot + te - 1) // te * te
    base = jnp.cumsum(reg) - reg
    dst = base[None, :] + jnp.cumsum(pc, axis=0) - pc
    tiles_e = reg // te
    ends = jnp.cumsum(tiles_e)
    n_act = ends[-1]
    k = jnp.arange(n_et, dtype=i32)
    blk = jnp.minimum(k, n_act - 1)
    tile_e = jnp.minimum(jnp.sum((ends[None, :] <= blk[:, None]).astype(i32), axis=1), n_e - 1)
    flat = lambda a: a.reshape(-1).astype(i32)
    lo_s, dst_s, nch_s = flat(lo), flat(dst), flat(pc // ch)
    lov = jnp.broadcast_to(lo.astype(F32)[:, :, None], (nt, n_e, 128))
    lor = jnp.broadcast_to(jnp.pad(lo.astype(F32), ((0, 0), (0, 128 - n_e)))[:, None, :], (nt, 8, 128))

    xs = pl.pallas_call(
        _dispatch_kernel,
        grid_spec=pltpu.PrefetchScalarGridSpec(
            num_scalar_prefetch=5,
            grid=(nt,),
            in_specs=[pl.BlockSpec((tt, d), lambda i, *_: (i, 0)),
                      pl.BlockSpec((tt, 128), lambda i, *_: (i, 0)),
                      pl.BlockSpec((1, n_e, 128), lambda i, *_: (i, 0, 0))],
            out_specs=pl.BlockSpec(memory_space=pl.ANY),
            scratch_shapes=[pltpu.VMEM((MOE_SLOTS, MOE_XW), BF16), pltpu.VMEM((ch, MOE_XW), BF16),
                            pltpu.SemaphoreType.DMA(())],
        ),
        out_shape=jax.ShapeDtypeStruct((n_et * te, MOE_XW), BF16),
        compiler_params=_cparams(("arbitrary",)),
        name="moe_dispatch",
    )(lo_s, dst_s, nch_s, flat(base + tot), flat((reg - tot) // ch), x1, route, lov)

    wspec = lambda r, c: pl.BlockSpec((None, None, r, c), lambda k, te_s, blk_s, na: (l, te_s[k], 0, 0))
    ys = pl.pallas_call(
        _experts_kernel,
        grid_spec=pltpu.PrefetchScalarGridSpec(
            num_scalar_prefetch=3,
            grid=(n_et,),
            in_specs=[pl.BlockSpec((te, MOE_XW), lambda k, te_s, blk_s, na: (blk_s[k], 0)),
                      wspec(d, D_EXPERT), wspec(d, D_EXPERT), wspec(D_EXPERT, d)],
            out_specs=pl.BlockSpec((te, d), lambda k, te_s, blk_s, na: (blk_s[k], 0)),
            scratch_shapes=[pltpu.VMEM((d, 2 * D_EXPERT), BF16), pltpu.VMEM((D_EXPERT, d), BF16)],
        ),
        out_shape=jax.ShapeDtypeStruct((n_et * te, d), BF16),
        compiler_params=_cparams(("arbitrary",)),
        name="moe_experts",
    )(tile_e, blk, n_act.reshape(1).astype(i32), xs, w_gate, w_up, w_down)

    vec = pl.BlockSpec((None, 1, d), lambda i, *_: (l, 0, 0))
    return pl.pallas_call(
        functools.partial(_combine_kernel, alpha=alpha),
        grid_spec=pltpu.PrefetchScalarGridSpec(
            num_scalar_prefetch=3,
            grid=(nt,),
            in_specs=[pl.BlockSpec((tt, d), lambda i, *_: (i, 0)),
                      pl.BlockSpec((tt, 128), lambda i, *_: (i, 0)),
                      pl.BlockSpec((1, 8, 128), lambda i, *_: (i, 0, 0)),
                      vec, vec,
                      pl.BlockSpec(memory_space=pl.ANY)],
            out_specs=pl.BlockSpec((tt, d), lambda i, *_: (i, 0)),
            scratch_shapes=[pltpu.VMEM((MOE_SLOTS, d), BF16), pltpu.SemaphoreType.DMA(())],
        ),
        out_shape=jax.ShapeDtypeStruct((n, d), F32),
        compiler_params=_cparams(("arbitrary",)),
        name="moe_combine",
    )(lo_s, dst_s, nch_s, x1, route, lor, g, b, ys)


def _rope_table(pos):
    half = ROT_DIM // 2
    inv = ROPE_THETA ** (-jnp.arange(0, ROT_DIM, 2, dtype=F32) / ROT_DIM)
    ang = jnp.asarray(pos, F32)[:, None] * inv[None, :]
    cos, sin = jnp.cos(ang), jnp.sin(ang)
    n = ang.shape[0]
    z = jnp.zeros((n, HEAD_DIM - ROT_DIM), F32)
    zh = jnp.zeros((n, half), F32)
    c64 = jnp.concatenate([cos, cos, jnp.ones_like(z)], axis=1)
    s1 = jnp.concatenate([-sin, zh, z], axis=1)
    s2 = jnp.concatenate([zh, sin, z], axis=1)
    return jnp.concatenate([c64, c64, s1, s1, s2, s2], axis=1)


def _overlap_table(n_rows):
    c = np.arange(n_rows)[:, None] * CMP_STRIDE
    s = np.arange(128)[None, :] * SEL_BLOCK
    return jnp.asarray(((c < s + SEL_BLOCK) & (c + CMP_BLOCK > s)).astype(np.float32), BF16)


def _block_indicator(n_keys):
    k = np.arange(n_keys)[None, :] // SEL_BLOCK
    return jnp.asarray((np.arange(128)[:, None] == k).astype(np.float32), BF16)


def _block_diag2(w):
    z = jnp.zeros_like(w)
    return jnp.concatenate([jnp.concatenate([w, z], axis=-1), jnp.concatenate([z, w], axis=-1)], axis=-2)


def kernel(x_prompt, x_sample, cache_kv, cache_win, state_conv, state_pool, page_table, ln1_g, ln1_b, w_in, pe_cmp, w_cmp1, w_cmp2, conv_w, pool_w, pool_scale, w_o, ln2_g, ln2_b, w_rg, b_rg, w_re, b_re, w_gate, w_up, w_down):
    n_b, t_len, d_model = x_prompt.shape
    n_seq, t_dec, _ = x_sample.shape
    depth = w_in.shape[0]
    n_phys, page = cache_kv.shape[1], cache_kv.shape[2]
    n_pages = page_table.shape[1]
    past = n_pages * page
    win_rows = cache_win.shape[2]
    alpha = float((2 * depth) ** 0.25)
    assert d_model == 1024 and t_len % 512 == 0 and t_dec == 8 and past % SEL_BLOCK == 0 and win_rows == WINDOW

    pair_order = [h for c in range(4) for h in (c, c + 4)]
    w_main = jnp.concatenate([w_in[:, :, h * 64:(h + 1) * 64] for h in pair_order]
                             + [w_in[:, :, _C_BG:], w_in[:, :, _C_GL:_C_BG],
                              jnp.zeros((depth, d_model, 128 - (_C_BG - _C_GL)), F32)], axis=2)
    wm_p = jnp.concatenate([w_main, w_in[:, :, _C_KV:_C_KV + 256]], axis=2).astype(BF16)
    wm_s = jnp.concatenate([w_main, w_in[:, :, _C_KV + 256:_C_KV + 768]], axis=2).astype(BF16)
    wk_t = jnp.swapaxes(w_in[:, :, _C_KV:_C_GL], 1, 2).astype(BF16)
    cw8 = jnp.concatenate([conv_w, jnp.zeros((depth, 8 - CONV_K, 256), F32)], axis=1)
    pw_bd = jnp.zeros((depth, 256, 256), F32)
    for g in range(4):
        pw_bd = pw_bd.at[:, g * 64:(g + 1) * 64, g * 64:(g + 1) * 64].set(pool_w[:, g])
    pw_bd = pw_bd.astype(BF16)
    ps3 = pool_scale[:, None, :]

    w1r = w_cmp1.reshape(depth, 2, CMP_BLOCK, HEAD_DIM, CMP_HIDDEN)
    bd_half = lambda w: _block_diag2(w).reshape(depth, 2, CMP_STRIDE * 128, 2 * CMP_HIDDEN).astype(BF16)
    wa = bd_half(w1r[:, :, :CMP_STRIDE])
    wb = bd_half(w1r[:, :, CMP_STRIDE:])
    w2 = _block_diag2(w_cmp2).astype(BF16)
    pe2 = jnp.concatenate([pe_cmp, pe_cmp], axis=-1).reshape(depth, 2, 2, CMP_STRIDE, 128)

    wo_perm = jnp.concatenate([w_o[:, h * 64:(h + 1) * 64] for h in pair_order] + [w_o[:, ATT_WIDTH:]],
                              axis=1).astype(BF16)
    wr32 = jnp.concatenate([w_re, w_rg, jnp.zeros((depth, d_model, 128 - N_EXPERTS - N_EXPERT_GROUPS), F32)], axis=2)
    wr_hi = wr32.astype(BF16)
    wr = jnp.concatenate([wr_hi, (wr32 - wr_hi.astype(F32)).astype(BF16)], axis=2)
    br = jnp.concatenate([b_re, b_rg, jnp.zeros((depth, 128 - N_EXPERTS - N_EXPERT_GROUPS), F32)], axis=1)[:, None, :]
    g1, b1, g2, b2 = ln1_g[:, None, :], ln1_b[:, None, :], ln2_g[:, None, :], ln2_b[:, None, :]

    tab_p = _rope_table(np.arange(t_len))
    tab_s = _rope_table(past + np.repeat(np.arange(t_dec), n_seq))
    tab_pt, tab_st = tab_p.T, tab_s.T
    nc_p = t_len // CMP_STRIDE
    nc_s = past // CMP_STRIDE
    ov_p, ov_s = _overlap_table(nc_p), _overlap_table(nc_s)
    et_p = _block_indicator(t_len)
    es_s = _block_indicator(past + 128)

    cache_t = jnp.transpose(cache_kv, (0, 1, 3, 4, 5, 2)).reshape(depth * n_phys, 512, page)
    cwin_t = jnp.transpose(cache_win, (0, 1, 3, 4, 5, 2)).reshape(depth * n_seq, 256, win_rows)
    pt_flat = page_table.reshape(-1)

    def tm_state(st):
        k = st.shape[2]
        st = jnp.swapaxes(st, 1, 2)
        st = jnp.concatenate([jnp.zeros((depth, CARRY_STEPS - k, n_seq, 256), F32), st], axis=1)
        return st.reshape(depth, CARRY_STEPS * n_seq, 256)

    cst_s, pst_s = tm_state(state_conv), tm_state(state_pool)
    zst_p = jnp.zeros((n_b, CARRY_STEPS, 256), F32)

    tm_p = 512
    tm_row = 512
    seqs_step = 4 if n_seq % 4 == 0 else 1

    xp = x_prompt
    xs = x_sample.reshape(n_seq * t_dec, d_model)
    outs = {k: [] for k in ("kv_p", "kv_s", "win_p", "conv_p", "conv_s", "pool_p", "pool_s")}
    win_s_all = None
    for l in range(depth):
        qq, gates, ocp, rows, kvt, wint, kvtb, cnew, pnew = _proj_call(
            xp, wm_p, wk_t, tab_p, tab_pt, cw8, pw_bd, ps3, zst_p, zst_p,
            l=l, tm=tm_p, rs=1, pos0=0, rows_dtype=F32, rows_rope=())
        kc, vc = _compress_rows_call(rows, pe2, wa, wb, w2, l=l)
        oa = _attn_prompt_call(qq, gates, kc, vc, kvtb, et_p, ov_p)
        n_p = n_b * t_len
        x1, route, cnt = _mix_call(oa.reshape(n_p, 512), ocp.reshape(n_p, 512), xp.reshape(n_p, d_model),
                                   wo_perm, g1, b1, wr, br, l=l, alpha=alpha, tm=MOE_TT)
        xp = _moe_call(x1, route, cnt, w_gate, w_up, w_down, g2, b2, l=l, alpha=alpha).reshape(n_b, t_len, d_model)
        outs["kv_p"].append(jnp.transpose(kvt.reshape(n_b, 4, 2, HEAD_DIM, t_len), (0, 4, 1, 2, 3)))
        outs["win_p"].append(jnp.transpose(wint[:, :, t_len - WINDOW:].reshape(n_b, 2, 2, HEAD_DIM, WINDOW),
                                           (0, 4, 1, 2, 3)))
        outs["conv_p"].append(cnew[:, CARRY_STEPS - (CONV_K - 1):])
        outs["pool_p"].append(pnew[:, CARRY_STEPS - POOL_STATE:])

        xs_tm = jnp.swapaxes(xs.reshape(n_seq, t_dec, d_model), 0, 1).reshape(1, t_dec * n_seq, d_model)
        qq, gates, ocp, rows, kvt, wint, kvtb, cnew, pnew = _proj_call(
            xs_tm, wm_s, wk_t, tab_s, tab_st, cw8, pw_bd, ps3, cst_s[l][None], pst_s[l][None],
            l=l, tm=t_dec * n_seq, rs=n_seq, pos0=past, rows_dtype=BF16, rows_rope=(0, 2))
        seq_major = lambda a: jnp.swapaxes(a.reshape(t_dec, n_seq, a.shape[-1]), 0, 1)
        kc, vc = _compress_pages_call(pt_flat, cache_t, pe2, wa, wb, w2, l=l, n_seq=n_seq, n_pages=n_pages,
                                      n_phys=n_phys, seqs_step=seqs_step)
        kc = kc.reshape(n_seq, nc_s, 128)
        vc = vc.reshape(n_seq, nc_s, 128)
        new_win_t = jnp.transpose(wint.reshape(256, t_dec, n_seq), (2, 0, 1))
        oa, win_s_all = _attn_sample_call(pt_flat, cache_t, seq_major(qq), seq_major(gates), kc, vc,
                                          seq_major(rows), cwin_t, new_win_t, es_s, ov_s, win_s_all,
                                          l=l, depth=depth, n_pages=n_pages, n_phys=n_phys, past=past)
        n_s = n_seq * t_dec
        x1, route, cnt = _mix_call(oa.reshape(n_s, 512), seq_major(ocp).reshape(n_s, 512), xs,
                                   wo_perm, g1, b1, wr, br, l=l, alpha=alpha, tm=MOE_TT)
        xs = _moe_call(x1, route, cnt, w_gate, w_up, w_down, g2, b2, l=l, alpha=alpha)
        kvt5 = kvt.reshape(4, 2, HEAD_DIM, t_dec, n_seq)
        outs["kv_s"].append(jnp.transpose(kvt5, (4, 3, 0, 1, 2)))
        st_sm = lambda a, k: jnp.swapaxes(a.reshape(CARRY_STEPS, n_seq, 256)[CARRY_STEPS - k:], 0, 1)
        outs["conv_s"].append(st_sm(cnew, CONV_K - 1))
        outs["pool_s"].append(st_sm(pnew, POOL_STATE))

    st = lambda k: jnp.stack(outs[k])
    win_s = jnp.transpose(win_s_all.reshape(depth, n_seq, 2, 2, HEAD_DIM, win_rows), (0, 1, 5, 2, 3, 4))
    return (xp, xs.reshape(n_seq, t_dec, d_model), st("kv_p"), st("kv_s"), st("win_p"), win_s,
            st("conv_p"), st("conv_s"), st("pool_p"), st("pool_s"))
```

```python
import functools

import numpy as np
import jax
import jax.numpy as jnp
from jax import lax
from jax.experimental import pallas as pl
from jax.experimental.pallas import tpu as pltpu

F32 = jnp.float32
BF16 = jnp.bfloat16

HEAD_DIM = 64
N_HEADS = 8
N_KV_HEADS = 2
GQA_REP = N_HEADS // N_KV_HEADS
ATT_WIDTH = N_HEADS * HEAD_DIM
KV_W = N_KV_HEADS * HEAD_DIM
ROT_DIM = HEAD_DIM // 4
ROPE_THETA = 500000.0
CMP_BLOCK = 32
CMP_STRIDE = 16
CMP_HIDDEN = 256
SEL_BLOCK = 64
SEL_TOP_K = 16
WINDOW = 512
Q_BLOCK = 128
CONV_K = 3
POOL_WINDOWS = (2, 4, 8, 16)
POOL_STATE = max(POOL_WINDOWS) - 1
CARRY_STEPS = 16
N_EXPERT_GROUPS = 4
EXPERTS_PER_GROUP = 8
N_EXPERTS = N_EXPERT_GROUPS * EXPERTS_PER_GROUP
D_EXPERT = 256
LN_EPS = 1e-5
NEG = -1e30
FORCE_BONUS = 1e4
LANES = 128
VMEM_LIMIT = 56 * 1024 * 1024

_C_Q, _C_KV, _C_GL, _C_BG = 0, 512, 1280, 1304
_MAIN_W = 1664


def _cparams(sem):
    return pltpu.CompilerParams(dimension_semantics=sem, vmem_limit_bytes=VMEM_LIMIT)


def _dot(a, b):
    return jnp.dot(a, b, preferred_element_type=F32)


def _dot_nt(a, b):
    return lax.dot_general(a, b, (((1,), (1,)), ((), ())), preferred_element_type=F32)


def _layer_norm(y, g, b):
    mu = jnp.mean(y, axis=-1, keepdims=True)
    d = y - mu
    var = jnp.mean(d * d, axis=-1, keepdims=True)
    return d * lax.rsqrt(var + LN_EPS) * g + b


def _rope_rows(v, tab):
    return (v * tab[:, 0:128] + pltpu.roll(v, 120, 1) * tab[:, 128:256]
            + pltpu.roll(v, 8, 1) * tab[:, 256:384])


def _rope_cols(v, tab):
    return (v * tab[0:128] + pltpu.roll(v, 120, 0) * tab[128:256]
            + pltpu.roll(v, 8, 0) * tab[256:384])


def _proj_kernel(x_ref, wm_ref, wk_ref, tabr_ref, tabt_ref, cw_ref, pw_ref, ps_ref, cst_ref, pst_ref,
                 qq_ref, gate_ref, ocp_ref, rows_ref, kvt_ref, wint_ref, kvtb_ref, cnew_ref, pnew_ref,
                 cu_sc, cp_sc, *, tm, rs, pos0, rows_rope):
    ti = pl.program_id(1)
    carry = CARRY_STEPS * rs

    @pl.when(ti == 0)
    def _():
        cu_sc[...] = cst_ref[0]
        cp_sc[...] = pst_ref[0]

    xb = x_ref[0].astype(BF16)
    h = _dot(xb, wm_ref[...])
    kvt = _dot_nt(wk_ref[...], xb)
    tabr = tabr_ref[...]
    tabt = tabt_ref[...]

    scale = HEAD_DIM ** -0.5
    for c in range(4):
        qc = h[:, c * 128:(c + 1) * 128]
        qq_ref[0, :, c * 128:(c + 1) * 128] = (qc * scale).astype(BF16)
        qq_ref[0, :, 512 + c * 128:512 + (c + 1) * 128] = (_rope_rows(qc, tabr) * scale).astype(BF16)
    gate_ref[0] = 1.0 / (1.0 + jnp.exp(-h[:, 1536:1664]))

    for c in range(6):
        blk = kvt[c * 128:(c + 1) * 128]
        if c in (2, 4):
            blk = _rope_cols(blk, tabt)
        if c < 4:
            kvt_ref[0, c * 128:(c + 1) * 128, :] = blk
        else:
            wint_ref[0, (c - 4) * 128:(c - 3) * 128, :] = blk
        kvtb_ref[0, c * 128:(c + 1) * 128, :] = blk.astype(BF16)

    n_rows = rows_ref.shape[2]
    for c in range(n_rows // 128):
        blk = h[:, _MAIN_W + c * 128:_MAIN_W + (c + 1) * 128]
        if c in rows_rope:
            blk = _rope_rows(blk, tabr)
        rows_ref[0, :, c * 128:(c + 1) * 128] = blk.astype(rows_ref.dtype)

    bg = h[:, 512:768]
    cg = h[:, 768:1024]
    vc = h[:, 1024:1280]
    pin = h[:, 1280:1536]

    u = cg * vc
    eu = jnp.concatenate([cu_sc[...], u], axis=0)
    cw = cw_ref[...]
    y = (eu[carry - 2 * rs:carry - 2 * rs + tm] * cw[0:1] + eu[carry - rs:carry - rs + tm] * cw[1:2]
         + u * cw[2:3])
    o_conv = bg * y
    new_u = eu[tm:tm + carry]
    cu_sc[...] = new_u
    cnew_ref[0] = new_u

    ep = jnp.concatenate([cp_sc[...], pin], axis=0)
    s2 = ep[rs:] + ep[:-rs]
    s4 = s2[2 * rs:] + s2[:-2 * rs]
    s8 = s4[4 * rs:] + s4[:-4 * rs]
    s16 = s8[8 * rs:] + s8[:-8 * rs]
    take = lambda a: a[a.shape[0] - tm:]
    lane = lax.broadcasted_iota(jnp.int32, (tm, 256), 1)
    row = lax.broadcasted_iota(jnp.int32, (tm, 256), 0)
    dsum = jnp.where(lane < 64, take(s2), jnp.where(lane < 128, take(s4), jnp.where(lane < 192, take(s8), take(s16))))
    wlen = jnp.where(lane < 64, 2, jnp.where(lane < 128, 4, jnp.where(lane < 192, 8, 16)))
    step = pos0 + (ti * tm + row) // rs
    cnt = jnp.minimum(wlen, step + 1).astype(F32)
    d = dsum / cnt - pin
    o_pool = _dot(d.astype(BF16), pw_ref[...]) * ps_ref[...]
    new_p = ep[tm:tm + carry]
    cp_sc[...] = new_p
    pnew_ref[0] = new_p

    ocp_ref[0, :, 0:256] = o_conv.astype(BF16)
    ocp_ref[0, :, 256:512] = o_pool.astype(BF16)


def _proj_call(x3, wm, wk, tabr, tabt, cw, pw, ps, cst, pst, *, l, tm, rs, pos0, rows_dtype, rows_rope):
    n_sg, n_rows, d_model = x3.shape
    tiles = n_rows // tm
    n_main = wm.shape[2]
    n_rowcols = n_main - _MAIN_W
    carry = CARRY_STEPS * rs
    kern = functools.partial(_proj_kernel, tm=tm, rs=rs, pos0=pos0, rows_rope=rows_rope)
    out_shape = (
        jax.ShapeDtypeStruct((n_sg, n_rows, 1024), BF16),
        jax.ShapeDtypeStruct((n_sg, n_rows, 128), F32),
        jax.ShapeDtypeStruct((n_sg, n_rows, 512), BF16),
        jax.ShapeDtypeStruct((n_sg, n_rows, n_rowcols), rows_dtype),
        jax.ShapeDtypeStruct((n_sg, 512, n_rows), F32),
        jax.ShapeDtypeStruct((n_sg, 256, n_rows), F32),
        jax.ShapeDtypeStruct((n_sg, 768, n_rows), BF16),
        jax.ShapeDtypeStruct((n_sg, carry, 256), F32),
        jax.ShapeDtypeStruct((n_sg, carry, 256), F32),
    )
    row_blk = lambda w: pl.BlockSpec((1, tm, w), lambda s, t: (s, t, 0))
    col_blk = lambda w: pl.BlockSpec((1, w, tm), lambda s, t: (s, 0, t))
    st_blk = pl.BlockSpec((1, carry, 256), lambda s, t: (s, 0, 0))
    return pl.pallas_call(
        kern,
        grid=(n_sg, tiles),
        in_specs=[
            row_blk(d_model),
            pl.BlockSpec((None, d_model, n_main), lambda s, t: (l, 0, 0)),
            pl.BlockSpec((None, 768, d_model), lambda s, t: (l, 0, 0)),
            pl.BlockSpec((tm, 384), lambda s, t: (t, 0)),
            pl.BlockSpec((384, tm), lambda s, t: (0, t)),
            pl.BlockSpec((None, 8, 256), lambda s, t: (l, 0, 0)),
            pl.BlockSpec((None, 256, 256), lambda s, t: (l, 0, 0)),
            pl.BlockSpec((None, 1, 256), lambda s, t: (l, 0, 0)),
            st_blk, st_blk,
        ],
        out_specs=(row_blk(1024), row_blk(128), row_blk(512), row_blk(n_rowcols),
                   col_blk(512), col_blk(256), col_blk(768), st_blk, st_blk),
        out_shape=out_shape,
        scratch_shapes=[pltpu.VMEM((carry, 256), F32), pltpu.VMEM((carry, 256), F32)],
        compiler_params=_cparams(("arbitrary", "arbitrary")),
        name="proj",
    )(x3, wm, wk, tabr, tabt, cw, pw, ps, cst, pst)


def _gelu_tanh(x):
    return 0.5 * x * (1.0 + jnp.tanh(0.7978845608028654 * (x + 0.044715 * x * x * x)))


def _compress_core(read_rows, n, pe_ref, wa_ref, wb_ref, w2_ref, kc_ref, vc_ref):
    for kv, out_ref in ((0, kc_ref), (1, vc_ref)):
        cols = [read_rows(kv, r) for r in range(CMP_STRIDE)]
        xa = jnp.concatenate([cols[r] + pe_ref[kv, 0, r:r + 1, :] for r in range(CMP_STRIDE)], axis=1)
        xb = jnp.concatenate([cols[r] + pe_ref[kv, 1, r:r + 1, :] for r in range(CMP_STRIDE)], axis=1)
        a = _dot(xa.astype(BF16), wa_ref[kv])
        b = _dot(xb.astype(BF16), wb_ref[kv])
        pre = a + pltpu.roll(b, n - 1, 0)
        hid = _gelu_tanh(pre)
        out_ref[0] = _dot(hid.astype(BF16), w2_ref[kv]).astype(BF16)


def _compress_rows_kernel(k_ref, v_ref, pe_ref, wa_ref, wb_ref, w2_ref, kc_ref, vc_ref, *, n):
    srcs = (k_ref, v_ref)
    read = lambda kv, r: srcs[kv][0, pl.ds(r, n, stride=CMP_STRIDE), :]
    _compress_core(read, n, pe_ref, wa_ref, wb_ref, w2_ref, kc_ref, vc_ref)


def _compress_pages_kernel(pt_ref, *refs, n_pages_step, page, n):
    pages = refs[:n_pages_step]
    pe_ref, wa_ref, wb_ref, w2_ref, kc_ref, vc_ref, kbuf, vbuf = refs[n_pages_step:]
    for i, pg in enumerate(pages):
        kbuf[i * page:(i + 1) * page, :] = pg[0:128, :].T
        vbuf[i * page:(i + 1) * page, :] = pg[128:256, :].T
    bufs = (kbuf, vbuf)
    read = lambda kv, r: bufs[kv][pl.ds(r, n, stride=CMP_STRIDE), :]
    _compress_core(read, n, pe_ref, wa_ref, wb_ref, w2_ref, kc_ref, vc_ref)


def _cmp_weight_specs(l, nidx):
    im4 = (lambda *a: (l, 0, 0, 0))
    im5 = (lambda *a: (l, 0, 0, 0, 0))
    return [
        pl.BlockSpec((None, 2, 2, CMP_STRIDE, 128), im5),
        pl.BlockSpec((None, 2, 2048, 512), im4),
        pl.BlockSpec((None, 2, 2048, 512), im4),
        pl.BlockSpec((None, 2, 512, 128), im4),
    ]


def _compress_rows_call(rows, pe, wa, wb, w2, *, l):
    b, t, _ = rows.shape
    n = t // CMP_STRIDE
    out = jax.ShapeDtypeStruct((b, n, 128), BF16)
    return pl.pallas_call(
        functools.partial(_compress_rows_kernel, n=n),
        grid=(b,),
        in_specs=[pl.BlockSpec((1, t, 128), lambda i: (i, 0, 0)),
                  pl.BlockSpec((1, t, 128), lambda i: (i, 0, 1))] + _cmp_weight_specs(l, 1),
        out_specs=(pl.BlockSpec((1, n, 128), lambda i: (i, 0, 0)),) * 2,
        out_shape=(out, out),
        compiler_params=_cparams(("arbitrary",)),
        name="compress_rows",
    )(rows, rows, pe, wa, wb, w2)


def _compress_pages_call(pt_flat, cache_t, pe, wa, wb, w2, *, l, n_seq, n_pages, n_phys, seqs_step):
    page = cache_t.shape[2]
    n_pages_step = seqs_step * n_pages
    n = n_pages_step * page // CMP_STRIDE
    steps = n_seq // seqs_step

    def page_spec(k):
        s, p = divmod(k, n_pages)
        return pl.BlockSpec((None, 256, page),
                            lambda i, pt: (l * n_phys + pt[(i * seqs_step + s) * n_pages + p], 0, 0))

    out = jax.ShapeDtypeStruct((steps, n, 128), BF16)
    return pl.pallas_call(
        functools.partial(_compress_pages_kernel, n_pages_step=n_pages_step, page=page, n=n),
        grid_spec=pltpu.PrefetchScalarGridSpec(
            num_scalar_prefetch=1,
            grid=(steps,),
            in_specs=[page_spec(k) for k in range(n_pages_step)] + _cmp_weight_specs(l, 2),
            out_specs=(pl.BlockSpec((1, n, 128), lambda i, pt: (i, 0, 0)),) * 2,
            scratch_shapes=[pltpu.VMEM((n_pages_step * page, 128), F32)] * 2,
        ),
        out_shape=(out, out),
        compiler_params=_cparams(("arbitrary",)),
        name="compress_pages",
    )(pt_flat, *([cache_t] * n_pages_step), pe, wa, wb, w2)


def _q_rows(q, tq):
    lo = lax.broadcasted_iota(jnp.int32, (tq, 128), 1) < 64
    cols = [q[:, c * 128:(c + 1) * 128] for c in range(4)]
    return jnp.concatenate([jnp.where(lo, c, 0.0) for c in cols] + [jnp.where(lo, 0.0, c) for c in cols], axis=0)


def _pair_cols(o, tq):
    lo = lax.broadcasted_iota(jnp.int32, (tq, 128), 1) < 64
    return [jnp.where(lo, o[c * tq:(c + 1) * tq], o[(4 + c) * tq:(5 + c) * tq]) for c in range(4)]


def _row_pos(tq, n, pos_base):
    row = lax.broadcasted_iota(jnp.int32, (8 * tq, n), 0)
    return pos_base + (row & (tq - 1))


def _compressed_branch(qc, kc, vc, ov, tq, pos_base):
    nc = kc.shape[0]
    s = _dot_nt(qc, kc)
    pos = _row_pos(tq, nc, pos_base)
    cend = lax.broadcasted_iota(jnp.int32, (8 * tq, nc), 1) * CMP_STRIDE + (CMP_BLOCK - 1)
    vis = cend <= pos
    s = jnp.where(vis, s, NEG)
    e = jnp.where(vis, jnp.exp(s - jnp.max(s, axis=-1, keepdims=True)), 0.0)
    den = jnp.sum(e, axis=-1, keepdims=True)
    p = (e / jnp.where(den > 0.0, den, 1.0)).astype(BF16)
    o = _dot(p, vc)
    imp8 = _dot(p, ov)
    imp = jnp.concatenate([imp8[(4 * g) * tq:(4 * g + 1) * tq] + imp8[(4 * g + 1) * tq:(4 * g + 2) * tq]
                           + imp8[(4 * g + 2) * tq:(4 * g + 3) * tq] + imp8[(4 * g + 3) * tq:(4 * g + 4) * tq]
                           for g in range(2)], axis=0)
    return o, imp


def _select_blocks(imp, tq, pos_base, n_sel):
    row = lax.broadcasted_iota(jnp.int32, (2 * tq, 128), 0)
    blk = lax.broadcasted_iota(jnp.int32, (2 * tq, 128), 1)
    pos = pos_base + (row & (tq - 1))
    cur = pos // SEL_BLOCK
    forced = (blk == 0) | (blk == cur) | (blk == cur - 1)
    valid = blk * SEL_BLOCK <= pos
    score = jnp.where(valid, imp + jnp.where(forced, FORCE_BONUS, 0.0), NEG)

    if 2 * tq >= 128:
        ns8 = -(-n_sel // 8) * 8
        st = score.T[0:ns8]
        sub = lax.broadcasted_iota(jnp.int32, st.shape, 0)
        rank = jnp.zeros(st.shape, F32)
        for b in range(n_sel):
            other = st[b:b + 1, :]
            rank = rank + jnp.where((other > st) | ((other == st) & (sub > b)), 1.0, 0.0)
        keep = jnp.where((rank < SEL_TOP_K) & (sub < n_sel), 1.0, 0.0)
        keep = jnp.concatenate([keep, jnp.zeros((128 - ns8, 2 * tq), F32)], axis=0)
        return keep.T
    rank = jnp.zeros(score.shape, F32)
    for b in range(n_sel):
        other = score[:, b:b + 1]
        rank = rank + jnp.where((other > score) | ((other == score) & (blk > b)), 1.0, 0.0)
    return jnp.where((rank < SEL_TOP_K) & (blk < n_sel), 1.0, 0.0)


def _softmax_pv(pieces):
    m = None
    for s, _, _ in pieces:
        mi = jnp.max(s, axis=-1, keepdims=True)
        m = mi if m is None else jnp.maximum(m, mi)
    den = 0.0
    acc = 0.0
    for s, v, fm in pieces:
        p = jnp.exp(s - m)
        den = den + jnp.sum(p, axis=-1, keepdims=True)
        pb = p.astype(BF16)
        acc = acc + (_dot_nt(pb, v) if fm else _dot(pb, v))
    return acc / den


def _gated_sum(branches, gates, tq):
    lo = lax.broadcasted_iota(jnp.int32, (tq, 128), 1) < 64
    cols = [_pair_cols(o, tq) for o in branches]
    out = []
    for c in range(4):
        acc = 0.0
        for n in range(3):
            ga = gates[:, 3 * c + n:3 * c + n + 1]
            gb = gates[:, 3 * (c + 4) + n:3 * (c + 4) + n + 1]
            acc = acc + cols[n][c] * jnp.where(lo, ga, gb)
        out.append(acc)
    return out


def _attn_prompt_kernel(qq_ref, gate_ref, kc_ref, vc_ref, kst_ref, vst_ref, kwt_ref, vwt_ref, et_ref, ov_ref,
                        o_ref, kaug_sc, m_sc, l_sc, acc_sc, *, tq, t_len, tk):
    j = pl.program_id(1)
    rows = 8 * tq

    @pl.when(j == 0)
    def _():
        kaug_sc[0:128, :] = kst_ref[0]
        kaug_sc[128:256, :] = et_ref[...]

    pos_base = j * tq
    qq = qq_ref[0].astype(F32)
    qc = _q_rows(qq[:, 0:512], tq).astype(BF16)
    qs = _q_rows(qq[:, 512:1024], tq).astype(BF16)

    o_cmp, imp = _compressed_branch(qc, kc_ref[0], vc_ref[0], ov_ref[...], tq, pos_base)
    sel = _select_blocks(imp, tq, pos_base, -(-t_len // SEL_BLOCK))
    bias = jnp.where(sel > 0.5, 0.0, NEG).astype(BF16)
    bias_rows = jnp.concatenate([bias[0:tq]] * 4 + [bias[tq:2 * tq]] * 4, axis=0)
    q_aug = jnp.concatenate([qs, bias_rows], axis=1)

    m_sc[...] = jnp.full((rows, 128), NEG, F32)
    l_sc[...] = jnp.zeros((rows, 128), F32)
    acc_sc[...] = jnp.zeros((rows, 128), F32)

    def update(k0, causal):
        s = _dot(q_aug, kaug_sc[:, pl.ds(k0, tk)])
        if causal:
            kpos = k0 + lax.broadcasted_iota(jnp.int32, (rows, tk), 1)
            s = jnp.where(kpos <= _row_pos(tq, tk, pos_base), s, NEG)
        m_old = m_sc[...]
        m_new = jnp.maximum(m_old, jnp.max(s, axis=-1, keepdims=True))
        alpha = jnp.exp(m_old - m_new)
        p = jnp.exp(s - jnp.concatenate([m_new] * (tk // 128), axis=1))
        l_sc[...] = alpha * l_sc[...] + jnp.sum(p, axis=-1, keepdims=True)
        acc_sc[...] = alpha * acc_sc[...] + _dot_nt(p.astype(BF16), vst_ref[0, :, pl.ds(k0, tk)])
        m_sc[...] = m_new

    n_bulk = (j * tq) // tk

    def bulk(kt, c):
        update(pl.multiple_of(kt * tk, tk), False)
        return c

    lax.fori_loop(0, n_bulk, bulk, 0)
    update(pl.multiple_of(n_bulk * tk, tk), True)
    o_slc = acc_sc[...] / l_sc[...]

    wk = WINDOW + tq
    k0 = pl.multiple_of(jnp.maximum(j * tq - WINDOW, 0), 128)
    s = _dot(qs, kwt_ref[0, :, pl.ds(k0, wk)])
    dist = _row_pos(tq, wk, pos_base) - (k0 + lax.broadcasted_iota(jnp.int32, (rows, wk), 1))
    s = jnp.where((dist >= 0) & (dist < WINDOW), s, NEG)
    o_win = _softmax_pv([(s, vwt_ref[0, :, pl.ds(k0, wk)], True)])

    cols = _gated_sum([o_cmp, o_slc, o_win], gate_ref[0], tq)
    for c in range(4):
        o_ref[0, :, c * 128:(c + 1) * 128] = cols[c].astype(BF16)


def _attn_prompt_call(qq, gates, kc, vc, kvtb, et, ov):
    b, t_len, _ = qq.shape
    tq = Q_BLOCK
    tk = 512
    nc = kc.shape[1]
    kern = functools.partial(_attn_prompt_kernel, tq=tq, t_len=t_len, tk=tk)
    kv_blk = lambda c: pl.BlockSpec((1, 128, t_len), lambda i, j: (i, c, 0))
    return pl.pallas_call(
        kern,
        grid=(b, t_len // tq),
        in_specs=[
            pl.BlockSpec((1, tq, 1024), lambda i, j: (i, j, 0)),
            pl.BlockSpec((1, tq, 128), lambda i, j: (i, j, 0)),
            pl.BlockSpec((1, nc, 128), lambda i, j: (i, 0, 0)),
            pl.BlockSpec((1, nc, 128), lambda i, j: (i, 0, 0)),
            kv_blk(2), kv_blk(3), kv_blk(4), kv_blk(5),
            pl.BlockSpec((128, t_len), lambda i, j: (0, 0)),
            pl.BlockSpec((nc, 128), lambda i, j: (0, 0)),
        ],
        out_specs=pl.BlockSpec((1, tq, 512), lambda i, j: (i, j, 0)),
        out_shape=jax.ShapeDtypeStruct((b, t_len, 512), BF16),
        scratch_shapes=[pltpu.VMEM((256, t_len), BF16), pltpu.VMEM((8 * tq, 128), F32),
                        pltpu.VMEM((8 * tq, 128), F32), pltpu.VMEM((8 * tq, 128), F32)],
        compiler_params=_cparams(("arbitrary", "arbitrary")),
        name="attn_prompt",
    )(qq, gates, kc, vc, kvtb, kvtb, kvtb, kvtb, et, ov)


def _attn_sample_kernel(pt_ref, *refs, n_pages, page, tq, past, win_rows, spb):
    pages = refs[:spb * n_pages]
    (qq_ref, gate_ref, kc_ref, vc_ref, new_ref, cwin_ref, neww_ref, es_ref,
     ov_ref) = refs[spb * n_pages:spb * n_pages + 9]
    o_ref, wout_ref = refs[-2:]
    rows = 8 * tq
    pos_base = past
    for sq in range(spb):
        seq_pages = pages[sq * n_pages:(sq + 1) * n_pages]
        qq = qq_ref[sq].astype(F32)
        qc = _q_rows(qq[:, 0:512], tq).astype(BF16)
        qs = _q_rows(qq[:, 512:1024], tq).astype(BF16)

        o_cmp, imp = _compressed_branch(qc, kc_ref[sq], vc_ref[sq], ov_ref[...], tq, pos_base)
        sel = _select_blocks(imp, tq, pos_base, -(-(past + tq) // SEL_BLOCK))

        new = jnp.concatenate([new_ref[sq].astype(F32), jnp.zeros((128 - tq, 512), F32)], axis=0).astype(BF16)

        kst = jnp.concatenate([pg[0:128, :].astype(BF16) for pg in seq_pages], axis=1)
        vst = jnp.concatenate([pg[128:256, :].astype(BF16) for pg in seq_pages], axis=1)
        n_keys = past + 128
        member = _dot(sel.astype(BF16), es_ref[...])
        member = jnp.concatenate([member[0:tq]] * 4 + [member[tq:2 * tq]] * 4, axis=0)
        kpos = lax.broadcasted_iota(jnp.int32, (rows, n_keys), 1)
        ok = (member > 0.5) & (kpos <= _row_pos(tq, n_keys, pos_base))
        s_past = jnp.where(ok[:, 0:past], _dot(qs, kst), NEG)
        s_new = jnp.where(ok[:, past:], _dot_nt(qs, new[:, 0:128]), NEG)
        o_slc = _softmax_pv([(s_past, vst, True), (s_new, new[:, 128:256], False)])

        cw = cwin_ref[sq]
        n_wk = win_rows + 128
        kpos_w = (past - win_rows) + lax.broadcasted_iota(jnp.int32, (rows, n_wk), 1)
        dist = _row_pos(tq, n_wk, pos_base) - kpos_w
        okw = (dist >= 0) & (dist < WINDOW)
        s_old = jnp.where(okw[:, 0:win_rows], _dot(qs, cw[0:128].astype(BF16)), NEG)
        s_nw = jnp.where(okw[:, win_rows:], _dot_nt(qs, new[:, 256:384]), NEG)
        o_win = _softmax_pv([(s_old, cw[128:256].astype(BF16), True), (s_nw, new[:, 384:512], False)])

        cols = _gated_sum([o_cmp, o_slc, o_win], gate_ref[sq], tq)
        for c in range(4):
            o_ref[sq, :, c * 128:(c + 1) * 128] = cols[c].astype(BF16)

        wout_ref[sq] = pltpu.roll(cw, win_rows - tq, 1)
        wout_ref[sq, :, win_rows - tq:] = neww_ref[sq]


def _attn_sample_call(pt_flat, cache_t, qq, gates, kc, vc, new_rows, cwin_t, new_win_t, es, ov, win_prev, *,
                      l, depth, n_pages, n_phys, past):
    n_seq, tq, _ = qq.shape
    page = cache_t.shape[2]
    win_rows = cwin_t.shape[2]
    nc = kc.shape[1]
    spb = 2 if n_seq % 2 == 0 else 1
    steps = n_seq // spb
    kern = functools.partial(_attn_sample_kernel, n_pages=n_pages, page=page, tq=tq, past=past, win_rows=win_rows,
                             spb=spb)

    def page_spec(k):
        sq, p = divmod(k, n_pages)
        return pl.BlockSpec((None, 256, page), lambda i, pt: (l * n_phys + pt[(i * spb + sq) * n_pages + p], 1, 0))

    seq_blk = lambda r, w: pl.BlockSpec((spb, r, w), lambda i, pt: (i, 0, 0))
    win_blk = pl.BlockSpec((spb, 256, win_rows), lambda i, pt: (l * steps + i, 0, 0))
    in_specs = [page_spec(k) for k in range(spb * n_pages)] + [
        seq_blk(tq, 1024), seq_blk(tq, 128), seq_blk(nc, 128), seq_blk(nc, 128), seq_blk(tq, 512),
        win_blk, seq_blk(256, tq),
        pl.BlockSpec((128, past + 128), lambda i, pt: (0, 0)),
        pl.BlockSpec((nc, 128), lambda i, pt: (0, 0)),
    ]
    operands = [pt_flat] + [cache_t] * (spb * n_pages) + [qq, gates, kc, vc, new_rows, cwin_t, new_win_t, es, ov]
    aliases = {}
    if win_prev is not None:
        in_specs.append(pl.BlockSpec(memory_space=pl.ANY))
        aliases = {len(operands): 1}
        operands.append(win_prev)
    return pl.pallas_call(
        kern,
        grid_spec=pltpu.PrefetchScalarGridSpec(
            num_scalar_prefetch=1,
            grid=(steps,),
            in_specs=in_specs,
            out_specs=(seq_blk(tq, 512), win_blk),
        ),
        out_shape=(jax.ShapeDtypeStruct((n_seq, tq, 512), BF16),
                   jax.ShapeDtypeStruct((depth * n_seq, 256, win_rows), F32)),
        input_output_aliases=aliases,
        compiler_params=_cparams(("arbitrary",)),
        name="attn_sample",
    )(*operands)


def _mix_kernel(oa_ref, ocp_ref, x_ref, wo_ref, g_ref, b_ref, wr_ref, br_ref, *rest, alpha):
    x1_ref, route_ref, cnt_ref = rest[-3:]
    mix = _dot(oa_ref[...], wo_ref[0:512, :]) + _dot(ocp_ref[...], wo_ref[512:1024, :])
    x1 = _layer_norm(alpha * x_ref[...] + mix, g_ref[...], b_ref[...])
    x1_ref[...] = x1

    tm = x1.shape[0]
    xh = x1.astype(BF16)
    xl = (x1 - xh.astype(F32)).astype(BF16)
    hw = _dot(xh, wr_ref[...])
    logits = hw[:, 0:128] + hw[:, 128:256] + _dot(xl, wr_ref[:, 0:128]) + br_ref[...]
    lane = lax.broadcasted_iota(jnp.int32, (tm, 128), 1).astype(F32)
    big = 1e9
    is_g = (lane >= N_EXPERTS) & (lane < N_EXPERTS + N_EXPERT_GROUPS)
    lg = jnp.where(is_g, logits, NEG)
    ge = jnp.where(is_g, jnp.exp(lg - jnp.max(lg, axis=-1, keepdims=True)), 0.0)
    gp = ge / jnp.sum(ge, axis=-1, keepdims=True)
    gw = jnp.max(gp, axis=-1, keepdims=True)
    gidx = jnp.min(jnp.where(is_g & (gp == gw), lane - N_EXPERTS, big), axis=-1, keepdims=True)
    in_g = (lane >= gidx * EXPERTS_PER_GROUP) & (lane < (gidx + 1.0) * EXPERTS_PER_GROUP)
    le = jnp.where(in_g, logits, NEG)
    ee = jnp.where(in_g, jnp.exp(le - jnp.max(le, axis=-1, keepdims=True)), 0.0)
    ep = ee / jnp.sum(ee, axis=-1, keepdims=True)
    w1 = jnp.max(jnp.where(in_g, ep, -1.0), axis=-1, keepdims=True)
    i1 = jnp.min(jnp.where(in_g & (ep == w1), lane, big), axis=-1, keepdims=True)
    rest = in_g & (lane != i1)
    w2 = jnp.max(jnp.where(rest, ep, -1.0), axis=-1, keepdims=True)
    i2 = jnp.min(jnp.where(rest & (ep == w2), lane, big), axis=-1, keepdims=True)
    den = w1 + w2
    route_ref[...] = jnp.where(lane == 0.0, i1, jnp.where(lane == 1.0, i2, jnp.where(
        lane == 2.0, gw * (w1 / den), jnp.where(lane == 3.0, gw * (w2 / den), 0.0))))
    pairs = jnp.sum(jnp.where((lane == i1) | (lane == i2), 1.0, 0.0), axis=0, keepdims=True)
    cnt_ref[0] = jnp.broadcast_to(pairs, (8, 128))


def _mix_call(oa, ocp, x, wo, g, b, wr, br, *, l, alpha, tm, tile0, n_all, prev):
    n, d = x.shape
    row = lambda w: pl.BlockSpec((tm, w), lambda i: (i, 0))
    out_row = lambda w: pl.BlockSpec((tm, w), lambda i: (i + tile0, 0))
    vec = lambda w: pl.BlockSpec((None, 1, w), lambda i: (l, 0, 0))
    in_specs = [row(512), row(512), row(d),
                pl.BlockSpec((None, 1024, d), lambda i: (l, 0, 0)), vec(d), vec(d),
                pl.BlockSpec((None, d, 256), lambda i: (l, 0, 0)), vec(128)]
    operands = [oa, ocp, x, wo, g, b, wr, br]
    aliases = {}
    if prev is not None:
        aliases = {len(operands) + k: k for k in range(3)}
        in_specs += [pl.BlockSpec(memory_space=pl.ANY)] * 3
        operands += list(prev)
    return pl.pallas_call(
        functools.partial(_mix_kernel, alpha=alpha),
        grid=(n // tm,),
        in_specs=in_specs,
        out_specs=(out_row(d), out_row(128), pl.BlockSpec((1, 8, 128), lambda i: (i + tile0, 0, 0))),
        out_shape=(jax.ShapeDtypeStruct((n_all, d), F32), jax.ShapeDtypeStruct((n_all, 128), F32),
                   jax.ShapeDtypeStruct((n_all // tm, 8, 128), F32)),
        input_output_aliases=aliases,
        compiler_params=_cparams(("arbitrary",)),
        name="mix_router",
    )(*operands)


MOE_CHUNK = 16
MOE_TT = 512
MOE_TE = 256
MOE_SLOTS = 2 * MOE_TT + N_EXPERTS * MOE_CHUNK
MOE_XW = 1024 + LANES


def _chunk_copy(src, s0, dst, d0, sem):
    return pltpu.make_async_copy(src.at[pl.ds(s0, MOE_CHUNK), :], dst.at[pl.ds(d0, MOE_CHUNK), :], sem)


def _for_each_chunk(i, lo_s, dst_s, nch_s, fn):
    def per_expert(e, total):
        k = i * N_EXPERTS + e
        n = nch_s[k]

        def per_chunk(c, carry):
            fn(pl.multiple_of(lo_s[k] + c * MOE_CHUNK, MOE_CHUNK), pl.multiple_of(dst_s[k] + c * MOE_CHUNK, MOE_CHUNK))
            return carry

        lax.fori_loop(0, n, per_chunk, 0)
        return total + n

    return lax.fori_loop(0, N_EXPERTS, per_expert, 0)


def _split3(c):
    h = c.astype(BF16).astype(F32)
    m = (c - h).astype(BF16).astype(F32)
    r = (c - h - m).astype(BF16).astype(F32)
    lane = lax.broadcasted_iota(jnp.int32, (c.shape[0], 128), 1)
    return jnp.where(lane == 0, h, jnp.where(lane == 1, m, jnp.where(lane == 2, r, 0.0)))


def _tile_chunks(i, nch_s):
    return lax.fori_loop(0, N_EXPERTS, lambda e, total: total + nch_s[i * N_EXPERTS + e], 0)


def _wait_chunks(n, src, dst, sem):
    def body(c, carry):
        _chunk_copy(src, 0, dst, 0, sem).wait()
        return carry
    lax.fori_loop(0, n, body, 0)


def _dispatch_kernel(lo_s, dst_s, nch_s, fst_s, fn_s, x1_ref, route_ref, lov_ref, xs_hbm, xs_sc, z_sc, sem):
    i = pl.program_id(0)
    last = pl.num_programs(0) - 1
    tt = MOE_TT
    cur = i % 2
    buf = xs_sc.at[cur]

    @pl.when(i == 0)
    def _():
        z_sc[...] = jnp.zeros_like(z_sc)

        def per_expert(e, total):
            def per_chunk(c, carry):
                _chunk_copy(z_sc, 0, xs_hbm, pl.multiple_of(fst_s[e] + c * MOE_CHUNK, MOE_CHUNK), sem.at[1]).start()
                return carry
            lax.fori_loop(0, fn_s[e], per_chunk, 0)
            return total + fn_s[e]

        _wait_chunks(lax.fori_loop(0, N_EXPERTS, per_expert, 0), z_sc, xs_hbm, sem.at[1])

    @pl.when(i >= 2)
    def _():
        _wait_chunks(_tile_chunks(i - 2, nch_s), buf, xs_hbm, sem.at[cur])

    route = route_ref[...]
    rt = route.T
    eio = lax.broadcasted_iota(jnp.int32, (N_EXPERTS, tt), 0).astype(F32)
    m1 = eio == rt[0:1]
    m2 = eio == rt[1:2]
    before = (lax.broadcasted_iota(jnp.int32, (tt, tt), 0) < lax.broadcasted_iota(jnp.int32, (tt, tt), 1))
    rank = _dot(jnp.where(m1 | m2, 1.0, 0.0).astype(BF16), jnp.where(before, 1.0, 0.0).astype(BF16))
    slot = jnp.concatenate([lov_ref[0]] * (tt // 128), axis=1) + rank
    s1 = jnp.sum(jnp.where(m1, slot, 0.0), axis=0, keepdims=True)
    s2 = jnp.sum(jnp.where(m2, slot, 0.0), axis=0, keepdims=True)
    sio = lax.broadcasted_iota(jnp.int32, (MOE_SLOTS, tt), 0).astype(F32)
    p1 = jnp.where(sio == s1, 1.0, 0.0).astype(BF16)
    p2 = jnp.where(sio == s2, 1.0, 0.0).astype(BF16)
    buf[:, 0:1024] = _dot(p1 + p2, x1_ref[...].astype(BF16)).astype(BF16)
    cw = _dot(p1, _split3(route[:, 2:3]).astype(BF16)) + _dot(p2, _split3(route[:, 3:4]).astype(BF16))
    buf[:, 1024:MOE_XW] = cw.astype(BF16)

    n_out = _for_each_chunk(i, lo_s, dst_s, nch_s,
                            lambda s0, d0: _chunk_copy(buf, s0, xs_hbm, d0, sem.at[cur]).start())

    @pl.when(i == last)
    def _():
        _wait_chunks(n_out, buf, xs_hbm, sem.at[cur])

        @pl.when(i >= 1)
        def _():
            _wait_chunks(_tile_chunks(i - 1, nch_s), xs_sc.at[1 - cur], xs_hbm, sem.at[1 - cur])


def _experts_kernel(te_s, blk_s, nact_s, xs_ref, wg_ref, wu_ref, wd_ref, y_ref, wgu_sc, wd_sc):
    k = pl.program_id(0)
    active = k < nact_s[0]
    fresh = (k == 0) | (te_s[k] != te_s[jnp.maximum(k - 1, 0)])

    @pl.when(active & fresh)
    def _():
        wgu_sc[:, 0:D_EXPERT] = wg_ref[...].astype(BF16)
        wgu_sc[:, D_EXPERT:2 * D_EXPERT] = wu_ref[...].astype(BF16)
        wd_sc[...] = wd_ref[...].astype(BF16)

    @pl.when(active)
    def _():
        cw = xs_ref[:, 1024:MOE_XW].astype(F32)
        c = cw[:, 0:1] + cw[:, 1:2] + cw[:, 2:3]
        hgu = _dot(xs_ref[:, 0:1024], wgu_sc[...])
        hg = hgu[:, 0:D_EXPERT]
        hid = hg * (1.0 / (1.0 + jnp.exp(-hg))) * hgu[:, D_EXPERT:2 * D_EXPERT]
        y_ref[...] = _dot((hid * c).astype(BF16), wd_sc[...]).astype(BF16)


def _combine_kernel(lo_s, dst_s, nch_s, x1_ref, route_ref, lor_ref, g_ref, b_ref, ys_hbm, oa_ref, ob_ref, ys_sc,
                    sem, *, alpha, nt_a):
    i = pl.program_id(0)
    tt = MOE_TT
    cur = i % 2

    def fetch(tile, slot):
        return _for_each_chunk(tile, lo_s, dst_s, nch_s,
                               lambda s0, d0: _chunk_copy(ys_hbm, d0, ys_sc.at[slot], s0, sem.at[slot]).start())

    @pl.when(i == 0)
    def _():
        ys_sc[...] = jnp.zeros_like(ys_sc)
        fetch(0, 0)

    @pl.when(i + 1 < pl.num_programs(0))
    def _():
        fetch(i + 1, 1 - cur)

    route = route_ref[...]
    lane = lax.broadcasted_iota(jnp.int32, (tt, 128), 1).astype(F32)
    m1 = lane == route[:, 0:1]
    m2 = lane == route[:, 1:2]
    before = (lax.broadcasted_iota(jnp.int32, (tt, tt), 1) < lax.broadcasted_iota(jnp.int32, (tt, tt), 0))
    rank = _dot(jnp.where(before, 1.0, 0.0).astype(BF16), jnp.where(m1 | m2, 1.0, 0.0).astype(BF16))
    slot = lor_ref[0, 0:1, :] + rank
    s1 = jnp.sum(jnp.where(m1, slot, 0.0), axis=-1, keepdims=True)
    s2 = jnp.sum(jnp.where(m2, slot, 0.0), axis=-1, keepdims=True)
    sio = lax.broadcasted_iota(jnp.int32, (tt, MOE_SLOTS), 1).astype(F32)
    place = jnp.where((sio == s1) | (sio == s2), 1.0, 0.0).astype(BF16)

    _wait_chunks(_tile_chunks(i, nch_s), ys_hbm, ys_sc.at[cur], sem.at[cur])
    y = _dot(place, ys_sc[cur])
    out = _layer_norm(alpha * x1_ref[...] + y, g_ref[...], b_ref[...])

    @pl.when(i < nt_a)
    def _():
        oa_ref[...] = out

    @pl.when(i >= nt_a)
    def _():
        ob_ref[...] = out


def _moe_call(x1, route, cnt, w_gate, w_up, w_down, g, b, *, l, alpha, nt_a):
    n, d = x1.shape
    tt, te, ch, n_e = MOE_TT, MOE_TE, MOE_CHUNK, N_EXPERTS
    nt = n // tt
    n_et = -(-(2 * n + nt * n_e * (ch - 1)) // te) + n_e
    i32 = jnp.int32

    pairs = cnt[:, 0, :n_e].astype(i32)
    pc = (pairs + ch - 1) // ch * ch
    lo = jnp.cumsum(pc, axis=1) - pc
    tot = jnp.sum(pc, axis=0)
    reg = (tot + te - 1) // te * te
    base = jnp.cumsum(reg) - reg
    dst = base[None, :] + jnp.cumsum(pc, axis=0) - pc
    tiles_e = reg // te
    ends = jnp.cumsum(tiles_e)
    n_act = ends[-1]
    k = jnp.arange(n_et, dtype=i32)
    blk = jnp.minimum(k, n_act - 1)
    tile_e = jnp.minimum(jnp.sum((ends[None, :] <= blk[:, None]).astype(i32), axis=1), n_e - 1)
    flat = lambda a: a.reshape(-1).astype(i32)
    lo_s, dst_s, nch_s = flat(lo), flat(dst), flat(pc // ch)
    lov = jnp.broadcast_to(lo.astype(F32)[:, :, None], (nt, n_e, 128))
    lor = jnp.broadcast_to(jnp.pad(lo.astype(F32), ((0, 0), (0, 128 - n_e)))[:, None, :], (nt, 8, 128))

    xs = pl.pallas_call(
        _dispatch_kernel,
        grid_spec=pltpu.PrefetchScalarGridSpec(
            num_scalar_prefetch=5,
            grid=(nt,),
            in_specs=[pl.BlockSpec((tt, d), lambda i, *_: (i, 0)),
                      pl.BlockSpec((tt, 128), lambda i, *_: (i, 0)),
                      pl.BlockSpec((1, n_e, 128), lambda i, *_: (i, 0, 0))],
            out_specs=pl.BlockSpec(memory_space=pl.ANY),
            scratch_shapes=[pltpu.VMEM((2, MOE_SLOTS, MOE_XW), BF16), pltpu.VMEM((ch, MOE_XW), BF16),
                            pltpu.SemaphoreType.DMA((2,))],
        ),
        out_shape=jax.ShapeDtypeStruct((n_et * te, MOE_XW), BF16),
        compiler_params=_cparams(("arbitrary",)),
        name="moe_dispatch",
    )(lo_s, dst_s, nch_s, flat(base + tot), flat((reg - tot) // ch), x1, route, lov)

    wspec = lambda r, c: pl.BlockSpec((None, None, r, c), lambda k, te_s, blk_s, na: (l, te_s[k], 0, 0))
    ys = pl.pallas_call(
        _experts_kernel,
        grid_spec=pltpu.PrefetchScalarGridSpec(
            num_scalar_prefetch=3,
            grid=(n_et,),
            in_specs=[pl.BlockSpec((te, MOE_XW), lambda k, te_s, blk_s, na: (blk_s[k], 0)),
                      wspec(d, D_EXPERT), wspec(d, D_EXPERT), wspec(D_EXPERT, d)],
            out_specs=pl.BlockSpec((te, d), lambda k, te_s, blk_s, na: (blk_s[k], 0)),
            scratch_shapes=[pltpu.VMEM((d, 2 * D_EXPERT), BF16), pltpu.VMEM((D_EXPERT, d), BF16)],
        ),
        out_shape=jax.ShapeDtypeStruct((n_et * te, d), BF16),
        compiler_params=_cparams(("arbitrary",)),
        name="moe_experts",
    )(tile_e, blk, n_act.reshape(1).astype(i32), xs, w_gate, w_up, w_down)

    vec = pl.BlockSpec((None, 1, d), lambda i, *_: (l, 0, 0))
    return pl.pallas_call(
        functools.partial(_combine_kernel, alpha=alpha, nt_a=nt_a),
        grid_spec=pltpu.PrefetchScalarGridSpec(
            num_scalar_prefetch=3,
            grid=(nt,),
            in_specs=[pl.BlockSpec((tt, d), lambda i, *_: (i, 0)),
                      pl.BlockSpec((tt, 128), lambda i, *_: (i, 0)),
                      pl.BlockSpec((1, 8, 128), lambda i, *_: (i, 0, 0)),
                      vec, vec,
                      pl.BlockSpec(memory_space=pl.ANY)],
            out_specs=(pl.BlockSpec((tt, d), lambda i, *_: (jnp.minimum(i, nt_a - 1), 0)),
                       pl.BlockSpec((tt, d), lambda i, *_: (jnp.maximum(i - nt_a, 0), 0))),
            scratch_shapes=[pltpu.VMEM((2, MOE_SLOTS, d), BF16), pltpu.SemaphoreType.DMA((2,))],
        ),
        out_shape=(jax.ShapeDtypeStruct((nt_a * tt, d), F32), jax.ShapeDtypeStruct((n - nt_a * tt, d), F32)),
        compiler_params=_cparams(("arbitrary",)),
        name="moe_combine",
    )(lo_s, dst_s, nch_s, x1, route, lor, g, b, ys)


def _rope_table(pos):
    half = ROT_DIM // 2
    inv = ROPE_THETA ** (-jnp.arange(0, ROT_DIM, 2, dtype=F32) / ROT_DIM)
    ang = jnp.asarray(pos, F32)[:, None] * inv[None, :]
    cos, sin = jnp.cos(ang), jnp.sin(ang)
    n = ang.shape[0]
    z = jnp.zeros((n, HEAD_DIM - ROT_DIM), F32)
    zh = jnp.zeros((n, half), F32)
    c64 = jnp.concatenate([cos, cos, jnp.ones_like(z)], axis=1)
    s1 = jnp.concatenate([-sin, zh, z], axis=1)
    s2 = jnp.concatenate([zh, sin, z], axis=1)
    return jnp.concatenate([c64, c64, s1, s1, s2, s2], axis=1)


def _overlap_table(n_rows):
    c = np.arange(n_rows)[:, None] * CMP_STRIDE
    s = np.arange(128)[None, :] * SEL_BLOCK
    return jnp.asarray(((c < s + SEL_BLOCK) & (c + CMP_BLOCK > s)).astype(np.float32), BF16)


def _block_indicator(n_keys):
    k = np.arange(n_keys)[None, :] // SEL_BLOCK
    return jnp.asarray((np.arange(128)[:, None] == k).astype(np.float32), BF16)


def _block_diag2(w):
    z = jnp.zeros_like(w)
    return jnp.concatenate([jnp.concatenate([w, z], axis=-1), jnp.concatenate([z, w], axis=-1)], axis=-2)


def kernel(x_prompt, x_sample, cache_kv, cache_win, state_conv, state_pool, page_table, ln1_g, ln1_b, w_in, pe_cmp, w_cmp1, w_cmp2, conv_w, pool_w, pool_scale, w_o, ln2_g, ln2_b, w_rg, b_rg, w_re, b_re, w_gate, w_up, w_down):
    n_b, t_len, d_model = x_prompt.shape
    n_seq, t_dec, _ = x_sample.shape
    depth = w_in.shape[0]
    n_phys, page = cache_kv.shape[1], cache_kv.shape[2]
    n_pages = page_table.shape[1]
    past = n_pages * page
    win_rows = cache_win.shape[2]
    alpha = float((2 * depth) ** 0.25)
    assert d_model == 1024 and t_len % 512 == 0 and t_dec == 8 and past % SEL_BLOCK == 0 and win_rows == WINDOW

    pair_order = [h for c in range(4) for h in (c, c + 4)]
    w_main = jnp.concatenate([w_in[:, :, h * 64:(h + 1) * 64] for h in pair_order]
                             + [w_in[:, :, _C_BG:], w_in[:, :, _C_GL:_C_BG],
                              jnp.zeros((depth, d_model, 128 - (_C_BG - _C_GL)), F32)], axis=2)
    wm_p = jnp.concatenate([w_main, w_in[:, :, _C_KV:_C_KV + 256]], axis=2).astype(BF16)
    wm_s = jnp.concatenate([w_main, w_in[:, :, _C_KV + 256:_C_KV + 768]], axis=2).astype(BF16)
    wk_t = jnp.swapaxes(w_in[:, :, _C_KV:_C_GL], 1, 2).astype(BF16)
    cw8 = jnp.concatenate([conv_w, jnp.zeros((depth, 8 - CONV_K, 256), F32)], axis=1)
    pw_bd = jnp.zeros((depth, 256, 256), F32)
    for g in range(4):
        pw_bd = pw_bd.at[:, g * 64:(g + 1) * 64, g * 64:(g + 1) * 64].set(pool_w[:, g])
    pw_bd = pw_bd.astype(BF16)
    ps3 = pool_scale[:, None, :]

    w1r = w_cmp1.reshape(depth, 2, CMP_BLOCK, HEAD_DIM, CMP_HIDDEN)
    bd_half = lambda w: _block_diag2(w).reshape(depth, 2, CMP_STRIDE * 128, 2 * CMP_HIDDEN).astype(BF16)
    wa = bd_half(w1r[:, :, :CMP_STRIDE])
    wb = bd_half(w1r[:, :, CMP_STRIDE:])
    w2 = _block_diag2(w_cmp2).astype(BF16)
    pe2 = jnp.concatenate([pe_cmp, pe_cmp], axis=-1).reshape(depth, 2, 2, CMP_STRIDE, 128)

    wo_perm = jnp.concatenate([w_o[:, h * 64:(h + 1) * 64] for h in pair_order] + [w_o[:, ATT_WIDTH:]],
                              axis=1).astype(BF16)
    wr32 = jnp.concatenate([w_re, w_rg, jnp.zeros((depth, d_model, 128 - N_EXPERTS - N_EXPERT_GROUPS), F32)], axis=2)
    wr_hi = wr32.astype(BF16)
    wr = jnp.concatenate([wr_hi, (wr32 - wr_hi.astype(F32)).astype(BF16)], axis=2)
    br = jnp.concatenate([b_re, b_rg, jnp.zeros((depth, 128 - N_EXPERTS - N_EXPERT_GROUPS), F32)], axis=1)[:, None, :]
    g1, b1, g2, b2 = ln1_g[:, None, :], ln1_b[:, None, :], ln2_g[:, None, :], ln2_b[:, None, :]

    tab_p = _rope_table(np.arange(t_len))
    tab_s = _rope_table(past + np.repeat(np.arange(t_dec), n_seq))
    tab_pt, tab_st = tab_p.T, tab_s.T
    nc_p = t_len // CMP_STRIDE
    nc_s = past // CMP_STRIDE
    ov_p, ov_s = _overlap_table(nc_p), _overlap_table(nc_s)
    et_p = _block_indicator(t_len)
    es_s = _block_indicator(past + 128)

    cache_t = jnp.transpose(cache_kv, (0, 1, 3, 4, 5, 2)).reshape(depth * n_phys, 512, page)
    cwin_t = jnp.transpose(cache_win, (0, 1, 3, 4, 5, 2)).reshape(depth * n_seq, 256, win_rows)
    pt_flat = page_table.reshape(-1)

    def tm_state(st):
        k = st.shape[2]
        st = jnp.swapaxes(st, 1, 2)
        st = jnp.concatenate([jnp.zeros((depth, CARRY_STEPS - k, n_seq, 256), F32), st], axis=1)
        return st.reshape(depth, CARRY_STEPS * n_seq, 256)

    cst_s, pst_s = tm_state(state_conv), tm_state(state_pool)
    zst_p = jnp.zeros((n_b, CARRY_STEPS, 256), F32)

    tm_p = 512
    tm_row = 512
    seqs_step = 4 if n_seq % 4 == 0 else 1

    xp = x_prompt
    xs = x_sample.reshape(n_seq * t_dec, d_model)
    outs = {k: [] for k in ("kv_p", "kv_s", "win_p", "conv_p", "conv_s", "pool_p", "pool_s")}
    win_s_all = None
    for l in range(depth):
        qq, gates, ocp, rows, kvt, wint, kvtb, cnew, pnew = _proj_call(
            xp, wm_p, wk_t, tab_p, tab_pt, cw8, pw_bd, ps3, zst_p, zst_p,
            l=l, tm=tm_p, rs=1, pos0=0, rows_dtype=F32, rows_rope=())
        kc, vc = _compress_rows_call(rows, pe2, wa, wb, w2, l=l)
        oa = _attn_prompt_call(qq, gates, kc, vc, kvtb, et_p, ov_p)
        n_p = n_b * t_len
        n_s = n_seq * t_dec
        routed = _mix_call(oa.reshape(n_p, 512), ocp.reshape(n_p, 512), xp.reshape(n_p, d_model),
                           wo_perm, g1, b1, wr, br, l=l, alpha=alpha, tm=MOE_TT, tile0=0, n_all=n_p + n_s,
                           prev=None)
        outs["kv_p"].append(jnp.transpose(kvt.reshape(n_b, 4, 2, HEAD_DIM, t_len), (0, 4, 1, 2, 3)))
        outs["win_p"].append(jnp.transpose(wint[:, :, t_len - WINDOW:].reshape(n_b, 2, 2, HEAD_DIM, WINDOW),
                                           (0, 4, 1, 2, 3)))
        outs["conv_p"].append(cnew[:, CARRY_STEPS - (CONV_K - 1):])
        outs["pool_p"].append(pnew[:, CARRY_STEPS - POOL_STATE:])

        xs_tm = jnp.swapaxes(xs.reshape(n_seq, t_dec, d_model), 0, 1).reshape(1, t_dec * n_seq, d_model)
        qq, gates, ocp, rows, kvt, wint, kvtb, cnew, pnew = _proj_call(
            xs_tm, wm_s, wk_t, tab_s, tab_st, cw8, pw_bd, ps3, cst_s[l][None], pst_s[l][None],
            l=l, tm=t_dec * n_seq, rs=n_seq, pos0=past, rows_dtype=BF16, rows_rope=(0, 2))
        seq_major = lambda a: jnp.swapaxes(a.reshape(t_dec, n_seq, a.shape[-1]), 0, 1)
        kc, vc = _compress_pages_call(pt_flat, cache_t, pe2, wa, wb, w2, l=l, n_seq=n_seq, n_pages=n_pages,
                                      n_phys=n_phys, seqs_step=seqs_step)
        kc = kc.reshape(n_seq, nc_s, 128)
        vc = vc.reshape(n_seq, nc_s, 128)
        new_win_t = jnp.transpose(wint.reshape(256, t_dec, n_seq), (2, 0, 1))
        oa, win_s_all = _attn_sample_call(pt_flat, cache_t, seq_major(qq), seq_major(gates), kc, vc,
                                          seq_major(rows), cwin_t, new_win_t, es_s, ov_s, win_s_all,
                                          l=l, depth=depth, n_pages=n_pages, n_phys=n_phys, past=past)
        x1, route, cnt = _mix_call(oa.reshape(n_s, 512), seq_major(ocp).reshape(n_s, 512), xs,
                                   wo_perm, g1, b1, wr, br, l=l, alpha=alpha, tm=MOE_TT, tile0=n_p // MOE_TT,
                                   n_all=n_p + n_s, prev=routed)
        xp, xs = _moe_call(x1, route, cnt, w_gate, w_up, w_down, g2, b2, l=l, alpha=alpha, nt_a=n_p // MOE_TT)
        xp = xp.reshape(n_b, t_len, d_model)
        kvt5 = kvt.reshape(4, 2, HEAD_DIM, t_dec, n_seq)
        outs["kv_s"].append(jnp.transpose(kvt5, (4, 3, 0, 1, 2)))
        st_sm = lambda a, k: jnp.swapaxes(a.reshape(CARRY_STEPS, n_seq, 256)[CARRY_STEPS - k:], 0, 1)
        outs["conv_s"].append(st_sm(cnew, CONV_K - 1))
        outs["pool_s"].append(st_sm(pnew, POOL_STATE))

    st = lambda k: jnp.stack(outs[k])
    win_s = jnp.transpose(win_s_all.reshape(depth, n_seq, 2, 2, HEAD_DIM, win_rows), (0, 1, 5, 2, 3, 4))
    return (xp, xs.reshape(n_seq, t_dec, d_model), st("kv_p"), st("kv_s"), st("win_p"), win_s,
            st("conv_p"), st("conv_s"), st("pool_p"), st("pool_s"))
```

```python
import functools

import numpy as np
import jax
import jax.numpy as jnp
from jax import lax
from jax.experimental import pallas as pl
from jax.experimental.pallas import tpu as pltpu

F32 = jnp.float32
BF16 = jnp.bfloat16

HEAD_DIM = 64
N_HEADS = 8
N_KV_HEADS = 2
GQA_REP = N_HEADS // N_KV_HEADS
ATT_WIDTH = N_HEADS * HEAD_DIM
KV_W = N_KV_HEADS * HEAD_DIM
ROT_DIM = HEAD_DIM // 4
ROPE_THETA = 500000.0
CMP_BLOCK = 32
CMP_STRIDE = 16
CMP_HIDDEN = 256
SEL_BLOCK = 64
SEL_TOP_K = 16
WINDOW = 512
Q_BLOCK = 128
CONV_K = 3
POOL_WINDOWS = (2, 4, 8, 16)
POOL_STATE = max(POOL_WINDOWS) - 1
CARRY_STEPS = 16
N_EXPERT_GROUPS = 4
EXPERTS_PER_GROUP = 8
N_EXPERTS = N_EXPERT_GROUPS * EXPERTS_PER_GROUP
D_EXPERT = 256
LN_EPS = 1e-5
NEG = -1e30
FORCE_BONUS = 1e4
LOG2_E = 1.4426950408889634
LANES = 128
VMEM_LIMIT = 56 * 1024 * 1024

_C_Q, _C_KV, _C_GL, _C_BG = 0, 512, 1280, 1304
_MAIN_W = 1664


def _cparams(sem):
    return pltpu.CompilerParams(dimension_semantics=sem, vmem_limit_bytes=VMEM_LIMIT)


def _dot(a, b):
    return jnp.dot(a, b, preferred_element_type=F32)


def _dot_nt(a, b):
    return lax.dot_general(a, b, (((1,), (1,)), ((), ())), preferred_element_type=F32)


def _layer_norm(y, g, b):
    mu = jnp.mean(y, axis=-1, keepdims=True)
    d = y - mu
    var = jnp.mean(d * d, axis=-1, keepdims=True)
    return d * lax.rsqrt(var + LN_EPS) * g + b


def _rope_rows(v, tab):
    return (v * tab[:, 0:128] + pltpu.roll(v, 120, 1) * tab[:, 128:256]
            + pltpu.roll(v, 8, 1) * tab[:, 256:384])


def _rope_cols(v, tab):
    return (v * tab[0:128] + pltpu.roll(v, 120, 0) * tab[128:256]
            + pltpu.roll(v, 8, 0) * tab[256:384])


def _proj_kernel(x_ref, wm_ref, wk_ref, tabr_ref, tabt_ref, cw_ref, pw_ref, ps_ref, cst_ref, pst_ref, *rest,
                 tm, rs, pos0, rows_rope):
    qq_ref, gate_ref, ocp_ref, rows_ref, kvt_ref, wint_ref, kvtb_ref, cnew_ref, pnew_ref, cu_sc, cp_sc = rest[-11:]
    ti = pl.program_id(1)
    carry = CARRY_STEPS * rs

    @pl.when(ti == 0)
    def _():
        cu_sc[...] = cst_ref[0]
        cp_sc[...] = pst_ref[0]

    xb = x_ref[0].astype(BF16)
    h = _dot(xb, wm_ref[...])
    kvt = _dot_nt(wk_ref[...], xb)
    tabr = tabr_ref[...]
    tabt = tabt_ref[...]

    scale = HEAD_DIM ** -0.5 * LOG2_E
    for c in range(4):
        qc = h[:, c * 128:(c + 1) * 128]
        qq_ref[0, :, c * 128:(c + 1) * 128] = (qc * scale).astype(BF16)
        qq_ref[0, :, 512 + c * 128:512 + (c + 1) * 128] = (_rope_rows(qc, tabr) * scale).astype(BF16)
    gate_ref[0] = 1.0 / (1.0 + jnp.exp(-h[:, 1536:1664]))

    for c in range(6):
        blk = kvt[c * 128:(c + 1) * 128]
        if c in (2, 4):
            blk = _rope_cols(blk, tabt)
        if c < 4:
            kvt_ref[0, c * 128:(c + 1) * 128, :] = blk
        else:
            wint_ref[0, (c - 4) * 128:(c - 3) * 128, :] = blk
        kvtb_ref[0, c * 128:(c + 1) * 128, :] = blk.astype(BF16)

    n_rows = rows_ref.shape[2]
    for c in range(n_rows // 128):
        blk = h[:, _MAIN_W + c * 128:_MAIN_W + (c + 1) * 128]
        if c in rows_rope:
            blk = _rope_rows(blk, tabr)
        rows_ref[0, :, c * 128:(c + 1) * 128] = blk.astype(rows_ref.dtype)

    bg = h[:, 512:768]
    cg = h[:, 768:1024]
    vc = h[:, 1024:1280]
    pin = h[:, 1280:1536]

    u = cg * vc
    eu = jnp.concatenate([cu_sc[...], u], axis=0)
    cw = cw_ref[...]
    y = (eu[carry - 2 * rs:carry - 2 * rs + tm] * cw[0:1] + eu[carry - rs:carry - rs + tm] * cw[1:2]
         + u * cw[2:3])
    o_conv = bg * y
    new_u = eu[tm:tm + carry]
    cu_sc[...] = new_u
    cnew_ref[0] = new_u

    ep = jnp.concatenate([cp_sc[...], pin], axis=0)
    s2 = ep[rs:] + ep[:-rs]
    s4 = s2[2 * rs:] + s2[:-2 * rs]
    s8 = s4[4 * rs:] + s4[:-4 * rs]
    s16 = s8[8 * rs:] + s8[:-8 * rs]
    take = lambda a: a[a.shape[0] - tm:]
    lane = lax.broadcasted_iota(jnp.int32, (tm, 256), 1)
    row = lax.broadcasted_iota(jnp.int32, (tm, 256), 0)
    dsum = jnp.where(lane < 64, take(s2), jnp.where(lane < 128, take(s4), jnp.where(lane < 192, take(s8), take(s16))))
    wlen = jnp.where(lane < 64, 2, jnp.where(lane < 128, 4, jnp.where(lane < 192, 8, 16)))
    step = pos0 + (ti * tm + row) // rs
    cnt = jnp.minimum(wlen, step + 1).astype(F32)
    d = dsum / cnt - pin
    o_pool = _dot(d.astype(BF16), pw_ref[...]) * ps_ref[...]
    new_p = ep[tm:tm + carry]
    cp_sc[...] = new_p
    pnew_ref[0] = new_p

    ocp_ref[0, :, 0:256] = o_conv.astype(BF16)
    ocp_ref[0, :, 256:512] = o_pool.astype(BF16)


def _proj_call(x3, wm, wk, tabr, tabt, cw, pw, ps, cst, pst, *, l, tm, rs, pos0, rows_dtype, rows_rope,
               kv_slabs=1, kvt_prev=None):
    n_sg, n_rows, d_model = x3.shape
    kv_slab = l if kv_slabs > 1 else 0
    tiles = n_rows // tm
    n_main = wm.shape[2]
    n_rowcols = n_main - _MAIN_W
    carry = CARRY_STEPS * rs
    kern = functools.partial(_proj_kernel, tm=tm, rs=rs, pos0=pos0, rows_rope=rows_rope)
    out_shape = (
        jax.ShapeDtypeStruct((n_sg, n_rows, 1024), BF16),
        jax.ShapeDtypeStruct((n_sg, n_rows, 128), F32),
        jax.ShapeDtypeStruct((n_sg, n_rows, 512), BF16),
        jax.ShapeDtypeStruct((n_sg, n_rows, n_rowcols), rows_dtype),
        jax.ShapeDtypeStruct((kv_slabs * n_sg, 512, n_rows), F32),
        jax.ShapeDtypeStruct((n_sg, 256, n_rows), F32),
        jax.ShapeDtypeStruct((n_sg, 768, n_rows), BF16),
        jax.ShapeDtypeStruct((n_sg, carry, 256), F32),
        jax.ShapeDtypeStruct((n_sg, carry, 256), F32),
    )
    row_blk = lambda w: pl.BlockSpec((1, tm, w), lambda s, t: (s, t, 0))
    col_blk = lambda w: pl.BlockSpec((1, w, tm), lambda s, t: (s, 0, t))
    st_blk = pl.BlockSpec((1, carry, 256), lambda s, t: (s, 0, 0))
    in_specs = [
        row_blk(d_model),
        pl.BlockSpec((None, d_model, n_main), lambda s, t: (l, 0, 0)),
        pl.BlockSpec((None, 768, d_model), lambda s, t: (l, 0, 0)),
        pl.BlockSpec((tm, 384), lambda s, t: (t, 0)),
        pl.BlockSpec((384, tm), lambda s, t: (0, t)),
        pl.BlockSpec((None, 8, 256), lambda s, t: (l, 0, 0)),
        pl.BlockSpec((None, 256, 256), lambda s, t: (l, 0, 0)),
        pl.BlockSpec((None, 1, 256), lambda s, t: (l, 0, 0)),
        st_blk, st_blk,
    ]
    operands = [x3, wm, wk, tabr, tabt, cw, pw, ps, cst, pst]
    aliases = {}
    if kvt_prev is not None:
        aliases = {len(operands): 4}
        in_specs.append(pl.BlockSpec(memory_space=pl.ANY))
        operands.append(kvt_prev)
    return pl.pallas_call(
        kern,
        grid=(n_sg, tiles),
        in_specs=in_specs,
        out_specs=(row_blk(1024), row_blk(128), row_blk(512), row_blk(n_rowcols),
                   pl.BlockSpec((1, 512, tm), lambda s, t: (kv_slab * n_sg + s, 0, t)),
                   col_blk(256), col_blk(768), st_blk, st_blk),
        out_shape=out_shape,
        scratch_shapes=[pltpu.VMEM((carry, 256), F32), pltpu.VMEM((carry, 256), F32)],
        input_output_aliases=aliases,
        compiler_params=_cparams(("arbitrary", "arbitrary")),
        name="proj",
    )(*operands)


def _gelu_tanh(x):
    return 0.5 * x * (1.0 + jnp.tanh(0.7978845608028654 * (x + 0.044715 * x * x * x)))


def _compress_core(read_rows, n, pe_ref, wa_ref, wb_ref, w2_ref, kc_ref, vc_ref):
    for kv, out_ref in ((0, kc_ref), (1, vc_ref)):
        cols = [read_rows(kv, r) for r in range(CMP_STRIDE)]
        xa = jnp.concatenate([cols[r] + pe_ref[kv, 0, r:r + 1, :] for r in range(CMP_STRIDE)], axis=1)
        xb = jnp.concatenate([cols[r] + pe_ref[kv, 1, r:r + 1, :] for r in range(CMP_STRIDE)], axis=1)
        a = _dot(xa.astype(BF16), wa_ref[kv])
        b = _dot(xb.astype(BF16), wb_ref[kv])
        pre = a + pltpu.roll(b, n - 1, 0)
        hid = _gelu_tanh(pre)
        out_ref[0] = _dot(hid.astype(BF16), w2_ref[kv]).astype(BF16)


def _compress_rows_kernel(k_ref, v_ref, pe_ref, wa_ref, wb_ref, w2_ref, kc_ref, vc_ref, *, n):
    srcs = (k_ref, v_ref)
    read = lambda kv, r: srcs[kv][0, pl.ds(r, n, stride=CMP_STRIDE), :]
    _compress_core(read, n, pe_ref, wa_ref, wb_ref, w2_ref, kc_ref, vc_ref)


def _compress_pages_kernel(pt_ref, *refs, n_pages_step, page, n):
    pages = refs[:n_pages_step]
    pe_ref, wa_ref, wb_ref, w2_ref, kc_ref, vc_ref, kbuf, vbuf = refs[n_pages_step:]
    for i, pg in enumerate(pages):
        kbuf[i * page:(i + 1) * page, :] = pg[0:128, :].T
        vbuf[i * page:(i + 1) * page, :] = pg[128:256, :].T
    bufs = (kbuf, vbuf)
    read = lambda kv, r: bufs[kv][pl.ds(r, n, stride=CMP_STRIDE), :]
    _compress_core(read, n, pe_ref, wa_ref, wb_ref, w2_ref, kc_ref, vc_ref)


def _cmp_weight_specs(l, nidx):
    im4 = (lambda *a: (l, 0, 0, 0))
    im5 = (lambda *a: (l, 0, 0, 0, 0))
    return [
        pl.BlockSpec((None, 2, 2, CMP_STRIDE, 128), im5),
        pl.BlockSpec((None, 2, 2048, 512), im4),
        pl.BlockSpec((None, 2, 2048, 512), im4),
        pl.BlockSpec((None, 2, 512, 128), im4),
    ]


def _compress_rows_call(rows, pe, wa, wb, w2, *, l):
    b, t, _ = rows.shape
    n = t // CMP_STRIDE
    out = jax.ShapeDtypeStruct((b, n, 128), BF16)
    return pl.pallas_call(
        functools.partial(_compress_rows_kernel, n=n),
        grid=(b,),
        in_specs=[pl.BlockSpec((1, t, 128), lambda i: (i, 0, 0)),
                  pl.BlockSpec((1, t, 128), lambda i: (i, 0, 1))] + _cmp_weight_specs(l, 1),
        out_specs=(pl.BlockSpec((1, n, 128), lambda i: (i, 0, 0)),) * 2,
        out_shape=(out, out),
        compiler_params=_cparams(("arbitrary",)),
        name="compress_rows",
    )(rows, rows, pe, wa, wb, w2)


def _compress_pages_call(pt_flat, cache_t, pe, wa, wb, w2, *, l, n_seq, n_pages, n_phys, seqs_step):
    page = cache_t.shape[2]
    n_pages_step = seqs_step * n_pages
    n = n_pages_step * page // CMP_STRIDE
    steps = n_seq // seqs_step

    def page_spec(k):
        s, p = divmod(k, n_pages)
        return pl.BlockSpec((None, 256, page),
                            lambda i, pt: (l * n_phys + pt[(i * seqs_step + s) * n_pages + p], 0, 0))

    out = jax.ShapeDtypeStruct((steps, n, 128), BF16)
    return pl.pallas_call(
        functools.partial(_compress_pages_kernel, n_pages_step=n_pages_step, page=page, n=n),
        grid_spec=pltpu.PrefetchScalarGridSpec(
            num_scalar_prefetch=1,
            grid=(steps,),
            in_specs=[page_spec(k) for k in range(n_pages_step)] + _cmp_weight_specs(l, 2),
            out_specs=(pl.BlockSpec((1, n, 128), lambda i, pt: (i, 0, 0)),) * 2,
            scratch_shapes=[pltpu.VMEM((n_pages_step * page, 128), F32)] * 2,
        ),
        out_shape=(out, out),
        compiler_params=_cparams(("arbitrary",)),
        name="compress_pages",
    )(pt_flat, *([cache_t] * n_pages_step), pe, wa, wb, w2)


def _q_rows(q, tq):
    lo = lax.broadcasted_iota(jnp.int32, (tq, 128), 1) < 64
    cols = [q[:, c * 128:(c + 1) * 128] for c in range(4)]
    return jnp.concatenate([jnp.where(lo, c, 0.0) for c in cols] + [jnp.where(lo, 0.0, c) for c in cols], axis=0)


def _pair_cols(o, tq):
    lo = lax.broadcasted_iota(jnp.int32, (tq, 128), 1) < 64
    return [jnp.where(lo, o[c * tq:(c + 1) * tq], o[(4 + c) * tq:(5 + c) * tq]) for c in range(4)]


def _row_pos(tq, n, pos_base):
    row = lax.broadcasted_iota(jnp.int32, (8 * tq, n), 0)
    return pos_base + (row & (tq - 1))


def _add_row_bias(s, bias, tq):
    n = s.shape[1]
    return (s.reshape(8, tq, n) + bias[None]).reshape(8 * tq, n)


def _compressed_branch(qc, kc, vc, ov, tq, pos_base):
    nc = kc.shape[0]
    tpos = pos_base + lax.broadcasted_iota(jnp.int32, (tq, nc), 0)
    cend = lax.broadcasted_iota(jnp.int32, (tq, nc), 1) * CMP_STRIDE + (CMP_BLOCK - 1)
    s = _add_row_bias(_dot_nt(qc, kc), jnp.where(cend <= tpos, 0.0, NEG), tq)
    e = jnp.exp2(s - jnp.max(s, axis=-1, keepdims=True))
    e = jnp.where(_row_pos(tq, 1, pos_base) >= CMP_BLOCK - 1, e, 0.0)
    den = jnp.sum(e, axis=-1, keepdims=True)
    p = (e / jnp.where(den > 0.0, den, 1.0)).astype(BF16)
    o = _dot(p, vc)
    imp8 = _dot(p, ov)
    imp = jnp.concatenate([imp8[(4 * g) * tq:(4 * g + 1) * tq] + imp8[(4 * g + 1) * tq:(4 * g + 2) * tq]
                           + imp8[(4 * g + 2) * tq:(4 * g + 3) * tq] + imp8[(4 * g + 3) * tq:(4 * g + 4) * tq]
                           for g in range(2)], axis=0)
    return o, imp


def _select_blocks(imp, tq, pos_base, n_sel):
    row = lax.broadcasted_iota(jnp.int32, (2 * tq, 128), 0)
    blk = lax.broadcasted_iota(jnp.int32, (2 * tq, 128), 1)
    pos = pos_base + (row & (tq - 1))
    cur = pos // SEL_BLOCK
    forced = (blk == 0) | (blk == cur) | (blk == cur - 1)
    valid = blk * SEL_BLOCK <= pos
    score = jnp.where(valid, imp + jnp.where(forced, FORCE_BONUS, 0.0), NEG)

    if 2 * tq >= 128:
        ns8 = -(-n_sel // 8) * 8
        st = score.T[0:ns8]
        sub = lax.broadcasted_iota(jnp.int32, st.shape, 0)
        rank = jnp.zeros(st.shape, F32)
        for b in range(n_sel):
            other = st[b:b + 1, :]
            rank = rank + jnp.where((other > st) | ((other == st) & (sub > b)), 1.0, 0.0)
        keep = jnp.where((rank < SEL_TOP_K) & (sub < n_sel), 1.0, 0.0)
        keep = jnp.concatenate([keep, jnp.zeros((128 - ns8, 2 * tq), F32)], axis=0)
        return keep.T
    rank = jnp.zeros(score.shape, F32)
    for b in range(n_sel):
        other = score[:, b:b + 1]
        rank = rank + jnp.where((other > score) | ((other == score) & (blk > b)), 1.0, 0.0)
    return jnp.where((rank < SEL_TOP_K) & (blk < n_sel), 1.0, 0.0)


def _softmax_pv(pieces):
    m = None
    for s, _, _ in pieces:
        mi = jnp.max(s, axis=-1, keepdims=True)
        m = mi if m is None else jnp.maximum(m, mi)
    den = 0.0
    acc = 0.0
    for s, v, fm in pieces:
        p = jnp.exp2(s - m)
        den = den + jnp.sum(p, axis=-1, keepdims=True)
        pb = p.astype(BF16)
        acc = acc + (_dot_nt(pb, v) if fm else _dot(pb, v))
    return acc / den


def _gated_sum(branches, gates, tq):
    lo = lax.broadcasted_iota(jnp.int32, (tq, 128), 1) < 64
    cols = [_pair_cols(o, tq) for o in branches]
    out = []
    for c in range(4):
        acc = 0.0
        for n in range(3):
            ga = gates[:, 3 * c + n:3 * c + n + 1]
            gb = gates[:, 3 * (c + 4) + n:3 * (c + 4) + n + 1]
            acc = acc + cols[n][c] * jnp.where(lo, ga, gb)
        out.append(acc)
    return out


def _attn_prompt_kernel(qq_ref, gate_ref, kc_ref, vc_ref, kst_ref, vst_ref, kwt_ref, vwt_ref, et_ref, ov_ref,
                        o_ref, kaug_sc, m_sc, l_sc, acc_sc, *, tq, t_len, tk):
    j = pl.program_id(1)
    rows = 8 * tq

    @pl.when(j == 0)
    def _():
        kaug_sc[0:128, :] = kst_ref[0]
        kaug_sc[128:256, :] = et_ref[...]

    pos_base = j * tq
    qq = qq_ref[0].astype(F32)
    qc = _q_rows(qq[:, 0:512], tq).astype(BF16)
    qs = _q_rows(qq[:, 512:1024], tq).astype(BF16)

    o_cmp, imp = _compressed_branch(qc, kc_ref[0], vc_ref[0], ov_ref[...], tq, pos_base)
    sel = _select_blocks(imp, tq, pos_base, -(-t_len // SEL_BLOCK))
    bias = jnp.where(sel > 0.5, 0.0, NEG).astype(BF16)
    bias_rows = jnp.concatenate([bias[0:tq]] * 4 + [bias[tq:2 * tq]] * 4, axis=0)
    q_aug = jnp.concatenate([qs, bias_rows], axis=1)

    m_sc[...] = jnp.full((rows, 128), NEG, F32)
    l_sc[...] = jnp.zeros((rows, 128), F32)
    acc_sc[...] = jnp.zeros((rows, 128), F32)

    def update(k0, causal):
        s = _dot(q_aug, kaug_sc[:, pl.ds(k0, tk)])
        if causal:
            kpos = k0 + lax.broadcasted_iota(jnp.int32, (tq, tk), 1)
            qpos = pos_base + lax.broadcasted_iota(jnp.int32, (tq, tk), 0)
            s = _add_row_bias(s, jnp.where(kpos <= qpos, 0.0, NEG), tq)
        m_old = m_sc[...]
        m_new = jnp.maximum(m_old, jnp.max(s, axis=-1, keepdims=True))
        alpha = jnp.exp2(m_old - m_new)
        p = jnp.exp2(s - jnp.concatenate([m_new] * (tk // 128), axis=1))
        l_sc[...] = alpha * l_sc[...] + jnp.sum(p, axis=-1, keepdims=True)
        acc_sc[...] = alpha * acc_sc[...] + _dot_nt(p.astype(BF16), vst_ref[0, :, pl.ds(k0, tk)])
        m_sc[...] = m_new

    n_bulk = (j * tq) // tk

    def bulk(kt, c):
        update(pl.multiple_of(kt * tk, tk), False)
        return c

    lax.fori_loop(0, n_bulk, bulk, 0)
    update(pl.multiple_of(n_bulk * tk, tk), True)
    o_slc = acc_sc[...] / l_sc[...]

    wk = WINDOW + tq
    k0 = pl.multiple_of(jnp.maximum(j * tq - WINDOW, 0), 128)
    dist = (pos_base + lax.broadcasted_iota(jnp.int32, (tq, wk), 0)) - (k0 + lax.broadcasted_iota(jnp.int32, (tq, wk), 1))
    in_win = jnp.where((dist >= 0) & (dist < WINDOW), 0.0, NEG)
    s = _add_row_bias(_dot(qs, kwt_ref[0, :, pl.ds(k0, wk)]), in_win, tq)
    o_win = _softmax_pv([(s, vwt_ref[0, :, pl.ds(k0, wk)], True)])

    cols = _gated_sum([o_cmp, o_slc, o_win], gate_ref[0], tq)
    for c in range(4):
        o_ref[0, :, c * 128:(c + 1) * 128] = cols[c].astype(BF16)


def _attn_prompt_call(qq, gates, kc, vc, kvtb, et, ov):
    b, t_len, _ = qq.shape
    tq = Q_BLOCK
    tk = 512
    nc = kc.shape[1]
    kern = functools.partial(_attn_prompt_kernel, tq=tq, t_len=t_len, tk=tk)
    kv_blk = lambda c: pl.BlockSpec((1, 128, t_len), lambda i, j: (i, c, 0))
    return pl.pallas_call(
        kern,
        grid=(b, t_len // tq),
        in_specs=[
            pl.BlockSpec((1, tq, 1024), lambda i, j: (i, j, 0)),
            pl.BlockSpec((1, tq, 128), lambda i, j: (i, j, 0)),
            pl.BlockSpec((1, nc, 128), lambda i, j: (i, 0, 0)),
            pl.BlockSpec((1, nc, 128), lambda i, j: (i, 0, 0)),
            kv_blk(2), kv_blk(3), kv_blk(4), kv_blk(5),
            pl.BlockSpec((128, t_len), lambda i, j: (0, 0)),
            pl.BlockSpec((nc, 128), lambda i, j: (0, 0)),
        ],
        out_specs=pl.BlockSpec((1, tq, 512), lambda i, j: (i, j, 0)),
        out_shape=jax.ShapeDtypeStruct((b, t_len, 512), BF16),
        scratch_shapes=[pltpu.VMEM((256, t_len), BF16), pltpu.VMEM((8 * tq, 128), F32),
                        pltpu.VMEM((8 * tq, 128), F32), pltpu.VMEM((8 * tq, 128), F32)],
        compiler_params=_cparams(("arbitrary", "arbitrary")),
        name="attn_prompt",
    )(qq, gates, kc, vc, kvtb, kvtb, kvtb, kvtb, et, ov)


def _attn_sample_kernel(pt_ref, *refs, n_pages, page, tq, past, win_rows, spb):
    pages = refs[:spb * n_pages]
    (qq_ref, gate_ref, kc_ref, vc_ref, new_ref, cwin_ref, neww_ref, es_ref,
     ov_ref) = refs[spb * n_pages:spb * n_pages + 9]
    o_ref, wout_ref = refs[-2:]
    rows = 8 * tq
    pos_base = past
    for sq in range(spb):
        seq_pages = pages[sq * n_pages:(sq + 1) * n_pages]
        qq = qq_ref[sq].astype(F32)
        qc = _q_rows(qq[:, 0:512], tq).astype(BF16)
        qs = _q_rows(qq[:, 512:1024], tq).astype(BF16)

        o_cmp, imp = _compressed_branch(qc, kc_ref[sq], vc_ref[sq], ov_ref[...], tq, pos_base)
        sel = _select_blocks(imp, tq, pos_base, -(-(past + tq) // SEL_BLOCK))

        new = jnp.concatenate([new_ref[sq].astype(F32), jnp.zeros((128 - tq, 512), F32)], axis=0).astype(BF16)

        kst = jnp.concatenate([pg[0:128, :].astype(BF16) for pg in seq_pages], axis=1)
        vst = jnp.concatenate([pg[128:256, :].astype(BF16) for pg in seq_pages], axis=1)
        n_keys = past + 128
        member = _dot(sel.astype(BF16), es_ref[...])
        member = jnp.concatenate([member[0:tq]] * 4 + [member[tq:2 * tq]] * 4, axis=0)
        kpos = lax.broadcasted_iota(jnp.int32, (rows, n_keys), 1)
        ok = (member > 0.5) & (kpos <= _row_pos(tq, n_keys, pos_base))
        s_past = jnp.where(ok[:, 0:past], _dot(qs, kst), NEG)
        s_new = jnp.where(ok[:, past:], _dot_nt(qs, new[:, 0:128]), NEG)
        o_slc = _softmax_pv([(s_past, vst, True), (s_new, new[:, 128:256], False)])

        cw = cwin_ref[sq]
        n_wk = win_rows + 128
        kpos_w = (past - win_rows) + lax.broadcasted_iota(jnp.int32, (rows, n_wk), 1)
        dist = _row_pos(tq, n_wk, pos_base) - kpos_w
        okw = (dist >= 0) & (dist < WINDOW)
        s_old = jnp.where(okw[:, 0:win_rows], _dot(qs, cw[0:128].astype(BF16)), NEG)
        s_nw = jnp.where(okw[:, win_rows:], _dot_nt(qs, new[:, 256:384]), NEG)
        o_win = _softmax_pv([(s_old, cw[128:256].astype(BF16), True), (s_nw, new[:, 384:512], False)])

        cols = _gated_sum([o_cmp, o_slc, o_win], gate_ref[sq], tq)
        for c in range(4):
            o_ref[sq, :, c * 128:(c + 1) * 128] = cols[c].astype(BF16)

        wout_ref[sq] = pltpu.roll(cw, win_rows - tq, 1)
        wout_ref[sq, :, win_rows - tq:] = neww_ref[sq]


def _attn_sample_call(pt_flat, cache_t, qq, gates, kc, vc, new_rows, cwin_t, new_win_t, es, ov, win_prev, *,
                      l, depth, n_pages, n_phys, past):
    n_seq, tq, _ = qq.shape
    page = cache_t.shape[2]
    win_rows = cwin_t.shape[2]
    nc = kc.shape[1]
    spb = 2 if n_seq % 2 == 0 else 1
    steps = n_seq // spb
    kern = functools.partial(_attn_sample_kernel, n_pages=n_pages, page=page, tq=tq, past=past, win_rows=win_rows,
                             spb=spb)

    def page_spec(k):
        sq, p = divmod(k, n_pages)
        return pl.BlockSpec((None, 256, page), lambda i, pt: (l * n_phys + pt[(i * spb + sq) * n_pages + p], 1, 0))

    seq_blk = lambda r, w: pl.BlockSpec((spb, r, w), lambda i, pt: (i, 0, 0))
    win_blk = pl.BlockSpec((spb, 256, win_rows), lambda i, pt: (l * steps + i, 0, 0))
    in_specs = [page_spec(k) for k in range(spb * n_pages)] + [
        seq_blk(tq, 1024), seq_blk(tq, 128), seq_blk(nc, 128), seq_blk(nc, 128), seq_blk(tq, 512),
        win_blk, seq_blk(256, tq),
        pl.BlockSpec((128, past + 128), lambda i, pt: (0, 0)),
        pl.BlockSpec((nc, 128), lambda i, pt: (0, 0)),
    ]
    operands = [pt_flat] + [cache_t] * (spb * n_pages) + [qq, gates, kc, vc, new_rows, cwin_t, new_win_t, es, ov]
    aliases = {}
    if win_prev is not None:
        in_specs.append(pl.BlockSpec(memory_space=pl.ANY))
        aliases = {len(operands): 1}
        operands.append(win_prev)
    return pl.pallas_call(
        kern,
        grid_spec=pltpu.PrefetchScalarGridSpec(
            num_scalar_prefetch=1,
            grid=(steps,),
            in_specs=in_specs,
            out_specs=(seq_blk(tq, 512), win_blk),
        ),
        out_shape=(jax.ShapeDtypeStruct((n_seq, tq, 512), BF16),
                   jax.ShapeDtypeStruct((depth * n_seq, 256, win_rows), F32)),
        input_output_aliases=aliases,
        compiler_params=_cparams(("arbitrary",)),
        name="attn_sample",
    )(*operands)


def _mix_kernel(oa_ref, ocp_ref, x_ref, wo_ref, g_ref, b_ref, wr_ref, br_ref, *rest, alpha):
    x1_ref, route_ref, cnt_ref = rest[-3:]
    mix = _dot(oa_ref[...], wo_ref[0:512, :]) + _dot(ocp_ref[...], wo_ref[512:1024, :])
    x1 = _layer_norm(alpha * x_ref[...] + mix, g_ref[...], b_ref[...])
    x1_ref[...] = x1

    tm = x1.shape[0]
    xh = x1.astype(BF16)
    xl = (x1 - xh.astype(F32)).astype(BF16)
    hw = _dot(xh, wr_ref[...])
    logits = hw[:, 0:128] + hw[:, 128:256] + _dot(xl, wr_ref[:, 0:128]) + br_ref[...]
    lane = lax.broadcasted_iota(jnp.int32, (tm, 128), 1).astype(F32)
    big = 1e9
    is_g = (lane >= N_EXPERTS) & (lane < N_EXPERTS + N_EXPERT_GROUPS)
    lg = jnp.where(is_g, logits, NEG)
    ge = jnp.where(is_g, jnp.exp(lg - jnp.max(lg, axis=-1, keepdims=True)), 0.0)
    gp = ge / jnp.sum(ge, axis=-1, keepdims=True)
    gw = jnp.max(gp, axis=-1, keepdims=True)
    gidx = jnp.min(jnp.where(is_g & (gp == gw), lane - N_EXPERTS, big), axis=-1, keepdims=True)
    in_g = (lane >= gidx * EXPERTS_PER_GROUP) & (lane < (gidx + 1.0) * EXPERTS_PER_GROUP)
    le = jnp.where(in_g, logits, NEG)
    ee = jnp.where(in_g, jnp.exp(le - jnp.max(le, axis=-1, keepdims=True)), 0.0)
    ep = ee / jnp.sum(ee, axis=-1, keepdims=True)
    w1 = jnp.max(jnp.where(in_g, ep, -1.0), axis=-1, keepdims=True)
    i1 = jnp.min(jnp.where(in_g & (ep == w1), lane, big), axis=-1, keepdims=True)
    rest = in_g & (lane != i1)
    w2 = jnp.max(jnp.where(rest, ep, -1.0), axis=-1, keepdims=True)
    i2 = jnp.min(jnp.where(rest & (ep == w2), lane, big), axis=-1, keepdims=True)
    den = w1 + w2
    route_ref[...] = jnp.where(lane == 0.0, i1, jnp.where(lane == 1.0, i2, jnp.where(
        lane == 2.0, gw * (w1 / den), jnp.where(lane == 3.0, gw * (w2 / den), 0.0))))
    pairs = jnp.sum(jnp.where((lane == i1) | (lane == i2), 1.0, 0.0), axis=0, keepdims=True)
    cnt_ref[0] = jnp.broadcast_to(pairs, (8, 128))


def _mix_call(oa, ocp, x, wo, g, b, wr, br, *, l, alpha, tm, tile0, n_all, prev):
    n, d = x.shape
    row = lambda w: pl.BlockSpec((tm, w), lambda i: (i, 0))
    out_row = lambda w: pl.BlockSpec((tm, w), lambda i: (i + tile0, 0))
    vec = lambda w: pl.BlockSpec((None, 1, w), lambda i: (l, 0, 0))
    in_specs = [row(512), row(512), row(d),
                pl.BlockSpec((None, 1024, d), lambda i: (l, 0, 0)), vec(d), vec(d),
                pl.BlockSpec((None, d, 256), lambda i: (l, 0, 0)), vec(128)]
    operands = [oa, ocp, x, wo, g, b, wr, br]
    aliases = {}
    if prev is not None:
        aliases = {len(operands) + k: k for k in range(3)}
        in_specs += [pl.BlockSpec(memory_space=pl.ANY)] * 3
        operands += list(prev)
    return pl.pallas_call(
        functools.partial(_mix_kernel, alpha=alpha),
        grid=(n // tm,),
        in_specs=in_specs,
        out_specs=(out_row(d), out_row(128), pl.BlockSpec((1, 8, 128), lambda i: (i + tile0, 0, 0))),
        out_shape=(jax.ShapeDtypeStruct((n_all, d), F32), jax.ShapeDtypeStruct((n_all, 128), F32),
                   jax.ShapeDtypeStruct((n_all // tm, 8, 128), F32)),
        input_output_aliases=aliases,
        compiler_params=_cparams(("arbitrary",)),
        name="mix_router",
    )(*operands)


MOE_CHUNK = 16
MOE_TT = 512
MOE_TE = 256
MOE_SLOTS = 2 * MOE_TT + N_EXPERTS * MOE_CHUNK
MOE_XW = 1024 + LANES


def _chunk_copy(src, s0, dst, d0, sem):
    return pltpu.make_async_copy(src.at[pl.ds(s0, MOE_CHUNK), :], dst.at[pl.ds(d0, MOE_CHUNK), :], sem)


def _for_each_chunk(i, lo_s, dst_s, nch_s, fn):
    def per_expert(e, total):
        k = i * N_EXPERTS + e
        n = nch_s[k]

        def per_chunk(c, carry):
            fn(pl.multiple_of(lo_s[k] + c * MOE_CHUNK, MOE_CHUNK), pl.multiple_of(dst_s[k] + c * MOE_CHUNK, MOE_CHUNK))
            return carry

        lax.fori_loop(0, n, per_chunk, 0)
        return total + n

    return lax.fori_loop(0, N_EXPERTS, per_expert, 0)


def _split3(c):
    h = c.astype(BF16).astype(F32)
    m = (c - h).astype(BF16).astype(F32)
    r = (c - h - m).astype(BF16).astype(F32)
    lane = lax.broadcasted_iota(jnp.int32, (c.shape[0], 128), 1)
    return jnp.where(lane == 0, h, jnp.where(lane == 1, m, jnp.where(lane == 2, r, 0.0)))


def _tile_chunks(i, nch_s):
    return lax.fori_loop(0, N_EXPERTS, lambda e, total: total + nch_s[i * N_EXPERTS + e], 0)


def _wait_chunks(n, src, dst, sem):
    def body(c, carry):
        _chunk_copy(src, 0, dst, 0, sem).wait()
        return carry
    lax.fori_loop(0, n, body, 0)


def _dispatch_kernel(lo_s, dst_s, nch_s, fst_s, fn_s, x1_ref, route_ref, lov_ref, xs_hbm, xs_sc, z_sc, sem):
    i = pl.program_id(0)
    last = pl.num_programs(0) - 1
    tt = MOE_TT
    cur = i % 2
    buf = xs_sc.at[cur]

    @pl.when(i == 0)
    def _():
        z_sc[...] = jnp.zeros_like(z_sc)

        def per_expert(e, total):
            def per_chunk(c, carry):
                _chunk_copy(z_sc, 0, xs_hbm, pl.multiple_of(fst_s[e] + c * MOE_CHUNK, MOE_CHUNK), sem.at[1]).start()
                return carry
            lax.fori_loop(0, fn_s[e], per_chunk, 0)
            return total + fn_s[e]

        _wait_chunks(lax.fori_loop(0, N_EXPERTS, per_expert, 0), z_sc, xs_hbm, sem.at[1])

    @pl.when(i >= 2)
    def _():
        _wait_chunks(_tile_chunks(i - 2, nch_s), buf, xs_hbm, sem.at[cur])

    route = route_ref[...]
    rt = route.T
    eio = lax.broadcasted_iota(jnp.int32, (N_EXPERTS, tt), 0).astype(F32)
    m1 = eio == rt[0:1]
    m2 = eio == rt[1:2]
    before = (lax.broadcasted_iota(jnp.int32, (tt, tt), 0) < lax.broadcasted_iota(jnp.int32, (tt, tt), 1))
    rank = _dot(jnp.where(m1 | m2, 1.0, 0.0).astype(BF16), jnp.where(before, 1.0, 0.0).astype(BF16))
    slot = jnp.concatenate([lov_ref[0]] * (tt // 128), axis=1) + rank
    s1 = jnp.sum(jnp.where(m1, slot, 0.0), axis=0, keepdims=True)
    s2 = jnp.sum(jnp.where(m2, slot, 0.0), axis=0, keepdims=True)
    sio = lax.broadcasted_iota(jnp.int32, (MOE_SLOTS, tt), 0).astype(F32)
    p1 = jnp.where(sio == s1, 1.0, 0.0).astype(BF16)
    p2 = jnp.where(sio == s2, 1.0, 0.0).astype(BF16)
    buf[:, 0:1024] = _dot(p1 + p2, x1_ref[...].astype(BF16)).astype(BF16)
    cw = _dot(p1, _split3(route[:, 2:3]).astype(BF16)) + _dot(p2, _split3(route[:, 3:4]).astype(BF16))
    buf[:, 1024:MOE_XW] = cw.astype(BF16)

    n_out = _for_each_chunk(i, lo_s, dst_s, nch_s,
                            lambda s0, d0: _chunk_copy(buf, s0, xs_hbm, d0, sem.at[cur]).start())

    @pl.when(i == last)
    def _():
        _wait_chunks(n_out, buf, xs_hbm, sem.at[cur])

        @pl.when(i >= 1)
        def _():
            _wait_chunks(_tile_chunks(i - 1, nch_s), xs_sc.at[1 - cur], xs_hbm, sem.at[1 - cur])


def _experts_kernel(te_s, blk_s, nact_s, xs_ref, wg_ref, wu_ref, wd_ref, y_ref, wgu_sc, wd_sc):
    k = pl.program_id(0)
    active = k < nact_s[0]
    fresh = (k == 0) | (te_s[k] != te_s[jnp.maximum(k - 1, 0)])

    @pl.when(active & fresh)
    def _():
        wgu_sc[:, 0:D_EXPERT] = wg_ref[...].astype(BF16)
        wgu_sc[:, D_EXPERT:2 * D_EXPERT] = wu_ref[...].astype(BF16)
        wd_sc[...] = wd_ref[...].astype(BF16)

    @pl.when(active)
    def _():
        cw = xs_ref[:, 1024:MOE_XW].astype(F32)
        c = cw[:, 0:1] + cw[:, 1:2] + cw[:, 2:3]
        hgu = _dot(xs_ref[:, 0:1024], wgu_sc[...])
        hg = hgu[:, 0:D_EXPERT]
        hid = hg * (1.0 / (1.0 + jnp.exp(-hg))) * hgu[:, D_EXPERT:2 * D_EXPERT]
        y_ref[...] = _dot((hid * c).astype(BF16), wd_sc[...]).astype(BF16)


def _combine_kernel(lo_s, dst_s, nch_s, x1_ref, route_ref, lor_ref, g_ref, b_ref, ys_hbm, oa_ref, ob_ref, ys_sc,
                    sem, *, alpha, nt_a):
    i = pl.program_id(0)
    tt = MOE_TT
    cur = i % 2

    def fetch(tile, slot):
        return _for_each_chunk(tile, lo_s, dst_s, nch_s,
                               lambda s0, d0: _chunk_copy(ys_hbm, d0, ys_sc.at[slot], s0, sem.at[slot]).start())

    @pl.when(i == 0)
    def _():
        ys_sc[...] = jnp.zeros_like(ys_sc)
        fetch(0, 0)

    @pl.when(i + 1 < pl.num_programs(0))
    def _():
        fetch(i + 1, 1 - cur)

    route = route_ref[...]
    lane = lax.broadcasted_iota(jnp.int32, (tt, 128), 1).astype(F32)
    m1 = lane == route[:, 0:1]
    m2 = lane == route[:, 1:2]
    before = (lax.broadcasted_iota(jnp.int32, (tt, tt), 1) < lax.broadcasted_iota(jnp.int32, (tt, tt), 0))
    rank = _dot(jnp.where(before, 1.0, 0.0).astype(BF16), jnp.where(m1 | m2, 1.0, 0.0).astype(BF16))
    slot = lor_ref[0, 0:1, :] + rank
    s1 = jnp.sum(jnp.where(m1, slot, 0.0), axis=-1, keepdims=True)
    s2 = jnp.sum(jnp.where(m2, slot, 0.0), axis=-1, keepdims=True)
    sio = lax.broadcasted_iota(jnp.int32, (tt, MOE_SLOTS), 1).astype(F32)
    place = jnp.where((sio == s1) | (sio == s2), 1.0, 0.0).astype(BF16)

    _wait_chunks(_tile_chunks(i, nch_s), ys_hbm, ys_sc.at[cur], sem.at[cur])
    y = _dot(place, ys_sc[cur])
    out = _layer_norm(alpha * x1_ref[...] + y, g_ref[...], b_ref[...])

    @pl.when(i < nt_a)
    def _():
        oa_ref[...] = out

    @pl.when(i >= nt_a)
    def _():
        ob_ref[...] = out


def _moe_call(x1, route, cnt, w_gate, w_up, w_down, g, b, *, l, alpha, nt_a):
    n, d = x1.shape
    tt, te, ch, n_e = MOE_TT, MOE_TE, MOE_CHUNK, N_EXPERTS
    nt = n // tt
    n_et = -(-(2 * n + nt * n_e * (ch - 1)) // te) + n_e
    i32 = jnp.int32

    pairs = cnt[:, 0, :n_e].astype(i32)
    pc = (pairs + ch - 1) // ch * ch
    lo = jnp.cumsum(pc, axis=1) - pc
    tot = jnp.sum(pc, axis=0)
    reg = (tot + te - 1) // te * te
    base = jnp.cumsum(reg) - reg
    dst = base[None, :] + jnp.cumsum(pc, axis=0) - pc
    tiles_e = reg // te
    ends = jnp.cumsum(tiles_e)
    n_act = ends[-1]
    k = jnp.arange(n_et, dtype=i32)
    blk = jnp.minimum(k, n_act - 1)
    tile_e = jnp.minimum(jnp.sum((ends[None, :] <= blk[:, None]).astype(i32), axis=1), n_e - 1)
    flat = lambda a: a.reshape(-1).astype(i32)
    lo_s, dst_s, nch_s = flat(lo), flat(dst), flat(pc // ch)
    lov = jnp.broadcast_to(lo.astype(F32)[:, :, None], (nt, n_e, 128))
    lor = jnp.broadcast_to(jnp.pad(lo.astype(F32), ((0, 0), (0, 128 - n_e)))[:, None, :], (nt, 8, 128))

    xs = pl.pallas_call(
        _dispatch_kernel,
        grid_spec=pltpu.PrefetchScalarGridSpec(
            num_scalar_prefetch=5,
            grid=(nt,),
            in_specs=[pl.BlockSpec((tt, d), lambda i, *_: (i, 0)),
                      pl.BlockSpec((tt, 128), lambda i, *_: (i, 0)),
                      pl.BlockSpec((1, n_e, 128), lambda i, *_: (i, 0, 0))],
            out_specs=pl.BlockSpec(memory_space=pl.ANY),
            scratch_shapes=[pltpu.VMEM((2, MOE_SLOTS, MOE_XW), BF16), pltpu.VMEM((ch, MOE_XW), BF16),
                            pltpu.SemaphoreType.DMA((2,))],
        ),
        out_shape=jax.ShapeDtypeStruct((n_et * te, MOE_XW), BF16),
        compiler_params=_cparams(("arbitrary",)),
        name="moe_dispatch",
    )(lo_s, dst_s, nch_s, flat(base + tot), flat((reg - tot) // ch), x1, route, lov)

    wspec = lambda r, c: pl.BlockSpec((None, None, r, c), lambda k, te_s, blk_s, na: (l, te_s[k], 0, 0))
    ys = pl.pallas_call(
        _experts_kernel,
        grid_spec=pltpu.PrefetchScalarGridSpec(
            num_scalar_prefetch=3,
            grid=(n_et,),
            in_specs=[pl.BlockSpec((te, MOE_XW), lambda k, te_s, blk_s, na: (blk_s[k], 0)),
                      wspec(d, D_EXPERT), wspec(d, D_EXPERT), wspec(D_EXPERT, d)],
            out_specs=pl.BlockSpec((te, d), lambda k, te_s, blk_s, na: (blk_s[k], 0)),
            scratch_shapes=[pltpu.VMEM((d, 2 * D_EXPERT), BF16), pltpu.VMEM((D_EXPERT, d), BF16)],
        ),
        out_shape=jax.ShapeDtypeStruct((n_et * te, d), BF16),
        compiler_params=_cparams(("arbitrary",)),
        name="moe_experts",
    )(tile_e, blk, n_act.reshape(1).astype(i32), xs, w_gate, w_up, w_down)

    vec = pl.BlockSpec((None, 1, d), lambda i, *_: (l, 0, 0))
    return pl.pallas_call(
        functools.partial(_combine_kernel, alpha=alpha, nt_a=nt_a),
        grid_spec=pltpu.PrefetchScalarGridSpec(
            num_scalar_prefetch=3,
            grid=(nt,),
            in_specs=[pl.BlockSpec((tt, d), lambda i, *_: (i, 0)),
                      pl.BlockSpec((tt, 128), lambda i, *_: (i, 0)),
                      pl.BlockSpec((1, 8, 128), lambda i, *_: (i, 0, 0)),
                      vec, vec,
                      pl.BlockSpec(memory_space=pl.ANY)],
            out_specs=(pl.BlockSpec((tt, d), lambda i, *_: (jnp.minimum(i, nt_a - 1), 0)),
                       pl.BlockSpec((tt, d), lambda i, *_: (jnp.maximum(i - nt_a, 0), 0))),
            scratch_shapes=[pltpu.VMEM((2, MOE_SLOTS, d), BF16), pltpu.SemaphoreType.DMA((2,))],
        ),
        out_shape=(jax.ShapeDtypeStruct((nt_a * tt, d), F32), jax.ShapeDtypeStruct((n - nt_a * tt, d), F32)),
        compiler_params=_cparams(("arbitrary",)),
        name="moe_combine",
    )(lo_s, dst_s, nch_s, x1, route, lor, g, b, ys)


def _rope_table(pos):
    half = ROT_DIM // 2
    inv = ROPE_THETA ** (-jnp.arange(0, ROT_DIM, 2, dtype=F32) / ROT_DIM)
    ang = jnp.asarray(pos, F32)[:, None] * inv[None, :]
    cos, sin = jnp.cos(ang), jnp.sin(ang)
    n = ang.shape[0]
    z = jnp.zeros((n, HEAD_DIM - ROT_DIM), F32)
    zh = jnp.zeros((n, half), F32)
    c64 = jnp.concatenate([cos, cos, jnp.ones_like(z)], axis=1)
    s1 = jnp.concatenate([-sin, zh, z], axis=1)
    s2 = jnp.concatenate([zh, sin, z], axis=1)
    return jnp.concatenate([c64, c64, s1, s1, s2, s2], axis=1)


def _overlap_table(n_rows):
    c = np.arange(n_rows)[:, None] * CMP_STRIDE
    s = np.arange(128)[None, :] * SEL_BLOCK
    return jnp.asarray(((c < s + SEL_BLOCK) & (c + CMP_BLOCK > s)).astype(np.float32), BF16)


def _block_indicator(n_keys):
    k = np.arange(n_keys)[None, :] // SEL_BLOCK
    return jnp.asarray((np.arange(128)[:, None] == k).astype(np.float32), BF16)


def _block_diag2(w):
    z = jnp.zeros_like(w)
    return jnp.concatenate([jnp.concatenate([w, z], axis=-1), jnp.concatenate([z, w], axis=-1)], axis=-2)


def kernel(x_prompt, x_sample, cache_kv, cache_win, state_conv, state_pool, page_table, ln1_g, ln1_b, w_in, pe_cmp, w_cmp1, w_cmp2, conv_w, pool_w, pool_scale, w_o, ln2_g, ln2_b, w_rg, b_rg, w_re, b_re, w_gate, w_up, w_down):
    n_b, t_len, d_model = x_prompt.shape
    n_seq, t_dec, _ = x_sample.shape
    depth = w_in.shape[0]
    n_phys, page = cache_kv.shape[1], cache_kv.shape[2]
    n_pages = page_table.shape[1]
    past = n_pages * page
    win_rows = cache_win.shape[2]
    alpha = float((2 * depth) ** 0.25)
    assert d_model == 1024 and t_len % 512 == 0 and t_dec == 8 and past % SEL_BLOCK == 0 and win_rows == WINDOW

    pair_order = [h for c in range(4) for h in (c, c + 4)]
    w_main = jnp.concatenate([w_in[:, :, h * 64:(h + 1) * 64] for h in pair_order]
                             + [w_in[:, :, _C_BG:], w_in[:, :, _C_GL:_C_BG],
                              jnp.zeros((depth, d_model, 128 - (_C_BG - _C_GL)), F32)], axis=2)
    wm_p = jnp.concatenate([w_main, w_in[:, :, _C_KV:_C_KV + 256]], axis=2).astype(BF16)
    wm_s = jnp.concatenate([w_main, w_in[:, :, _C_KV + 256:_C_KV + 768]], axis=2).astype(BF16)
    wk_t = jnp.swapaxes(w_in[:, :, _C_KV:_C_GL], 1, 2).astype(BF16)
    cw8 = jnp.concatenate([conv_w, jnp.zeros((depth, 8 - CONV_K, 256), F32)], axis=1)
    pw_bd = jnp.zeros((depth, 256, 256), F32)
    for g in range(4):
        pw_bd = pw_bd.at[:, g * 64:(g + 1) * 64, g * 64:(g + 1) * 64].set(pool_w[:, g])
    pw_bd = pw_bd.astype(BF16)
    ps3 = pool_scale[:, None, :]

    w1r = w_cmp1.reshape(depth, 2, CMP_BLOCK, HEAD_DIM, CMP_HIDDEN)
    bd_half = lambda w: _block_diag2(w).reshape(depth, 2, CMP_STRIDE * 128, 2 * CMP_HIDDEN).astype(BF16)
    wa = bd_half(w1r[:, :, :CMP_STRIDE])
    wb = bd_half(w1r[:, :, CMP_STRIDE:])
    w2 = _block_diag2(w_cmp2).astype(BF16)
    pe2 = jnp.concatenate([pe_cmp, pe_cmp], axis=-1).reshape(depth, 2, 2, CMP_STRIDE, 128)

    wo_perm = jnp.concatenate([w_o[:, h * 64:(h + 1) * 64] for h in pair_order] + [w_o[:, ATT_WIDTH:]],
                              axis=1).astype(BF16)
    wr32 = jnp.concatenate([w_re, w_rg, jnp.zeros((depth, d_model, 128 - N_EXPERTS - N_EXPERT_GROUPS), F32)], axis=2)
    wr_hi = wr32.astype(BF16)
    wr = jnp.concatenate([wr_hi, (wr32 - wr_hi.astype(F32)).astype(BF16)], axis=2)
    br = jnp.concatenate([b_re, b_rg, jnp.zeros((depth, 128 - N_EXPERTS - N_EXPERT_GROUPS), F32)], axis=1)[:, None, :]
    g1, b1, g2, b2 = ln1_g[:, None, :], ln1_b[:, None, :], ln2_g[:, None, :], ln2_b[:, None, :]

    tab_p = _rope_table(np.arange(t_len))
    tab_s = _rope_table(past + np.repeat(np.arange(t_dec), n_seq))
    tab_pt, tab_st = tab_p.T, tab_s.T
    nc_p = t_len // CMP_STRIDE
    nc_s = past // CMP_STRIDE
    ov_p, ov_s = _overlap_table(nc_p), _overlap_table(nc_s)
    et_p = _block_indicator(t_len)
    es_s = _block_indicator(past + 128)

    cache_t = jnp.transpose(cache_kv, (0, 1, 3, 4, 5, 2)).reshape(depth * n_phys, 512, page)
    cwin_t = jnp.transpose(cache_win, (0, 1, 3, 4, 5, 2)).reshape(depth * n_seq, 256, win_rows)
    pt_flat = page_table.reshape(-1)

    def tm_state(st):
        k = st.shape[2]
        st = jnp.swapaxes(st, 1, 2)
        st = jnp.concatenate([jnp.zeros((depth, CARRY_STEPS - k, n_seq, 256), F32), st], axis=1)
        return st.reshape(depth, CARRY_STEPS * n_seq, 256)

    cst_s, pst_s = tm_state(state_conv), tm_state(state_pool)
    zst_p = jnp.zeros((n_b, CARRY_STEPS, 256), F32)

    tm_p = 512
    tm_row = 512
    seqs_step = 4 if n_seq % 4 == 0 else 1

    xp = x_prompt
    xs = x_sample.reshape(n_seq * t_dec, d_model)
    outs = {k: [] for k in ("kv_s", "win_p", "conv_p", "conv_s", "pool_p", "pool_s")}
    win_s_all = None
    kvt_p_all = None
    for l in range(depth):
        qq, gates, ocp, rows, kvt_p_all, wint, kvtb, cnew, pnew = _proj_call(
            xp, wm_p, wk_t, tab_p, tab_pt, cw8, pw_bd, ps3, zst_p, zst_p,
            l=l, tm=tm_p, rs=1, pos0=0, rows_dtype=F32, rows_rope=(), kv_slabs=depth, kvt_prev=kvt_p_all)
        kc, vc = _compress_rows_call(rows, pe2, wa, wb, w2, l=l)
        oa = _attn_prompt_call(qq, gates, kc, vc, kvtb, et_p, ov_p)
        n_p = n_b * t_len
        n_s = n_seq * t_dec
        routed = _mix_call(oa.reshape(n_p, 512), ocp.reshape(n_p, 512), xp.reshape(n_p, d_model),
                           wo_perm, g1, b1, wr, br, l=l, alpha=alpha, tm=MOE_TT, tile0=0, n_all=n_p + n_s,
                           prev=None)
        outs["win_p"].append(jnp.transpose(wint[:, :, t_len - WINDOW:].reshape(n_b, 2, 2, HEAD_DIM, WINDOW),
                                           (0, 4, 1, 2, 3)))
        outs["conv_p"].append(cnew[:, CARRY_STEPS - (CONV_K - 1):])
        outs["pool_p"].append(pnew[:, CARRY_STEPS - POOL_STATE:])

        xs_tm = jnp.swapaxes(xs.reshape(n_seq, t_dec, d_model), 0, 1).reshape(1, t_dec * n_seq, d_model)
        qq, gates, ocp, rows, kvt, wint, kvtb, cnew, pnew = _proj_call(
            xs_tm, wm_s, wk_t, tab_s, tab_st, cw8, pw_bd, ps3, cst_s[l][None], pst_s[l][None],
            l=l, tm=t_dec * n_seq, rs=n_seq, pos0=past, rows_dtype=BF16, rows_rope=(0, 2))
        seq_major = lambda a: jnp.swapaxes(a.reshape(t_dec, n_seq, a.shape[-1]), 0, 1)
        kc, vc = _compress_pages_call(pt_flat, cache_t, pe2, wa, wb, w2, l=l, n_seq=n_seq, n_pages=n_pages,
                                      n_phys=n_phys, seqs_step=seqs_step)
        kc = kc.reshape(n_seq, nc_s, 128)
        vc = vc.reshape(n_seq, nc_s, 128)
        new_win_t = jnp.transpose(wint.reshape(256, t_dec, n_seq), (2, 0, 1))
        oa, win_s_all = _attn_sample_call(pt_flat, cache_t, seq_major(qq), seq_major(gates), kc, vc,
                                          seq_major(rows), cwin_t, new_win_t, es_s, ov_s, win_s_all,
                                          l=l, depth=depth, n_pages=n_pages, n_phys=n_phys, past=past)
        x1, route, cnt = _mix_call(oa.reshape(n_s, 512), seq_major(ocp).reshape(n_s, 512), xs,
                                   wo_perm, g1, b1, wr, br, l=l, alpha=alpha, tm=MOE_TT, tile0=n_p // MOE_TT,
                                   n_all=n_p + n_s, prev=routed)
        xp, xs = _moe_call(x1, route, cnt, w_gate, w_up, w_down, g2, b2, l=l, alpha=alpha, nt_a=n_p // MOE_TT)
        xp = xp.reshape(n_b, t_len, d_model)
        kvt5 = kvt.reshape(4, 2, HEAD_DIM, t_dec, n_seq)
        outs["kv_s"].append(jnp.transpose(kvt5, (4, 3, 0, 1, 2)))
        st_sm = lambda a, k: jnp.swapaxes(a.reshape(CARRY_STEPS, n_seq, 256)[CARRY_STEPS - k:], 0, 1)
        outs["conv_s"].append(st_sm(cnew, CONV_K - 1))
        outs["pool_s"].append(st_sm(pnew, POOL_STATE))

    st = lambda k: jnp.stack(outs[k])
    win_s = jnp.transpose(win_s_all.reshape(depth, n_seq, 2, 2, HEAD_DIM, win_rows), (0, 1, 5, 2, 3, 4))
    kv_p = jnp.transpose(kvt_p_all.reshape(depth, n_b, 4, 2, HEAD_DIM, t_len), (0, 1, 5, 2, 3, 4))
    return (xp, xs.reshape(n_seq, t_dec, d_model), kv_p, st("kv_s"), st("win_p"), win_s,
            st("conv_p"), st("conv_s"), st("pool_p"), st("pool_s"))
```

```python
import functools

import numpy as np
import jax
import jax.numpy as jnp
from jax import lax
from jax.experimental import pallas as pl
from jax.experimental.pallas import tpu as pltpu

F32 = jnp.float32
BF16 = jnp.bfloat16

HEAD_DIM = 64
N_HEADS = 8
N_KV_HEADS = 2
GQA_REP = N_HEADS // N_KV_HEADS
ATT_WIDTH = N_HEADS * HEAD_DIM
KV_W = N_KV_HEADS * HEAD_DIM
ROT_DIM = HEAD_DIM // 4
ROPE_THETA = 500000.0
CMP_BLOCK = 32
CMP_STRIDE = 16
CMP_HIDDEN = 256
SEL_BLOCK = 64
SEL_TOP_K = 16
WINDOW = 512
Q_BLOCK = 128
CONV_K = 3
POOL_WINDOWS = (2, 4, 8, 16)
POOL_STATE = max(POOL_WINDOWS) - 1
CARRY_STEPS = 16
N_EXPERT_GROUPS = 4
EXPERTS_PER_GROUP = 8
N_EXPERTS = N_EXPERT_GROUPS * EXPERTS_PER_GROUP
D_EXPERT = 256
LN_EPS = 1e-5
NEG = -1e30
FORCE_BONUS = 1e4
LOG2_E = 1.4426950408889634
LANES = 128
VMEM_LIMIT = 56 * 1024 * 1024

_C_Q, _C_KV, _C_GL, _C_BG = 0, 512, 1280, 1304
_MAIN_W = 1664


def _cparams(sem):
    return pltpu.CompilerParams(dimension_semantics=sem, vmem_limit_bytes=VMEM_LIMIT)


def _dot(a, b):
    return jnp.dot(a, b, preferred_element_type=F32)


def _dot_nt(a, b):
    return lax.dot_general(a, b, (((1,), (1,)), ((), ())), preferred_element_type=F32)


def _layer_norm(y, g, b):
    mu = jnp.mean(y, axis=-1, keepdims=True)
    d = y - mu
    var = jnp.mean(d * d, axis=-1, keepdims=True)
    return d * lax.rsqrt(var + LN_EPS) * g + b


def _rope_rows(v, tab):
    return (v * tab[:, 0:128] + pltpu.roll(v, 120, 1) * tab[:, 128:256]
            + pltpu.roll(v, 8, 1) * tab[:, 256:384])


def _rope_cols(v, tab):
    return (v * tab[0:128] + pltpu.roll(v, 120, 0) * tab[128:256]
            + pltpu.roll(v, 8, 0) * tab[256:384])


def _proj_kernel(x_ref, wm_ref, wk_ref, tabr_ref, tabt_ref, cw_ref, pw_ref, ps_ref, cst_ref, pst_ref, *rest,
                 tm, rs, pos0, rows_rope):
    qq_ref, gate_ref, ocp_ref, rows_ref, kvt_ref, wint_ref, kvtb_ref, cnew_ref, pnew_ref, cu_sc, cp_sc = rest[-11:]
    ti = pl.program_id(1)
    carry = CARRY_STEPS * rs

    @pl.when(ti == 0)
    def _():
        cu_sc[...] = cst_ref[0]
        cp_sc[...] = pst_ref[0]

    xb = x_ref[0].astype(BF16)
    h = _dot(xb, wm_ref[...])
    kvt = _dot_nt(wk_ref[...], xb)
    tabr = tabr_ref[...]
    tabt = tabt_ref[...]

    scale = HEAD_DIM ** -0.5 * LOG2_E
    for c in range(4):
        qc = h[:, c * 128:(c + 1) * 128]
        qq_ref[0, :, c * 128:(c + 1) * 128] = (qc * scale).astype(BF16)
        qq_ref[0, :, 512 + c * 128:512 + (c + 1) * 128] = (_rope_rows(qc, tabr) * scale).astype(BF16)
    gate_ref[0] = 1.0 / (1.0 + jnp.exp(-h[:, 1536:1664]))

    for c in range(6):
        blk = kvt[c * 128:(c + 1) * 128]
        if c in (2, 4):
            blk = _rope_cols(blk, tabt)
        if c < 4:
            kvt_ref[0, c * 128:(c + 1) * 128, :] = blk
        else:
            wint_ref[0, (c - 4) * 128:(c - 3) * 128, :] = blk
        kvtb_ref[0, c * 128:(c + 1) * 128, :] = blk.astype(BF16)

    n_rows = rows_ref.shape[2]
    for c in range(n_rows // 128):
        blk = h[:, _MAIN_W + c * 128:_MAIN_W + (c + 1) * 128]
        if c in rows_rope:
            blk = _rope_rows(blk, tabr)
        rows_ref[0, :, c * 128:(c + 1) * 128] = blk.astype(rows_ref.dtype)

    bg = h[:, 512:768]
    cg = h[:, 768:1024]
    vc = h[:, 1024:1280]
    pin = h[:, 1280:1536]

    u = cg * vc
    eu = jnp.concatenate([cu_sc[...], u], axis=0)
    cw = cw_ref[...]
    y = (eu[carry - 2 * rs:carry - 2 * rs + tm] * cw[0:1] + eu[carry - rs:carry - rs + tm] * cw[1:2]
         + u * cw[2:3])
    o_conv = bg * y
    new_u = eu[tm:tm + carry]
    cu_sc[...] = new_u
    cnew_ref[0] = new_u

    ep = jnp.concatenate([cp_sc[...], pin], axis=0)
    s2 = ep[rs:] + ep[:-rs]
    s4 = s2[2 * rs:] + s2[:-2 * rs]
    s8 = s4[4 * rs:] + s4[:-4 * rs]
    s16 = s8[8 * rs:] + s8[:-8 * rs]
    take = lambda a: a[a.shape[0] - tm:]
    lane = lax.broadcasted_iota(jnp.int32, (tm, 256), 1)
    row = lax.broadcasted_iota(jnp.int32, (tm, 256), 0)
    dsum = jnp.where(lane < 64, take(s2), jnp.where(lane < 128, take(s4), jnp.where(lane < 192, take(s8), take(s16))))
    wlen = jnp.where(lane < 64, 2, jnp.where(lane < 128, 4, jnp.where(lane < 192, 8, 16)))
    step = pos0 + (ti * tm + row) // rs
    cnt = jnp.minimum(wlen, step + 1).astype(F32)
    d = dsum / cnt - pin
    o_pool = _dot(d.astype(BF16), pw_ref[...]) * ps_ref[...]
    new_p = ep[tm:tm + carry]
    cp_sc[...] = new_p
    pnew_ref[0] = new_p

    ocp_ref[0, :, 0:256] = o_conv.astype(BF16)
    ocp_ref[0, :, 256:512] = o_pool.astype(BF16)


def _proj_call(x3, wm, wk, tabr, tabt, cw, pw, ps, cst, pst, *, l, tm, rs, pos0, rows_dtype, rows_rope,
               kv_slabs=1, kvt_prev=None):
    n_sg, n_rows, d_model = x3.shape
    kv_slab = l if kv_slabs > 1 else 0
    tiles = n_rows // tm
    n_main = wm.shape[2]
    n_rowcols = n_main - _MAIN_W
    carry = CARRY_STEPS * rs
    kern = functools.partial(_proj_kernel, tm=tm, rs=rs, pos0=pos0, rows_rope=rows_rope)
    out_shape = (
        jax.ShapeDtypeStruct((n_sg, n_rows, 1024), BF16),
        jax.ShapeDtypeStruct((n_sg, n_rows, 128), F32),
        jax.ShapeDtypeStruct((n_sg, n_rows, 512), BF16),
        jax.ShapeDtypeStruct((n_sg, n_rows, n_rowcols), rows_dtype),
        jax.ShapeDtypeStruct((kv_slabs * n_sg, 512, n_rows), F32),
        jax.ShapeDtypeStruct((n_sg, 256, n_rows), F32),
        jax.ShapeDtypeStruct((n_sg, 768, n_rows), BF16),
        jax.ShapeDtypeStruct((n_sg, carry, 256), F32),
        jax.ShapeDtypeStruct((n_sg, carry, 256), F32),
    )
    row_blk = lambda w: pl.BlockSpec((1, tm, w), lambda s, t: (s, t, 0))
    col_blk = lambda w: pl.BlockSpec((1, w, tm), lambda s, t: (s, 0, t))
    st_blk = pl.BlockSpec((1, carry, 256), lambda s, t: (s, 0, 0))
    in_specs = [
        row_blk(d_model),
        pl.BlockSpec((None, d_model, n_main), lambda s, t: (l, 0, 0)),
        pl.BlockSpec((None, 768, d_model), lambda s, t: (l, 0, 0)),
        pl.BlockSpec((tm, 384), lambda s, t: (t, 0)),
        pl.BlockSpec((384, tm), lambda s, t: (0, t)),
        pl.BlockSpec((None, 8, 256), lambda s, t: (l, 0, 0)),
        pl.BlockSpec((None, 256, 256), lambda s, t: (l, 0, 0)),
        pl.BlockSpec((None, 1, 256), lambda s, t: (l, 0, 0)),
        st_blk, st_blk,
    ]
    operands = [x3, wm, wk, tabr, tabt, cw, pw, ps, cst, pst]
    aliases = {}
    if kvt_prev is not None:
        aliases = {len(operands): 4}
        in_specs.append(pl.BlockSpec(memory_space=pl.ANY))
        operands.append(kvt_prev)
    return pl.pallas_call(
        kern,
        grid=(n_sg, tiles),
        in_specs=in_specs,
        out_specs=(row_blk(1024), row_blk(128), row_blk(512), row_blk(n_rowcols),
                   pl.BlockSpec((1, 512, tm), lambda s, t: (kv_slab * n_sg + s, 0, t)),
                   col_blk(256), col_blk(768), st_blk, st_blk),
        out_shape=out_shape,
        scratch_shapes=[pltpu.VMEM((carry, 256), F32), pltpu.VMEM((carry, 256), F32)],
        input_output_aliases=aliases,
        compiler_params=_cparams(("arbitrary", "arbitrary")),
        name="proj",
    )(*operands)


def _gelu_tanh(x):
    return 0.5 * x * (1.0 + jnp.tanh(0.7978845608028654 * (x + 0.044715 * x * x * x)))


def _compress_core(read_rows, n, pe_ref, wa_ref, wb_ref, w2_ref, kc_ref, vc_ref):
    for kv, out_ref in ((0, kc_ref), (1, vc_ref)):
        cols = [read_rows(kv, r) for r in range(CMP_STRIDE)]
        xa = jnp.concatenate([cols[r] + pe_ref[kv, 0, r:r + 1, :] for r in range(CMP_STRIDE)], axis=1)
        xb = jnp.concatenate([cols[r] + pe_ref[kv, 1, r:r + 1, :] for r in range(CMP_STRIDE)], axis=1)
        a = _dot(xa.astype(BF16), wa_ref[kv])
        b = _dot(xb.astype(BF16), wb_ref[kv])
        pre = a + pltpu.roll(b, n - 1, 0)
        hid = _gelu_tanh(pre)
        out_ref[0] = _dot(hid.astype(BF16), w2_ref[kv]).astype(BF16)


def _compress_rows_kernel(k_ref, v_ref, pe_ref, wa_ref, wb_ref, w2_ref, kc_ref, vc_ref, *, n):
    srcs = (k_ref, v_ref)
    read = lambda kv, r: srcs[kv][0, pl.ds(r, n, stride=CMP_STRIDE), :]
    _compress_core(read, n, pe_ref, wa_ref, wb_ref, w2_ref, kc_ref, vc_ref)


def _compress_pages_kernel(pt_ref, *refs, n_pages_step, page, n):
    pages = refs[:n_pages_step]
    pe_ref, wa_ref, wb_ref, w2_ref, kc_ref, vc_ref, kbuf, vbuf = refs[n_pages_step:]
    for i, pg in enumerate(pages):
        kbuf[i * page:(i + 1) * page, :] = pg[0:128, :].T
        vbuf[i * page:(i + 1) * page, :] = pg[128:256, :].T
    bufs = (kbuf, vbuf)
    read = lambda kv, r: bufs[kv][pl.ds(r, n, stride=CMP_STRIDE), :]
    _compress_core(read, n, pe_ref, wa_ref, wb_ref, w2_ref, kc_ref, vc_ref)


def _cmp_weight_specs(l, nidx):
    im4 = (lambda *a: (l, 0, 0, 0))
    im5 = (lambda *a: (l, 0, 0, 0, 0))
    return [
        pl.BlockSpec((None, 2, 2, CMP_STRIDE, 128), im5),
        pl.BlockSpec((None, 2, 2048, 512), im4),
        pl.BlockSpec((None, 2, 2048, 512), im4),
        pl.BlockSpec((None, 2, 512, 128), im4),
    ]


def _compress_rows_call(rows, pe, wa, wb, w2, *, l):
    b, t, _ = rows.shape
    n = t // CMP_STRIDE
    out = jax.ShapeDtypeStruct((b, n, 128), BF16)
    return pl.pallas_call(
        functools.partial(_compress_rows_kernel, n=n),
        grid=(b,),
        in_specs=[pl.BlockSpec((1, t, 128), lambda i: (i, 0, 0)),
                  pl.BlockSpec((1, t, 128), lambda i: (i, 0, 1))] + _cmp_weight_specs(l, 1),
        out_specs=(pl.BlockSpec((1, n, 128), lambda i: (i, 0, 0)),) * 2,
        out_shape=(out, out),
        compiler_params=_cparams(("arbitrary",)),
        name="compress_rows",
    )(rows, rows, pe, wa, wb, w2)


def _compress_pages_call(pt_flat, cache_t, pe, wa, wb, w2, *, l, n_seq, n_pages, n_phys, seqs_step):
    page = cache_t.shape[2]
    n_pages_step = seqs_step * n_pages
    n = n_pages_step * page // CMP_STRIDE
    steps = n_seq // seqs_step

    def page_spec(k):
        s, p = divmod(k, n_pages)
        return pl.BlockSpec((None, 256, page),
                            lambda i, pt: (l * n_phys + pt[(i * seqs_step + s) * n_pages + p], 0, 0))

    out = jax.ShapeDtypeStruct((steps, n, 128), BF16)
    return pl.pallas_call(
        functools.partial(_compress_pages_kernel, n_pages_step=n_pages_step, page=page, n=n),
        grid_spec=pltpu.PrefetchScalarGridSpec(
            num_scalar_prefetch=1,
            grid=(steps,),
            in_specs=[page_spec(k) for k in range(n_pages_step)] + _cmp_weight_specs(l, 2),
            out_specs=(pl.BlockSpec((1, n, 128), lambda i, pt: (i, 0, 0)),) * 2,
            scratch_shapes=[pltpu.VMEM((n_pages_step * page, 128), F32)] * 2,
        ),
        out_shape=(out, out),
        compiler_params=_cparams(("arbitrary",)),
        name="compress_pages",
    )(pt_flat, *([cache_t] * n_pages_step), pe, wa, wb, w2)


def _q_rows(q, tq):
    lo = lax.broadcasted_iota(jnp.int32, (tq, 128), 1) < 64
    cols = [q[:, c * 128:(c + 1) * 128] for c in range(4)]
    return jnp.concatenate([jnp.where(lo, c, 0.0) for c in cols] + [jnp.where(lo, 0.0, c) for c in cols], axis=0)


def _pair_cols(o, tq):
    lo = lax.broadcasted_iota(jnp.int32, (tq, 128), 1) < 64
    return [jnp.where(lo, o[c * tq:(c + 1) * tq], o[(4 + c) * tq:(5 + c) * tq]) for c in range(4)]


def _row_pos(tq, n, pos_base):
    row = lax.broadcasted_iota(jnp.int32, (8 * tq, n), 0)
    return pos_base + (row & (tq - 1))


def _add_row_bias(s, bias, tq):
    n = s.shape[1]
    return (s.reshape(8, tq, n) + bias[None]).reshape(8 * tq, n)


def _compressed_branch(qc, kc, vc, ov, tq, pos_base):
    nc = kc.shape[0]
    tpos = pos_base + lax.broadcasted_iota(jnp.int32, (tq, nc), 0)
    cend = lax.broadcasted_iota(jnp.int32, (tq, nc), 1) * CMP_STRIDE + (CMP_BLOCK - 1)
    s = _add_row_bias(_dot_nt(qc, kc), jnp.where(cend <= tpos, 0.0, NEG), tq)
    e = jnp.exp2(s - jnp.max(s, axis=-1, keepdims=True))
    e = jnp.where(_row_pos(tq, 1, pos_base) >= CMP_BLOCK - 1, e, 0.0)
    den = jnp.sum(e, axis=-1, keepdims=True)
    p = (e / jnp.where(den > 0.0, den, 1.0)).astype(BF16)
    o = _dot(p, vc)
    imp8 = _dot(p, ov)
    imp = jnp.concatenate([imp8[(4 * g) * tq:(4 * g + 1) * tq] + imp8[(4 * g + 1) * tq:(4 * g + 2) * tq]
                           + imp8[(4 * g + 2) * tq:(4 * g + 3) * tq] + imp8[(4 * g + 3) * tq:(4 * g + 4) * tq]
                           for g in range(2)], axis=0)
    return o, imp


def _select_blocks(imp, tq, pos_base, n_sel):
    row = lax.broadcasted_iota(jnp.int32, (2 * tq, 128), 0)
    blk = lax.broadcasted_iota(jnp.int32, (2 * tq, 128), 1)
    pos = pos_base + (row & (tq - 1))
    cur = pos // SEL_BLOCK
    forced = (blk == 0) | (blk == cur) | (blk == cur - 1)
    valid = blk * SEL_BLOCK <= pos
    score = jnp.where(valid, imp + jnp.where(forced, FORCE_BONUS, 0.0), NEG)

    if 2 * tq >= 128:
        ns8 = -(-n_sel // 8) * 8
        st = score.T[0:ns8]
        sub = lax.broadcasted_iota(jnp.int32, st.shape, 0)
        slabs = [st[r:r + 8] for r in range(0, ns8, 8)]
        ranks = [jnp.zeros((8, 2 * tq), F32) for _ in slabs]
        sub8 = lax.broadcasted_iota(jnp.int32, (8, 2 * tq), 0)
        for b in range(n_sel):
            other = st[b:b + 1, :]
            for k, slab in enumerate(slabs):
                if b < 8 * k:
                    beats = other >= slab
                elif b >= 8 * k + 8:
                    beats = other > slab
                else:
                    beats = (other > slab) | ((other == slab) & (sub8 + 8 * k > b))
                ranks[k] = ranks[k] + jnp.where(beats, 1.0, 0.0)
        rank = jnp.concatenate(ranks, axis=0)
        keep = jnp.where((rank < SEL_TOP_K) & (sub < n_sel), 1.0, 0.0)
        keep = jnp.concatenate([keep, jnp.zeros((128 - ns8, 2 * tq), F32)], axis=0)
        return keep.T
    rank = jnp.zeros(score.shape, F32)
    for b in range(n_sel):
        other = score[:, b:b + 1]
        rank = rank + jnp.where((other > score) | ((other == score) & (blk > b)), 1.0, 0.0)
    return jnp.where((rank < SEL_TOP_K) & (blk < n_sel), 1.0, 0.0)


def _softmax_pv(pieces):
    m = None
    for s, _, _ in pieces:
        mi = jnp.max(s, axis=-1, keepdims=True)
        m = mi if m is None else jnp.maximum(m, mi)
    den = 0.0
    acc = 0.0
    for s, v, fm in pieces:
        p = jnp.exp2(s - m)
        den = den + jnp.sum(p, axis=-1, keepdims=True)
        pb = p.astype(BF16)
        acc = acc + (_dot_nt(pb, v) if fm else _dot(pb, v))
    return acc / den


def _gated_sum(branches, gates, tq):
    lo = lax.broadcasted_iota(jnp.int32, (tq, 128), 1) < 64
    cols = [_pair_cols(o, tq) for o in branches]
    out = []
    for c in range(4):
        acc = 0.0
        for n in range(3):
            ga = gates[:, 3 * c + n:3 * c + n + 1]
            gb = gates[:, 3 * (c + 4) + n:3 * (c + 4) + n + 1]
            acc = acc + cols[n][c] * jnp.where(lo, ga, gb)
        out.append(acc)
    return out


def _attn_prompt_kernel(qq_ref, gate_ref, kc_ref, vc_ref, kst_ref, vst_ref, kwt_ref, vwt_ref, et_ref, ov_ref,
                        o_ref, kaug_sc, m_sc, l_sc, acc_sc, *, tq, t_len, tk):
    j = pl.program_id(1)
    rows = 8 * tq

    @pl.when(j == 0)
    def _():
        kaug_sc[0:128, :] = kst_ref[0]
        kaug_sc[128:256, :] = et_ref[...]

    pos_base = j * tq
    qq = qq_ref[0].astype(F32)
    qc = _q_rows(qq[:, 0:512], tq).astype(BF16)
    qs = _q_rows(qq[:, 512:1024], tq).astype(BF16)

    o_cmp, imp = _compressed_branch(qc, kc_ref[0], vc_ref[0], ov_ref[...], tq, pos_base)
    sel = _select_blocks(imp, tq, pos_base, -(-t_len // SEL_BLOCK))
    bias = jnp.where(sel > 0.5, 0.0, NEG).astype(BF16)
    bias_rows = jnp.concatenate([bias[0:tq]] * 4 + [bias[tq:2 * tq]] * 4, axis=0)
    q_aug = jnp.concatenate([qs, bias_rows], axis=1)

    m_sc[...] = jnp.full((rows, 128), NEG, F32)
    l_sc[...] = jnp.zeros((rows, 128), F32)
    acc_sc[...] = jnp.zeros((rows, 128), F32)

    def update(k0, causal, width):
        s = _dot(q_aug, kaug_sc[:, pl.ds(k0, width)])
        if causal:
            kpos = k0 + lax.broadcasted_iota(jnp.int32, (tq, width), 1)
            qpos = pos_base + lax.broadcasted_iota(jnp.int32, (tq, width), 0)
            s = _add_row_bias(s, jnp.where(kpos <= qpos, 0.0, NEG), tq)
        m_old = m_sc[...]
        m_new = jnp.maximum(m_old, jnp.max(s, axis=-1, keepdims=True))
        alpha = jnp.exp2(m_old - m_new)
        p = jnp.exp2(s - jnp.concatenate([m_new] * (width // 128), axis=1))
        l_sc[...] = alpha * l_sc[...] + jnp.sum(p, axis=-1, keepdims=True)
        acc_sc[...] = alpha * acc_sc[...] + _dot_nt(p.astype(BF16), vst_ref[0, :, pl.ds(k0, width)])
        m_sc[...] = m_new

    n_bulk = (j * tq) // tk

    def bulk(kt, c):
        update(pl.multiple_of(kt * tk, tk), False, tk)
        return c

    lax.fori_loop(0, n_bulk, bulk, 0)
    k_last = pl.multiple_of(n_bulk * tk, tk)
    short = (j + 1) * tq - n_bulk * tk <= tk // 2

    @pl.when(short)
    def _():
        update(k_last, True, tk // 2)

    @pl.when(jnp.logical_not(short))
    def _():
        update(k_last, True, tk)

    o_slc = acc_sc[...] / l_sc[...]

    wk = WINDOW + tq
    k0 = pl.multiple_of(jnp.maximum(j * tq - WINDOW, 0), 128)
    dist = (pos_base + lax.broadcasted_iota(jnp.int32, (tq, wk), 0)) - (k0 + lax.broadcasted_iota(jnp.int32, (tq, wk), 1))
    in_win = jnp.where((dist >= 0) & (dist < WINDOW), 0.0, NEG)
    s = _add_row_bias(_dot(qs, kwt_ref[0, :, pl.ds(k0, wk)]), in_win, tq)
    o_win = _softmax_pv([(s, vwt_ref[0, :, pl.ds(k0, wk)], True)])

    cols = _gated_sum([o_cmp, o_slc, o_win], gate_ref[0], tq)
    for c in range(4):
        o_ref[0, :, c * 128:(c + 1) * 128] = cols[c].astype(BF16)


def _attn_prompt_call(qq, gates, kc, vc, kvtb, et, ov):
    b, t_len, _ = qq.shape
    tq = Q_BLOCK
    tk = 512
    nc = kc.shape[1]
    kern = functools.partial(_attn_prompt_kernel, tq=tq, t_len=t_len, tk=tk)
    kv_blk = lambda c: pl.BlockSpec((1, 128, t_len), lambda i, j: (i, c, 0))
    return pl.pallas_call(
        kern,
        grid=(b, t_len // tq),
        in_specs=[
            pl.BlockSpec((1, tq, 1024), lambda i, j: (i, j, 0)),
            pl.BlockSpec((1, tq, 128), lambda i, j: (i, j, 0)),
            pl.BlockSpec((1, nc, 128), lambda i, j: (i, 0, 0)),
            pl.BlockSpec((1, nc, 128), lambda i, j: (i, 0, 0)),
            kv_blk(2), kv_blk(3), kv_blk(4), kv_blk(5),
            pl.BlockSpec((128, t_len), lambda i, j: (0, 0)),
            pl.BlockSpec((nc, 128), lambda i, j: (0, 0)),
        ],
        out_specs=pl.BlockSpec((1, tq, 512), lambda i, j: (i, j, 0)),
        out_shape=jax.ShapeDtypeStruct((b, t_len, 512), BF16),
        scratch_shapes=[pltpu.VMEM((256, t_len), BF16), pltpu.VMEM((8 * tq, 128), F32),
                        pltpu.VMEM((8 * tq, 128), F32), pltpu.VMEM((8 * tq, 128), F32)],
        compiler_params=_cparams(("arbitrary", "arbitrary")),
        name="attn_prompt",
    )(qq, gates, kc, vc, kvtb, kvtb, kvtb, kvtb, et, ov)


def _attn_sample_kernel(pt_ref, *refs, n_pages, page, tq, past, win_rows, spb):
    pages = refs[:spb * n_pages]
    (qq_ref, gate_ref, kc_ref, vc_ref, new_ref, cwin_ref, neww_ref, es_ref,
     ov_ref) = refs[spb * n_pages:spb * n_pages + 9]
    o_ref, wout_ref = refs[-2:]
    rows = 8 * tq
    pos_base = past
    for sq in range(spb):
        seq_pages = pages[sq * n_pages:(sq + 1) * n_pages]
        qq = qq_ref[sq].astype(F32)
        qc = _q_rows(qq[:, 0:512], tq).astype(BF16)
        qs = _q_rows(qq[:, 512:1024], tq).astype(BF16)

        o_cmp, imp = _compressed_branch(qc, kc_ref[sq], vc_ref[sq], ov_ref[...], tq, pos_base)
        sel = _select_blocks(imp, tq, pos_base, -(-(past + tq) // SEL_BLOCK))

        new = jnp.concatenate([new_ref[sq].astype(F32), jnp.zeros((128 - tq, 512), F32)], axis=0).astype(BF16)

        kst = jnp.concatenate([pg[0:128, :].astype(BF16) for pg in seq_pages], axis=1)
        vst = jnp.concatenate([pg[128:256, :].astype(BF16) for pg in seq_pages], axis=1)
        n_keys = past + 128
        member = _dot(sel.astype(BF16), es_ref[...])
        member = jnp.concatenate([member[0:tq]] * 4 + [member[tq:2 * tq]] * 4, axis=0)
        kpos = lax.broadcasted_iota(jnp.int32, (rows, n_keys), 1)
        ok = (member > 0.5) & (kpos <= _row_pos(tq, n_keys, pos_base))
        s_past = jnp.where(ok[:, 0:past], _dot(qs, kst), NEG)
        s_new = jnp.where(ok[:, past:], _dot_nt(qs, new[:, 0:128]), NEG)
        o_slc = _softmax_pv([(s_past, vst, True), (s_new, new[:, 128:256], False)])

        cw = cwin_ref[sq]
        n_wk = win_rows + 128
        kpos_w = (past - win_rows) + lax.broadcasted_iota(jnp.int32, (rows, n_wk), 1)
        dist = _row_pos(tq, n_wk, pos_base) - kpos_w
        okw = (dist >= 0) & (dist < WINDOW)
        s_old = jnp.where(okw[:, 0:win_rows], _dot(qs, cw[0:128].astype(BF16)), NEG)
        s_nw = jnp.where(okw[:, win_rows:], _dot_nt(qs, new[:, 256:384]), NEG)
        o_win = _softmax_pv([(s_old, cw[128:256].astype(BF16), True), (s_nw, new[:, 384:512], False)])

        cols = _gated_sum([o_cmp, o_slc, o_win], gate_ref[sq], tq)
        for c in range(4):
            o_ref[sq, :, c * 128:(c + 1) * 128] = cols[c].astype(BF16)

        wout_ref[sq] = pltpu.roll(cw, win_rows - tq, 1)
        wout_ref[sq, :, win_rows - tq:] = neww_ref[sq]


def _attn_sample_call(pt_flat, cache_t, qq, gates, kc, vc, new_rows, cwin_t, new_win_t, es, ov, win_prev, *,
                      l, depth, n_pages, n_phys, past):
    n_seq, tq, _ = qq.shape
    page = cache_t.shape[2]
    win_rows = cwin_t.shape[2]
    nc = kc.shape[1]
    spb = 2 if n_seq % 2 == 0 else 1
    steps = n_seq // spb
    kern = functools.partial(_attn_sample_kernel, n_pages=n_pages, page=page, tq=tq, past=past, win_rows=win_rows,
                             spb=spb)

    def page_spec(k):
        sq, p = divmod(k, n_pages)
        return pl.BlockSpec((None, 256, page), lambda i, pt: (l * n_phys + pt[(i * spb + sq) * n_pages + p], 1, 0))

    seq_blk = lambda r, w: pl.BlockSpec((spb, r, w), lambda i, pt: (i, 0, 0))
    win_blk = pl.BlockSpec((spb, 256, win_rows), lambda i, pt: (l * steps + i, 0, 0))
    in_specs = [page_spec(k) for k in range(spb * n_pages)] + [
        seq_blk(tq, 1024), seq_blk(tq, 128), seq_blk(nc, 128), seq_blk(nc, 128), seq_blk(tq, 512),
        win_blk, seq_blk(256, tq),
        pl.BlockSpec((128, past + 128), lambda i, pt: (0, 0)),
        pl.BlockSpec((nc, 128), lambda i, pt: (0, 0)),
    ]
    operands = [pt_flat] + [cache_t] * (spb * n_pages) + [qq, gates, kc, vc, new_rows, cwin_t, new_win_t, es, ov]
    aliases = {}
    if win_prev is not None:
        in_specs.append(pl.BlockSpec(memory_space=pl.ANY))
        aliases = {len(operands): 1}
        operands.append(win_prev)
    return pl.pallas_call(
        kern,
        grid_spec=pltpu.PrefetchScalarGridSpec(
            num_scalar_prefetch=1,
            grid=(steps,),
            in_specs=in_specs,
            out_specs=(seq_blk(tq, 512), win_blk),
        ),
        out_shape=(jax.ShapeDtypeStruct((n_seq, tq, 512), BF16),
                   jax.ShapeDtypeStruct((depth * n_seq, 256, win_rows), F32)),
        input_output_aliases=aliases,
        compiler_params=_cparams(("arbitrary",)),
        name="attn_sample",
    )(*operands)


def _mix_kernel(oa_ref, ocp_ref, x_ref, wo_ref, g_ref, b_ref, wr_ref, br_ref, *rest, alpha):
    x1_ref, route_ref, cnt_ref = rest[-3:]
    mix = _dot(oa_ref[...], wo_ref[0:512, :]) + _dot(ocp_ref[...], wo_ref[512:1024, :])
    x1 = _layer_norm(alpha * x_ref[...] + mix, g_ref[...], b_ref[...])
    x1_ref[...] = x1

    tm = x1.shape[0]
    xh = x1.astype(BF16)
    xl = (x1 - xh.astype(F32)).astype(BF16)
    hw = _dot(xh, wr_ref[...])
    logits = hw[:, 0:128] + hw[:, 128:256] + _dot(xl, wr_ref[:, 0:128]) + br_ref[...]
    lane = lax.broadcasted_iota(jnp.int32, (tm, 128), 1).astype(F32)
    big = 1e9
    is_g = (lane >= N_EXPERTS) & (lane < N_EXPERTS + N_EXPERT_GROUPS)
    lg = jnp.where(is_g, logits, NEG)
    ge = jnp.where(is_g, jnp.exp(lg - jnp.max(lg, axis=-1, keepdims=True)), 0.0)
    gp = ge / jnp.sum(ge, axis=-1, keepdims=True)
    gw = jnp.max(gp, axis=-1, keepdims=True)
    gidx = jnp.min(jnp.where(is_g & (gp == gw), lane - N_EXPERTS, big), axis=-1, keepdims=True)
    in_g = (lane >= gidx * EXPERTS_PER_GROUP) & (lane < (gidx + 1.0) * EXPERTS_PER_GROUP)
    le = jnp.where(in_g, logits, NEG)
    ee = jnp.where(in_g, jnp.exp(le - jnp.max(le, axis=-1, keepdims=True)), 0.0)
    ep = ee / jnp.sum(ee, axis=-1, keepdims=True)
    w1 = jnp.max(jnp.where(in_g, ep, -1.0), axis=-1, keepdims=True)
    i1 = jnp.min(jnp.where(in_g & (ep == w1), lane, big), axis=-1, keepdims=True)
    rest = in_g & (lane != i1)
    w2 = jnp.max(jnp.where(rest, ep, -1.0), axis=-1, keepdims=True)
    i2 = jnp.min(jnp.where(rest & (ep == w2), lane, big), axis=-1, keepdims=True)
    den = w1 + w2
    route_ref[...] = jnp.where(lane == 0.0, i1, jnp.where(lane == 1.0, i2, jnp.where(
        lane == 2.0, gw * (w1 / den), jnp.where(lane == 3.0, gw * (w2 / den), 0.0))))
    pairs = jnp.sum(jnp.where((lane == i1) | (lane == i2), 1.0, 0.0), axis=0, keepdims=True)
    cnt_ref[0] = jnp.broadcast_to(pairs, (8, 128))


def _mix_call(oa, ocp, x, wo, g, b, wr, br, *, l, alpha, tm, tile0, n_all, prev):
    n, d = x.shape
    row = lambda w: pl.BlockSpec((tm, w), lambda i: (i, 0))
    out_row = lambda w: pl.BlockSpec((tm, w), lambda i: (i + tile0, 0))
    vec = lambda w: pl.BlockSpec((None, 1, w), lambda i: (l, 0, 0))
    in_specs = [row(512), row(512), row(d),
                pl.BlockSpec((None, 1024, d), lambda i: (l, 0, 0)), vec(d), vec(d),
                pl.BlockSpec((None, d, 256), lambda i: (l, 0, 0)), vec(128)]
    operands = [oa, ocp, x, wo, g, b, wr, br]
    aliases = {}
    if prev is not None:
        aliases = {len(operands) + k: k for k in range(3)}
        in_specs += [pl.BlockSpec(memory_space=pl.ANY)] * 3
        operands += list(prev)
    return pl.pallas_call(
        functools.partial(_mix_kernel, alpha=alpha),
        grid=(n // tm,),
        in_specs=in_specs,
        out_specs=(out_row(d), out_row(128), pl.BlockSpec((1, 8, 128), lambda i: (i + tile0, 0, 0))),
        out_shape=(jax.ShapeDtypeStruct((n_all, d), F32), jax.ShapeDtypeStruct((n_all, 128), F32),
                   jax.ShapeDtypeStruct((n_all // tm, 8, 128), F32)),
        input_output_aliases=aliases,
        compiler_params=_cparams(("arbitrary",)),
        name="mix_router",
    )(*operands)


MOE_CHUNK = 16
MOE_TT = 512
MOE_TE = 256
MOE_SLOTS = 2 * MOE_TT + N_EXPERTS * MOE_CHUNK
MOE_XW = 1024 + LANES


def _chunk_copy(src, s0, dst, d0, sem):
    return pltpu.make_async_copy(src.at[pl.ds(s0, MOE_CHUNK), :], dst.at[pl.ds(d0, MOE_CHUNK), :], sem)


def _for_each_chunk(i, lo_s, dst_s, nch_s, fn):
    def per_expert(e, total):
        k = i * N_EXPERTS + e
        n = nch_s[k]

        def per_chunk(c, carry):
            fn(pl.multiple_of(lo_s[k] + c * MOE_CHUNK, MOE_CHUNK), pl.multiple_of(dst_s[k] + c * MOE_CHUNK, MOE_CHUNK))
            return carry

        lax.fori_loop(0, n, per_chunk, 0)
        return total + n

    return lax.fori_loop(0, N_EXPERTS, per_expert, 0)


def _split3(c):
    h = c.astype(BF16).astype(F32)
    m = (c - h).astype(BF16).astype(F32)
    r = (c - h - m).astype(BF16).astype(F32)
    lane = lax.broadcasted_iota(jnp.int32, (c.shape[0], 128), 1)
    return jnp.where(lane == 0, h, jnp.where(lane == 1, m, jnp.where(lane == 2, r, 0.0)))


def _tile_chunks(i, nch_s):
    return lax.fori_loop(0, N_EXPERTS, lambda e, total: total + nch_s[i * N_EXPERTS + e], 0)


def _wait_chunks(n, src, dst, sem):
    def body(c, carry):
        _chunk_copy(src, 0, dst, 0, sem).wait()
        return carry
    lax.fori_loop(0, n, body, 0)


def _dispatch_kernel(lo_s, dst_s, nch_s, fst_s, fn_s, x1_ref, route_ref, lov_ref, xs_hbm, xs_sc, z_sc, sem):
    i = pl.program_id(0)
    last = pl.num_programs(0) - 1
    tt = MOE_TT
    cur = i % 2
    buf = xs_sc.at[cur]

    @pl.when(i == 0)
    def _():
        z_sc[...] = jnp.zeros_like(z_sc)

        def per_expert(e, total):
            def per_chunk(c, carry):
                _chunk_copy(z_sc, 0, xs_hbm, pl.multiple_of(fst_s[e] + c * MOE_CHUNK, MOE_CHUNK), sem.at[1]).start()
                return carry
            lax.fori_loop(0, fn_s[e], per_chunk, 0)
            return total + fn_s[e]

        _wait_chunks(lax.fori_loop(0, N_EXPERTS, per_expert, 0), z_sc, xs_hbm, sem.at[1])

    @pl.when(i >= 2)
    def _():
        _wait_chunks(_tile_chunks(i - 2, nch_s), buf, xs_hbm, sem.at[cur])

    route = route_ref[...]
    rt = route.T
    eio = lax.broadcasted_iota(jnp.int32, (N_EXPERTS, tt), 0).astype(F32)
    m1 = eio == rt[0:1]
    m2 = eio == rt[1:2]
    before = (lax.broadcasted_iota(jnp.int32, (tt, tt), 0) < lax.broadcasted_iota(jnp.int32, (tt, tt), 1))
    rank = _dot(jnp.where(m1 | m2, 1.0, 0.0).astype(BF16), jnp.where(before, 1.0, 0.0).astype(BF16))
    slot = jnp.concatenate([lov_ref[0]] * (tt // 128), axis=1) + rank
    s1 = jnp.sum(jnp.where(m1, slot, 0.0), axis=0, keepdims=True)
    s2 = jnp.sum(jnp.where(m2, slot, 0.0), axis=0, keepdims=True)
    sio = lax.broadcasted_iota(jnp.int32, (MOE_SLOTS, tt), 0).astype(F32)
    p1 = jnp.where(sio == s1, 1.0, 0.0).astype(BF16)
    p2 = jnp.where(sio == s2, 1.0, 0.0).astype(BF16)
    buf[:, 0:1024] = _dot(p1 + p2, x1_ref[...].astype(BF16)).astype(BF16)
    cw = _dot(p1, _split3(route[:, 2:3]).astype(BF16)) + _dot(p2, _split3(route[:, 3:4]).astype(BF16))
    buf[:, 1024:MOE_XW] = cw.astype(BF16)

    n_out = _for_each_chunk(i, lo_s, dst_s, nch_s,
                            lambda s0, d0: _chunk_copy(buf, s0, xs_hbm, d0, sem.at[cur]).start())

    @pl.when(i == last)
    def _():
        _wait_chunks(n_out, buf, xs_hbm, sem.at[cur])

        @pl.when(i >= 1)
        def _():
            _wait_chunks(_tile_chunks(i - 1, nch_s), xs_sc.at[1 - cur], xs_hbm, sem.at[1 - cur])


def _experts_kernel(te_s, blk_s, nact_s, xs_ref, wg_ref, wu_ref, wd_ref, y_ref, wgu_sc, wd_sc):
    k = pl.program_id(0)
    active = k < nact_s[0]
    fresh = (k == 0) | (te_s[k] != te_s[jnp.maximum(k - 1, 0)])

    @pl.when(active & fresh)
    def _():
        wgu_sc[:, 0:D_EXPERT] = wg_ref[...].astype(BF16)
        wgu_sc[:, D_EXPERT:2 * D_EXPERT] = wu_ref[...].astype(BF16)
        wd_sc[...] = wd_ref[...].astype(BF16)

    @pl.when(active)
    def _():
        cw = xs_ref[:, 1024:MOE_XW].astype(F32)
        c = cw[:, 0:1] + cw[:, 1:2] + cw[:, 2:3]
        hgu = _dot(xs_ref[:, 0:1024], wgu_sc[...])
        hg = hgu[:, 0:D_EXPERT]
        hid = hg * (1.0 / (1.0 + jnp.exp(-hg))) * hgu[:, D_EXPERT:2 * D_EXPERT]
        y_ref[...] = _dot((hid * c).astype(BF16), wd_sc[...]).astype(BF16)


def _combine_kernel(lo_s, dst_s, nch_s, x1_ref, route_ref, lor_ref, g_ref, b_ref, ys_hbm, oa_ref, ob_ref, ys_sc,
                    sem, *, alpha, nt_a):
    i = pl.program_id(0)
    tt = MOE_TT
    cur = i % 2

    def fetch(tile, slot):
        return _for_each_chunk(tile, lo_s, dst_s, nch_s,
                               lambda s0, d0: _chunk_copy(ys_hbm, d0, ys_sc.at[slot], s0, sem.at[slot]).start())

    @pl.when(i == 0)
    def _():
        ys_sc[...] = jnp.zeros_like(ys_sc)
        fetch(0, 0)

    @pl.when(i + 1 < pl.num_programs(0))
    def _():
        fetch(i + 1, 1 - cur)

    route = route_ref[...]
    lane = lax.broadcasted_iota(jnp.int32, (tt, 128), 1).astype(F32)
    m1 = lane == route[:, 0:1]
    m2 = lane == route[:, 1:2]
    before = (lax.broadcasted_iota(jnp.int32, (tt, tt), 1) < lax.broadcasted_iota(jnp.int32, (tt, tt), 0))
    rank = _dot(jnp.where(before, 1.0, 0.0).astype(BF16), jnp.where(m1 | m2, 1.0, 0.0).astype(BF16))
    slot = lor_ref[0, 0:1, :] + rank
    s1 = jnp.sum(jnp.where(m1, slot, 0.0), axis=-1, keepdims=True)
    s2 = jnp.sum(jnp.where(m2, slot, 0.0), axis=-1, keepdims=True)
    sio = lax.broadcasted_iota(jnp.int32, (tt, MOE_SLOTS), 1).astype(F32)
    place = jnp.where((sio == s1) | (sio == s2), 1.0, 0.0).astype(BF16)

    _wait_chunks(_tile_chunks(i, nch_s), ys_hbm, ys_sc.at[cur], sem.at[cur])
    y = _dot(place, ys_sc[cur])
    out = _layer_norm(alpha * x1_ref[...] + y, g_ref[...], b_ref[...])

    @pl.when(i < nt_a)
    def _():
        oa_ref[...] = out

    @pl.when(i >= nt_a)
    def _():
        ob_ref[...] = out


def _moe_call(x1, route, cnt, w_gate, w_up, w_down, g, b, *, l, alpha, nt_a):
    n, d = x1.shape
    tt, te, ch, n_e = MOE_TT, MOE_TE, MOE_CHUNK, N_EXPERTS
    nt = n // tt
    n_et = -(-(2 * n + nt * n_e * (ch - 1)) // te) + n_e
    i32 = jnp.int32

    pairs = cnt[:, 0, :n_e].astype(i32)
    pc = (pairs + ch - 1) // ch * ch
    lo = jnp.cumsum(pc, axis=1) - pc
    tot = jnp.sum(pc, axis=0)
    reg = (tot + te - 1) // te * te
    base = jnp.cumsum(reg) - reg
    dst = base[None, :] + jnp.cumsum(pc, axis=0) - pc
    tiles_e = reg // te
    ends = jnp.cumsum(tiles_e)
    n_act = ends[-1]
    k = jnp.arange(n_et, dtype=i32)
    blk = jnp.minimum(k, n_act - 1)
    tile_e = jnp.minimum(jnp.sum((ends[None, :] <= blk[:, None]).astype(i32), axis=1), n_e - 1)
    flat = lambda a: a.reshape(-1).astype(i32)
    lo_s, dst_s, nch_s = flat(lo), flat(dst), flat(pc // ch)
    lov = jnp.broadcast_to(lo.astype(F32)[:, :, None], (nt, n_e, 128))
    lor = jnp.broadcast_to(jnp.pad(lo.astype(F32), ((0, 0), (0, 128 - n_e)))[:, None, :], (nt, 8, 128))

    xs = pl.pallas_call(
        _dispatch_kernel,
        grid_spec=pltpu.PrefetchScalarGridSpec(
            num_scalar_prefetch=5,
            grid=(nt,),
            in_specs=[pl.BlockSpec((tt, d), lambda i, *_: (i, 0)),
                      pl.BlockSpec((tt, 128), lambda i, *_: (i, 0)),
                      pl.BlockSpec((1, n_e, 128), lambda i, *_: (i, 0, 0))],
            out_specs=pl.BlockSpec(memory_space=pl.ANY),
            scratch_shapes=[pltpu.VMEM((2, MOE_SLOTS, MOE_XW), BF16), pltpu.VMEM((ch, MOE_XW), BF16),
                            pltpu.SemaphoreType.DMA((2,))],
        ),
        out_shape=jax.ShapeDtypeStruct((n_et * te, MOE_XW), BF16),
        compiler_params=_cparams(("arbitrary",)),
        name="moe_dispatch",
    )(lo_s, dst_s, nch_s, flat(base + tot), flat((reg - tot) // ch), x1, route, lov)

    wspec = lambda r, c: pl.BlockSpec((None, None, r, c), lambda k, te_s, blk_s, na: (l, te_s[k], 0, 0))
    ys = pl.pallas_call(
        _experts_kernel,
        grid_spec=pltpu.PrefetchScalarGridSpec(
            num_scalar_prefetch=3,
            grid=(n_et,),
            in_specs=[pl.BlockSpec((te, MOE_XW), lambda k, te_s, blk_s, na: (blk_s[k], 0)),
                      wspec(d, D_EXPERT), wspec(d, D_EXPERT), wspec(D_EXPERT, d)],
            out_specs=pl.BlockSpec((te, d), lambda k, te_s, blk_s, na: (blk_s[k], 0)),
            scratch_shapes=[pltpu.VMEM((d, 2 * D_EXPERT), BF16), pltpu.VMEM((D_EXPERT, d), BF16)],
        ),
        out_shape=jax.ShapeDtypeStruct((n_et * te, d), BF16),
        compiler_params=_cparams(("arbitrary",)),
        name="moe_experts",
    )(tile_e, blk, n_act.reshape(1).astype(i32), xs, w_gate, w_up, w_down)

    vec = pl.BlockSpec((None, 1, d), lambda i, *_: (l, 0, 0))
    return pl.pallas_call(
        functools.partial(_combine_kernel, alpha=alpha, nt_a=nt_a),
        grid_spec=pltpu.PrefetchScalarGridSpec(
            num_scalar_prefetch=3,
            grid=(nt,),
            in_specs=[pl.BlockSpec((tt, d), lambda i, *_: (i, 0)),
                      pl.BlockSpec((tt, 128), lambda i, *_: (i, 0)),
                      pl.BlockSpec((1, 8, 128), lambda i, *_: (i, 0, 0)),
                      vec, vec,
                      pl.BlockSpec(memory_space=pl.ANY)],
            out_specs=(pl.BlockSpec((tt, d), lambda i, *_: (jnp.minimum(i, nt_a - 1), 0)),
                       pl.BlockSpec((tt, d), lambda i, *_: (jnp.maximum(i - nt_a, 0), 0))),
            scratch_shapes=[pltpu.VMEM((2, MOE_SLOTS, d), BF16), pltpu.SemaphoreType.DMA((2,))],
        ),
        out_shape=(jax.ShapeDtypeStruct((nt_a * tt, d), F32), jax.ShapeDtypeStruct((n - nt_a * tt, d), F32)),
        compiler_params=_cparams(("arbitrary",)),
        name="moe_combine",
    )(lo_s, dst_s, nch_s, x1, route, lor, g, b, ys)


def _rope_table(pos):
    half = ROT_DIM // 2
    inv = ROPE_THETA ** (-jnp.arange(0, ROT_DIM, 2, dtype=F32) / ROT_DIM)
    ang = jnp.asarray(pos, F32)[:, None] * inv[None, :]
    cos, sin = jnp.cos(ang), jnp.sin(ang)
    n = ang.shape[0]
    z = jnp.zeros((n, HEAD_DIM - ROT_DIM), F32)
    zh = jnp.zeros((n, half), F32)
    c64 = jnp.concatenate([cos, cos, jnp.ones_like(z)], axis=1)
    s1 = jnp.concatenate([-sin, zh, z], axis=1)
    s2 = jnp.concatenate([zh, sin, z], axis=1)
    return jnp.concatenate([c64, c64, s1, s1, s2, s2], axis=1)


def _overlap_table(n_rows):
    c = np.arange(n_rows)[:, None] * CMP_STRIDE
    s = np.arange(128)[None, :] * SEL_BLOCK
    return jnp.asarray(((c < s + SEL_BLOCK) & (c + CMP_BLOCK > s)).astype(np.float32), BF16)


def _block_indicator(n_keys):
    k = np.arange(n_keys)[None, :] // SEL_BLOCK
    return jnp.asarray((np.arange(128)[:, None] == k).astype(np.float32), BF16)


def _block_diag2(w):
    z = jnp.zeros_like(w)
    return jnp.concatenate([jnp.concatenate([w, z], axis=-1), jnp.concatenate([z, w], axis=-1)], axis=-2)


def kernel(x_prompt, x_sample, cache_kv, cache_win, state_conv, state_pool, page_table, ln1_g, ln1_b, w_in, pe_cmp, w_cmp1, w_cmp2, conv_w, pool_w, pool_scale, w_o, ln2_g, ln2_b, w_rg, b_rg, w_re, b_re, w_gate, w_up, w_down):
    n_b, t_len, d_model = x_prompt.shape
    n_seq, t_dec, _ = x_sample.shape
    depth = w_in.shape[0]
    n_phys, page = cache_kv.shape[1], cache_kv.shape[2]
    n_pages = page_table.shape[1]
    past = n_pages * page
    win_rows = cache_win.shape[2]
    alpha = float((2 * depth) ** 0.25)
    assert d_model == 1024 and t_len % 512 == 0 and t_dec == 8 and past % SEL_BLOCK == 0 and win_rows == WINDOW

    pair_order = [h for c in range(4) for h in (c, c + 4)]
    w_main = jnp.concatenate([w_in[:, :, h * 64:(h + 1) * 64] for h in pair_order]
                             + [w_in[:, :, _C_BG:], w_in[:, :, _C_GL:_C_BG],
                              jnp.zeros((depth, d_model, 128 - (_C_BG - _C_GL)), F32)], axis=2)
    wm_p = jnp.concatenate([w_main, w_in[:, :, _C_KV:_C_KV + 256]], axis=2).astype(BF16)
    wm_s = jnp.concatenate([w_main, w_in[:, :, _C_KV + 256:_C_KV + 768]], axis=2).astype(BF16)
    wk_t = jnp.swapaxes(w_in[:, :, _C_KV:_C_GL], 1, 2).astype(BF16)
    cw8 = jnp.concatenate([conv_w, jnp.zeros((depth, 8 - CONV_K, 256), F32)], axis=1)
    pw_bd = jnp.zeros((depth, 256, 256), F32)
    for g in range(4):
        pw_bd = pw_bd.at[:, g * 64:(g + 1) * 64, g * 64:(g + 1) * 64].set(pool_w[:, g])
    pw_bd = pw_bd.astype(BF16)
    ps3 = pool_scale[:, None, :]

    w1r = w_cmp1.reshape(depth, 2, CMP_BLOCK, HEAD_DIM, CMP_HIDDEN)
    bd_half = lambda w: _block_diag2(w).reshape(depth, 2, CMP_STRIDE * 128, 2 * CMP_HIDDEN).astype(BF16)
    wa = bd_half(w1r[:, :, :CMP_STRIDE])
    wb = bd_half(w1r[:, :, CMP_STRIDE:])
    w2 = _block_diag2(w_cmp2).astype(BF16)
    pe2 = jnp.concatenate([pe_cmp, pe_cmp], axis=-1).reshape(depth, 2, 2, CMP_STRIDE, 128)

    wo_perm = jnp.concatenate([w_o[:, h * 64:(h + 1) * 64] for h in pair_order] + [w_o[:, ATT_WIDTH:]],
                              axis=1).astype(BF16)
    wr32 = jnp.concatenate([w_re, w_rg, jnp.zeros((depth, d_model, 128 - N_EXPERTS - N_EXPERT_GROUPS), F32)], axis=2)
    wr_hi = wr32.astype(BF16)
    wr = jnp.concatenate([wr_hi, (wr32 - wr_hi.astype(F32)).astype(BF16)], axis=2)
    br = jnp.concatenate([b_re, b_rg, jnp.zeros((depth, 128 - N_EXPERTS - N_EXPERT_GROUPS), F32)], axis=1)[:, None, :]
    g1, b1, g2, b2 = ln1_g[:, None, :], ln1_b[:, None, :], ln2_g[:, None, :], ln2_b[:, None, :]

    tab_p = _rope_table(np.arange(t_len))
    tab_s = _rope_table(past + np.repeat(np.arange(t_dec), n_seq))
    tab_pt, tab_st = tab_p.T, tab_s.T
    nc_p = t_len // CMP_STRIDE
    nc_s = past // CMP_STRIDE
    ov_p, ov_s = _overlap_table(nc_p), _overlap_table(nc_s)
    et_p = _block_indicator(t_len)
    es_s = _block_indicator(past + 128)

    cache_t = jnp.transpose(cache_kv, (0, 1, 3, 4, 5, 2)).reshape(depth * n_phys, 512, page)
    cwin_t = jnp.transpose(cache_win, (0, 1, 3, 4, 5, 2)).reshape(depth * n_seq, 256, win_rows)
    pt_flat = page_table.reshape(-1)

    def tm_state(st):
        k = st.shape[2]
        st = jnp.swapaxes(st, 1, 2)
        st = jnp.concatenate([jnp.zeros((depth, CARRY_STEPS - k, n_seq, 256), F32), st], axis=1)
        return st.reshape(depth, CARRY_STEPS * n_seq, 256)

    cst_s, pst_s = tm_state(state_conv), tm_state(state_pool)
    zst_p = jnp.zeros((n_b, CARRY_STEPS, 256), F32)

    tm_p = 1024 if t_len % 1024 == 0 else 512
    tm_row = 512
    seqs_step = 4 if n_seq % 4 == 0 else 1

    xp = x_prompt
    xs = x_sample.reshape(n_seq * t_dec, d_model)
    outs = {k: [] for k in ("kv_s", "win_p", "conv_p", "conv_s", "pool_p", "pool_s")}
    win_s_all = None
    kvt_p_all = None
    for l in range(depth):
        qq, gates, ocp, rows, kvt_p_all, wint, kvtb, cnew, pnew = _proj_call(
            xp, wm_p, wk_t, tab_p, tab_pt, cw8, pw_bd, ps3, zst_p, zst_p,
            l=l, tm=tm_p, rs=1, pos0=0, rows_dtype=F32, rows_rope=(), kv_slabs=depth, kvt_prev=kvt_p_all)
        kc, vc = _compress_rows_call(rows, pe2, wa, wb, w2, l=l)
        oa = _attn_prompt_call(qq, gates, kc, vc, kvtb, et_p, ov_p)
        n_p = n_b * t_len
        n_s = n_seq * t_dec
        routed = _mix_call(oa.reshape(n_p, 512), ocp.reshape(n_p, 512), xp.reshape(n_p, d_model),
                           wo_perm, g1, b1, wr, br, l=l, alpha=alpha, tm=MOE_TT, tile0=0, n_all=n_p + n_s,
                           prev=None)
        outs["win_p"].append(jnp.transpose(wint[:, :, t_len - WINDOW:].reshape(n_b, 2, 2, HEAD_DIM, WINDOW),
                                           (0, 4, 1, 2, 3)))
        outs["conv_p"].append(cnew[:, CARRY_STEPS - (CONV_K - 1):])
        outs["pool_p"].append(pnew[:, CARRY_STEPS - POOL_STATE:])

        xs_tm = jnp.swapaxes(xs.reshape(n_seq, t_dec, d_model), 0, 1).reshape(1, t_dec * n_seq, d_model)
        qq, gates, ocp, rows, kvt, wint, kvtb, cnew, pnew = _proj_call(
            xs_tm, wm_s, wk_t, tab_s, tab_st, cw8, pw_bd, ps3, cst_s[l][None], pst_s[l][None],
            l=l, tm=t_dec * n_seq, rs=n_seq, pos0=past, rows_dtype=BF16, rows_rope=(0, 2))
        seq_major = lambda a: jnp.swapaxes(a.reshape(t_dec, n_seq, a.shape[-1]), 0, 1)
        kc, vc = _compress_pages_call(pt_flat, cache_t, pe2, wa, wb, w2, l=l, n_seq=n_seq, n_pages=n_pages,
                                      n_phys=n_phys, seqs_step=seqs_step)
        kc = kc.reshape(n_seq, nc_s, 128)
        vc = vc.reshape(n_seq, nc_s, 128)
        new_win_t = jnp.transpose(wint.reshape(256, t_dec, n_seq), (2, 0, 1))
        oa, win_s_all = _attn_sample_call(pt_flat, cache_t, seq_major(qq), seq_major(gates), kc, vc,
                                          seq_major(rows), cwin_t, new_win_t, es_s, ov_s, win_s_all,
                                          l=l, depth=depth, n_pages=n_pages, n_phys=n_phys, past=past)
        x1, route, cnt = _mix_call(oa.reshape(n_s, 512), seq_major(ocp).reshape(n_s, 512), xs,
                                   wo_perm, g1, b1, wr, br, l=l, alpha=alpha, tm=MOE_TT, tile0=n_p // MOE_TT,
                                   n_all=n_p + n_s, prev=routed)
        xp, xs = _moe_call(x1, route, cnt, w_gate, w_up, w_down, g2, b2, l=l, alpha=alpha, nt_a=n_p // MOE_TT)
        xp = xp.reshape(n_b, t_len, d_model)
        kvt5 = kvt.reshape(4, 2, HEAD_DIM, t_dec, n_seq)
        outs["kv_s"].append(jnp.transpose(kvt5, (4, 3, 0, 1, 2)))
        st_sm = lambda a, k: jnp.swapaxes(a.reshape(CARRY_STEPS, n_seq, 256)[CARRY_STEPS - k:], 0, 1)
        outs["conv_s"].append(st_sm(cnew, CONV_K - 1))
        outs["pool_s"].append(st_sm(pnew, POOL_STATE))

    st = lambda k: jnp.stack(outs[k])
    win_s = jnp.transpose(win_s_all.reshape(depth, n_seq, 2, 2, HEAD_DIM, win_rows), (0, 1, 5, 2, 3, 4))
    kv_p = jnp.transpose(kvt_p_all.reshape(depth, n_b, 4, 2, HEAD_DIM, t_len), (0, 1, 5, 2, 3, 4))
    return (xp, xs.reshape(n_seq, t_dec, d_model), kv_p, st("kv_s"), st("win_p"), win_s,
            st("conv_p"), st("conv_s"), st("pool_p"), st("pool_s"))
```

```python
import functools

import numpy as np
import jax
import jax.numpy as jnp
from jax import lax
from jax.experimental import pallas as pl
from jax.experimental.pallas import tpu as pltpu

F32 = jnp.float32
BF16 = jnp.bfloat16

HEAD_DIM = 64
N_HEADS = 8
N_KV_HEADS = 2
GQA_REP = N_HEADS // N_KV_HEADS
ATT_WIDTH = N_HEADS * HEAD_DIM
KV_W = N_KV_HEADS * HEAD_DIM
ROT_DIM = HEAD_DIM // 4
ROPE_THETA = 500000.0
CMP_BLOCK = 32
CMP_STRIDE = 16
CMP_HIDDEN = 256
SEL_BLOCK = 64
SEL_TOP_K = 16
WINDOW = 512
Q_BLOCK = 128
CONV_K = 3
POOL_WINDOWS = (2, 4, 8, 16)
POOL_STATE = max(POOL_WINDOWS) - 1
CARRY_STEPS = 16
N_EXPERT_GROUPS = 4
EXPERTS_PER_GROUP = 8
N_EXPERTS = N_EXPERT_GROUPS * EXPERTS_PER_GROUP
D_EXPERT = 256
LN_EPS = 1e-5
NEG = -1e30
FORCE_BONUS = 1e4
LOG2_E = 1.4426950408889634
LANES = 128
VMEM_LIMIT = 56 * 1024 * 1024

_C_Q, _C_KV, _C_GL, _C_BG = 0, 512, 1280, 1304
_MAIN_W = 1664


def _cparams(sem):
    return pltpu.CompilerParams(dimension_semantics=sem, vmem_limit_bytes=VMEM_LIMIT)


def _dot(a, b):
    return jnp.dot(a, b, preferred_element_type=F32)


def _dot_nt(a, b):
    return lax.dot_general(a, b, (((1,), (1,)), ((), ())), preferred_element_type=F32)


def _layer_norm(y, g, b):
    mu = jnp.mean(y, axis=-1, keepdims=True)
    d = y - mu
    var = jnp.mean(d * d, axis=-1, keepdims=True)
    return d * lax.rsqrt(var + LN_EPS) * g + b


def _rope_rows(v, tab):
    return (v * tab[:, 0:128] + pltpu.roll(v, 120, 1) * tab[:, 128:256]
            + pltpu.roll(v, 8, 1) * tab[:, 256:384])


def _rope_cols(v, tab):
    return (v * tab[0:128] + pltpu.roll(v, 120, 0) * tab[128:256]
            + pltpu.roll(v, 8, 0) * tab[256:384])


def _proj_kernel(x_ref, wm_ref, wk_ref, tabr_ref, tabt_ref, cw_ref, pw_ref, ps_ref, cst_ref, pst_ref, *rest,
                 tm, rs, pos0, rows_rope):
    qq_ref, gate_ref, ocp_ref, rows_ref, kvt_ref, wint_ref, kvtb_ref, cnew_ref, pnew_ref, cu_sc, cp_sc = rest[-11:]
    ti = pl.program_id(1)
    carry = CARRY_STEPS * rs

    @pl.when(ti == 0)
    def _():
        cu_sc[...] = cst_ref[0]
        cp_sc[...] = pst_ref[0]

    xb = x_ref[0].astype(BF16)
    h = _dot(xb, wm_ref[...])
    kvt = _dot_nt(wk_ref[...], xb)
    tabr = tabr_ref[...]
    tabt = tabt_ref[...]

    scale = HEAD_DIM ** -0.5 * LOG2_E
    for c in range(4):
        qc = h[:, c * 128:(c + 1) * 128]
        qq_ref[0, :, c * 128:(c + 1) * 128] = (qc * scale).astype(BF16)
        qq_ref[0, :, 512 + c * 128:512 + (c + 1) * 128] = (_rope_rows(qc, tabr) * scale).astype(BF16)
    gate_ref[0] = 1.0 / (1.0 + jnp.exp(-h[:, 1536:1664]))

    for c in range(6):
        blk = kvt[c * 128:(c + 1) * 128]
        if c in (2, 4):
            blk = _rope_cols(blk, tabt)
        if c < 4:
            kvt_ref[0, c * 128:(c + 1) * 128, :] = blk
        else:
            wint_ref[0, (c - 4) * 128:(c - 3) * 128, :] = blk
        kvtb_ref[0, c * 128:(c + 1) * 128, :] = blk.astype(BF16)

    n_rows = rows_ref.shape[2]
    for c in range(n_rows // 128):
        blk = h[:, _MAIN_W + c * 128:_MAIN_W + (c + 1) * 128]
        if c in rows_rope:
            blk = _rope_rows(blk, tabr)
        rows_ref[0, :, c * 128:(c + 1) * 128] = blk.astype(rows_ref.dtype)

    bg = h[:, 512:768]
    cg = h[:, 768:1024]
    vc = h[:, 1024:1280]
    pin = h[:, 1280:1536]

    u = cg * vc
    eu = jnp.concatenate([cu_sc[...], u], axis=0)
    cw = cw_ref[...]
    y = (eu[carry - 2 * rs:carry - 2 * rs + tm] * cw[0:1] + eu[carry - rs:carry - rs + tm] * cw[1:2]
         + u * cw[2:3])
    o_conv = bg * y
    new_u = eu[tm:tm + carry]
    cu_sc[...] = new_u
    cnew_ref[0] = new_u

    ep = jnp.concatenate([cp_sc[...], pin], axis=0)
    s2 = ep[rs:] + ep[:-rs]
    s4 = s2[2 * rs:] + s2[:-2 * rs]
    s8 = s4[4 * rs:] + s4[:-4 * rs]
    s16 = s8[8 * rs:] + s8[:-8 * rs]
    take = lambda a: a[a.shape[0] - tm:]
    lane = lax.broadcasted_iota(jnp.int32, (tm, 256), 1)
    row = lax.broadcasted_iota(jnp.int32, (tm, 256), 0)
    dsum = jnp.where(lane < 64, take(s2), jnp.where(lane < 128, take(s4), jnp.where(lane < 192, take(s8), take(s16))))
    wlen = jnp.where(lane < 64, 2, jnp.where(lane < 128, 4, jnp.where(lane < 192, 8, 16)))
    step = pos0 + (ti * tm + row) // rs
    cnt = jnp.minimum(wlen, step + 1).astype(F32)
    d = dsum / cnt - pin
    o_pool = _dot(d.astype(BF16), pw_ref[...]) * ps_ref[...]
    new_p = ep[tm:tm + carry]
    cp_sc[...] = new_p
    pnew_ref[0] = new_p

    ocp_ref[0, :, 0:256] = o_conv.astype(BF16)
    ocp_ref[0, :, 256:512] = o_pool.astype(BF16)


def _proj_call(x3, wm, wk, tabr, tabt, cw, pw, ps, cst, pst, *, l, tm, rs, pos0, rows_dtype, rows_rope,
               kv_slabs=1, kvt_prev=None):
    n_sg, n_rows, d_model = x3.shape
    kv_slab = l if kv_slabs > 1 else 0
    tiles = n_rows // tm
    n_main = wm.shape[2]
    n_rowcols = n_main - _MAIN_W
    carry = CARRY_STEPS * rs
    kern = functools.partial(_proj_kernel, tm=tm, rs=rs, pos0=pos0, rows_rope=rows_rope)
    out_shape = (
        jax.ShapeDtypeStruct((n_sg, n_rows, 1024), BF16),
        jax.ShapeDtypeStruct((n_sg, n_rows, 128), F32),
        jax.ShapeDtypeStruct((n_sg, n_rows, 512), BF16),
        jax.ShapeDtypeStruct((n_sg, n_rows, n_rowcols), rows_dtype),
        jax.ShapeDtypeStruct((kv_slabs * n_sg, 512, n_rows), F32),
        jax.ShapeDtypeStruct((n_sg, 256, n_rows), F32),
        jax.ShapeDtypeStruct((n_sg, 768, n_rows), BF16),
        jax.ShapeDtypeStruct((n_sg, carry, 256), F32),
        jax.ShapeDtypeStruct((n_sg, carry, 256), F32),
    )
    row_blk = lambda w: pl.BlockSpec((1, tm, w), lambda s, t: (s, t, 0))
    col_blk = lambda w: pl.BlockSpec((1, w, tm), lambda s, t: (s, 0, t))
    st_blk = pl.BlockSpec((1, carry, 256), lambda s, t: (s, 0, 0))
    in_specs = [
        row_blk(d_model),
        pl.BlockSpec((None, d_model, n_main), lambda s, t: (l, 0, 0)),
        pl.BlockSpec((None, 768, d_model), lambda s, t: (l, 0, 0)),
        pl.BlockSpec((tm, 384), lambda s, t: (t, 0)),
        pl.BlockSpec((384, tm), lambda s, t: (0, t)),
        pl.BlockSpec((None, 8, 256), lambda s, t: (l, 0, 0)),
        pl.BlockSpec((None, 256, 256), lambda s, t: (l, 0, 0)),
        pl.BlockSpec((None, 1, 256), lambda s, t: (l, 0, 0)),
        st_blk, st_blk,
    ]
    operands = [x3, wm, wk, tabr, tabt, cw, pw, ps, cst, pst]
    aliases = {}
    if kvt_prev is not None:
        aliases = {len(operands): 4}
        in_specs.append(pl.BlockSpec(memory_space=pl.ANY))
        operands.append(kvt_prev)
    return pl.pallas_call(
        kern,
        grid=(n_sg, tiles),
        in_specs=in_specs,
        out_specs=(row_blk(1024), row_blk(128), row_blk(512), row_blk(n_rowcols),
                   pl.BlockSpec((1, 512, tm), lambda s, t: (kv_slab * n_sg + s, 0, t)),
                   col_blk(256), col_blk(768), st_blk, st_blk),
        out_shape=out_shape,
        scratch_shapes=[pltpu.VMEM((carry, 256), F32), pltpu.VMEM((carry, 256), F32)],
        input_output_aliases=aliases,
        compiler_params=_cparams(("arbitrary", "arbitrary")),
        name="proj",
    )(*operands)


def _gelu_tanh(x):
    return 0.5 * x * (1.0 + jnp.tanh(0.7978845608028654 * (x + 0.044715 * x * x * x)))


def _compress_core(read_rows, n, pe_ref, wa_ref, wb_ref, w2_ref, kc_ref, vc_ref):
    for kv, out_ref in ((0, kc_ref), (1, vc_ref)):
        cols = [read_rows(kv, r) for r in range(CMP_STRIDE)]
        xa = jnp.concatenate([cols[r] + pe_ref[kv, 0, r:r + 1, :] for r in range(CMP_STRIDE)], axis=1)
        xb = jnp.concatenate([cols[r] + pe_ref[kv, 1, r:r + 1, :] for r in range(CMP_STRIDE)], axis=1)
        a = _dot(xa.astype(BF16), wa_ref[kv])
        b = _dot(xb.astype(BF16), wb_ref[kv])
        pre = a + pltpu.roll(b, n - 1, 0)
        hid = _gelu_tanh(pre)
        out_ref[0] = _dot(hid.astype(BF16), w2_ref[kv]).astype(BF16)


def _compress_rows_kernel(k_ref, v_ref, pe_ref, wa_ref, wb_ref, w2_ref, kc_ref, vc_ref, *, n):
    srcs = (k_ref, v_ref)
    read = lambda kv, r: srcs[kv][0, pl.ds(r, n, stride=CMP_STRIDE), :]
    _compress_core(read, n, pe_ref, wa_ref, wb_ref, w2_ref, kc_ref, vc_ref)


def _compress_pages_kernel(pt_ref, *refs, n_pages_step, page, n):
    pages = refs[:n_pages_step]
    pe_ref, wa_ref, wb_ref, w2_ref, kc_ref, vc_ref, kbuf, vbuf = refs[n_pages_step:]
    for i, pg in enumerate(pages):
        kbuf[i * page:(i + 1) * page, :] = pg[0:128, :].T
        vbuf[i * page:(i + 1) * page, :] = pg[128:256, :].T
    bufs = (kbuf, vbuf)
    read = lambda kv, r: bufs[kv][pl.ds(r, n, stride=CMP_STRIDE), :]
    _compress_core(read, n, pe_ref, wa_ref, wb_ref, w2_ref, kc_ref, vc_ref)


def _cmp_weight_specs(l, nidx):
    im4 = (lambda *a: (l, 0, 0, 0))
    im5 = (lambda *a: (l, 0, 0, 0, 0))
    return [
        pl.BlockSpec((None, 2, 2, CMP_STRIDE, 128), im5),
        pl.BlockSpec((None, 2, 2048, 512), im4),
        pl.BlockSpec((None, 2, 2048, 512), im4),
        pl.BlockSpec((None, 2, 512, 128), im4),
    ]


def _compress_rows_call(rows, pe, wa, wb, w2, *, l):
    b, t, _ = rows.shape
    n = t // CMP_STRIDE
    out = jax.ShapeDtypeStruct((b, n, 128), BF16)
    return pl.pallas_call(
        functools.partial(_compress_rows_kernel, n=n),
        grid=(b,),
        in_specs=[pl.BlockSpec((1, t, 128), lambda i: (i, 0, 0)),
                  pl.BlockSpec((1, t, 128), lambda i: (i, 0, 1))] + _cmp_weight_specs(l, 1),
        out_specs=(pl.BlockSpec((1, n, 128), lambda i: (i, 0, 0)),) * 2,
        out_shape=(out, out),
        compiler_params=_cparams(("arbitrary",)),
        name="compress_rows",
    )(rows, rows, pe, wa, wb, w2)


def _compress_pages_call(pt_flat, cache_t, pe, wa, wb, w2, *, l, n_seq, n_pages, n_phys, seqs_step):
    page = cache_t.shape[2]
    n_pages_step = seqs_step * n_pages
    n = n_pages_step * page // CMP_STRIDE
    steps = n_seq // seqs_step

    def page_spec(k):
        s, p = divmod(k, n_pages)
        return pl.BlockSpec((None, 256, page),
                            lambda i, pt: (l * n_phys + pt[(i * seqs_step + s) * n_pages + p], 0, 0))

    out = jax.ShapeDtypeStruct((steps, n, 128), BF16)
    return pl.pallas_call(
        functools.partial(_compress_pages_kernel, n_pages_step=n_pages_step, page=page, n=n),
        grid_spec=pltpu.PrefetchScalarGridSpec(
            num_scalar_prefetch=1,
            grid=(steps,),
            in_specs=[page_spec(k) for k in range(n_pages_step)] + _cmp_weight_specs(l, 2),
            out_specs=(pl.BlockSpec((1, n, 128), lambda i, pt: (i, 0, 0)),) * 2,
            scratch_shapes=[pltpu.VMEM((n_pages_step * page, 128), F32)] * 2,
        ),
        out_shape=(out, out),
        compiler_params=_cparams(("arbitrary",)),
        name="compress_pages",
    )(pt_flat, *([cache_t] * n_pages_step), pe, wa, wb, w2)


def _q_rows(q, tq):
    lo = lax.broadcasted_iota(jnp.int32, (tq, 128), 1) < 64
    cols = [q[:, c * 128:(c + 1) * 128] for c in range(4)]
    return jnp.concatenate([jnp.where(lo, c, 0.0) for c in cols] + [jnp.where(lo, 0.0, c) for c in cols], axis=0)


def _pair_cols(o, tq):
    lo = lax.broadcasted_iota(jnp.int32, (tq, 128), 1) < 64
    return [jnp.where(lo, o[c * tq:(c + 1) * tq], o[(4 + c) * tq:(5 + c) * tq]) for c in range(4)]


def _row_pos(tq, n, pos_base):
    row = lax.broadcasted_iota(jnp.int32, (8 * tq, n), 0)
    return pos_base + (row & (tq - 1))


def _add_row_bias(s, bias, tq):
    rows, n = s.shape
    return (s.reshape(rows // tq, tq, n) + bias[None]).reshape(rows, n)


def _compressed_branch(qc, kc, vc, ov, tq, pos_base):
    nc = kc.shape[0]
    tpos = pos_base + lax.broadcasted_iota(jnp.int32, (tq, nc), 0)
    cend = lax.broadcasted_iota(jnp.int32, (tq, nc), 1) * CMP_STRIDE + (CMP_BLOCK - 1)
    s = _add_row_bias(_dot_nt(qc, kc), jnp.where(cend <= tpos, 0.0, NEG), tq)
    e = jnp.exp2(s - jnp.max(s, axis=-1, keepdims=True))
    e = jnp.where(_row_pos(tq, 1, pos_base) >= CMP_BLOCK - 1, e, 0.0)
    den = jnp.sum(e, axis=-1, keepdims=True)
    p = (e / jnp.where(den > 0.0, den, 1.0)).astype(BF16)
    o = _dot(p, vc)
    imp8 = _dot(p, ov)
    imp = jnp.concatenate([imp8[(4 * g) * tq:(4 * g + 1) * tq] + imp8[(4 * g + 1) * tq:(4 * g + 2) * tq]
                           + imp8[(4 * g + 2) * tq:(4 * g + 3) * tq] + imp8[(4 * g + 3) * tq:(4 * g + 4) * tq]
                           for g in range(2)], axis=0)
    return o, imp


def _select_blocks(imp, tq, pos_base, n_sel):
    row = lax.broadcasted_iota(jnp.int32, (2 * tq, 128), 0)
    blk = lax.broadcasted_iota(jnp.int32, (2 * tq, 128), 1)
    pos = pos_base + (row & (tq - 1))
    cur = pos // SEL_BLOCK
    forced = (blk == 0) | (blk == cur) | (blk == cur - 1)
    valid = blk * SEL_BLOCK <= pos
    score = jnp.where(valid, imp + jnp.where(forced, FORCE_BONUS, 0.0), NEG)

    if 2 * tq >= 128:
        ns8 = -(-n_sel // 8) * 8
        st = score.T[0:ns8]
        sub = lax.broadcasted_iota(jnp.int32, st.shape, 0)
        slabs = [st[r:r + 8] for r in range(0, ns8, 8)]
        ranks = [jnp.zeros((8, 2 * tq), F32) for _ in slabs]
        sub8 = lax.broadcasted_iota(jnp.int32, (8, 2 * tq), 0)
        for b in range(n_sel):
            other = st[b:b + 1, :]
            for k, slab in enumerate(slabs):
                if b < 8 * k:
                    beats = other >= slab
                elif b >= 8 * k + 8:
                    beats = other > slab
                else:
                    beats = (other > slab) | ((other == slab) & (sub8 + 8 * k > b))
                ranks[k] = ranks[k] + jnp.where(beats, 1.0, 0.0)
        rank = jnp.concatenate(ranks, axis=0)
        keep = jnp.where((rank < SEL_TOP_K) & (sub < n_sel), 1.0, 0.0)
        keep = jnp.concatenate([keep, jnp.zeros((128 - ns8, 2 * tq), F32)], axis=0)
        return keep.T
    rank = jnp.zeros(score.shape, F32)
    for b in range(n_sel):
        other = score[:, b:b + 1]
        rank = rank + jnp.where((other > score) | ((other == score) & (blk > b)), 1.0, 0.0)
    return jnp.where((rank < SEL_TOP_K) & (blk < n_sel), 1.0, 0.0)


def _softmax_pv(pieces):
    m = None
    for s, _, _ in pieces:
        mi = jnp.max(s, axis=-1, keepdims=True)
        m = mi if m is None else jnp.maximum(m, mi)
    den = 0.0
    acc = 0.0
    for s, v, fm in pieces:
        p = jnp.exp2(s - m)
        den = den + jnp.sum(p, axis=-1, keepdims=True)
        pb = p.astype(BF16)
        acc = acc + (_dot_nt(pb, v) if fm else _dot(pb, v))
    return acc / den


def _gated_sum(branches, gates, tq):
    lo = lax.broadcasted_iota(jnp.int32, (tq, 128), 1) < 64
    cols = [_pair_cols(o, tq) for o in branches]
    out = []
    for c in range(4):
        acc = 0.0
        for n in range(3):
            ga = gates[:, 3 * c + n:3 * c + n + 1]
            gb = gates[:, 3 * (c + 4) + n:3 * (c + 4) + n + 1]
            acc = acc + cols[n][c] * jnp.where(lo, ga, gb)
        out.append(acc)
    return out


def _attn_prompt_kernel(qq_ref, gate_ref, kc_ref, vc_ref, kst_ref, vst_ref, kwt_ref, vwt_ref, et_ref, ov_ref,
                        o_ref, kaug_sc, m_sc, l_sc, acc_sc, *, tq, t_len, tk):
    j = pl.program_id(1)
    rows = 8 * tq

    @pl.when(j == 0)
    def _():
        kaug_sc[0:128, :] = kst_ref[0]
        kaug_sc[128:256, :] = et_ref[...]

    pos_base = j * tq
    qq = qq_ref[0].astype(F32)
    qc = _q_rows(qq[:, 0:512], tq).astype(BF16)
    qs = _q_rows(qq[:, 512:1024], tq).astype(BF16)

    o_cmp, imp = _compressed_branch(qc, kc_ref[0], vc_ref[0], ov_ref[...], tq, pos_base)
    sel = _select_blocks(imp, tq, pos_base, -(-t_len // SEL_BLOCK))
    bias = jnp.where(sel > 0.5, 0.0, NEG).astype(BF16)
    bias_rows = jnp.concatenate([bias[0:tq]] * 4 + [bias[tq:2 * tq]] * 4, axis=0)
    q_aug = jnp.concatenate([qs, bias_rows], axis=1)

    m_sc[...] = jnp.full((rows, 128), NEG, F32)
    l_sc[...] = jnp.zeros((rows, 128), F32)
    acc_sc[...] = jnp.zeros((rows, 128), F32)

    def update(k0, causal, width):
        s = _dot(q_aug, kaug_sc[:, pl.ds(k0, width)])
        if causal:
            kpos = k0 + lax.broadcasted_iota(jnp.int32, (tq, width), 1)
            qpos = pos_base + lax.broadcasted_iota(jnp.int32, (tq, width), 0)
            s = _add_row_bias(s, jnp.where(kpos <= qpos, 0.0, NEG), tq)
        m_old = m_sc[...]
        m_new = jnp.maximum(m_old, jnp.max(s, axis=-1, keepdims=True))
        alpha = jnp.exp2(m_old - m_new)
        p = jnp.exp2(s - jnp.concatenate([m_new] * (width // 128), axis=1))
        l_sc[...] = alpha * l_sc[...] + jnp.sum(p, axis=-1, keepdims=True)
        acc_sc[...] = alpha * acc_sc[...] + _dot_nt(p.astype(BF16), vst_ref[0, :, pl.ds(k0, width)])
        m_sc[...] = m_new

    n_bulk = (j * tq) // tk

    def bulk(kt, c):
        update(pl.multiple_of(kt * tk, tk), False, tk)
        return c

    lax.fori_loop(0, n_bulk, bulk, 0)
    k_last = pl.multiple_of(n_bulk * tk, tk)
    short = (j + 1) * tq - n_bulk * tk <= tk // 2

    @pl.when(short)
    def _():
        update(k_last, True, tk // 2)

    @pl.when(jnp.logical_not(short))
    def _():
        update(k_last, True, tk)

    o_slc = acc_sc[...] / l_sc[...]

    wk = WINDOW + tq
    k0 = pl.multiple_of(jnp.maximum(j * tq - WINDOW, 0), 128)
    dist = (pos_base + lax.broadcasted_iota(jnp.int32, (tq, wk), 0)) - (k0 + lax.broadcasted_iota(jnp.int32, (tq, wk), 1))
    in_win = jnp.where((dist >= 0) & (dist < WINDOW), 0.0, NEG)
    o_win = jnp.concatenate([
        _softmax_pv([(_add_row_bias(_dot(qs[g * 4 * tq:(g + 1) * 4 * tq], kwt_ref[0, :, pl.ds(k0, wk)]), in_win, tq),
                      vwt_ref[0, :, pl.ds(k0, wk)], True)]) for g in range(N_KV_HEADS)], axis=0)

    cols = _gated_sum([o_cmp, o_slc, o_win], gate_ref[0], tq)
    for c in range(4):
        o_ref[0, :, c * 128:(c + 1) * 128] = cols[c].astype(BF16)


def _attn_prompt_call(qq, gates, kc, vc, kvtb, et, ov):
    b, t_len, _ = qq.shape
    tq = Q_BLOCK
    tk = 512
    nc = kc.shape[1]
    kern = functools.partial(_attn_prompt_kernel, tq=tq, t_len=t_len, tk=tk)
    kv_blk = lambda c: pl.BlockSpec((1, 128, t_len), lambda i, j: (i, c, 0))
    return pl.pallas_call(
        kern,
        grid=(b, t_len // tq),
        in_specs=[
            pl.BlockSpec((1, tq, 1024), lambda i, j: (i, j, 0)),
            pl.BlockSpec((1, tq, 128), lambda i, j: (i, j, 0)),
            pl.BlockSpec((1, nc, 128), lambda i, j: (i, 0, 0)),
            pl.BlockSpec((1, nc, 128), lambda i, j: (i, 0, 0)),
            kv_blk(2), kv_blk(3), kv_blk(4), kv_blk(5),
            pl.BlockSpec((128, t_len), lambda i, j: (0, 0)),
            pl.BlockSpec((nc, 128), lambda i, j: (0, 0)),
        ],
        out_specs=pl.BlockSpec((1, tq, 512), lambda i, j: (i, j, 0)),
        out_shape=jax.ShapeDtypeStruct((b, t_len, 512), BF16),
        scratch_shapes=[pltpu.VMEM((256, t_len), BF16), pltpu.VMEM((8 * tq, 128), F32),
                        pltpu.VMEM((8 * tq, 128), F32), pltpu.VMEM((8 * tq, 128), F32)],
        compiler_params=_cparams(("arbitrary", "arbitrary")),
        name="attn_prompt",
    )(qq, gates, kc, vc, kvtb, kvtb, kvtb, kvtb, et, ov)


def _attn_sample_kernel(pt_ref, *refs, n_pages, page, tq, past, win_rows, spb):
    pages = refs[:spb * n_pages]
    (qq_ref, gate_ref, kc_ref, vc_ref, new_ref, cwin_ref, neww_ref, es_ref,
     ov_ref) = refs[spb * n_pages:spb * n_pages + 9]
    o_ref, wout_ref = refs[-2:]
    rows = 8 * tq
    pos_base = past
    for sq in range(spb):
        seq_pages = pages[sq * n_pages:(sq + 1) * n_pages]
        qq = qq_ref[sq].astype(F32)
        qc = _q_rows(qq[:, 0:512], tq).astype(BF16)
        qs = _q_rows(qq[:, 512:1024], tq).astype(BF16)

        o_cmp, imp = _compressed_branch(qc, kc_ref[sq], vc_ref[sq], ov_ref[...], tq, pos_base)
        sel = _select_blocks(imp, tq, pos_base, -(-(past + tq) // SEL_BLOCK))

        new = jnp.concatenate([new_ref[sq].astype(F32), jnp.zeros((128 - tq, 512), F32)], axis=0).astype(BF16)

        kst = jnp.concatenate([pg[0:128, :].astype(BF16) for pg in seq_pages], axis=1)
        vst = jnp.concatenate([pg[128:256, :].astype(BF16) for pg in seq_pages], axis=1)
        n_keys = past + 128
        member = _dot(sel.astype(BF16), es_ref[...])
        member = jnp.concatenate([member[0:tq]] * 4 + [member[tq:2 * tq]] * 4, axis=0)
        kpos = lax.broadcasted_iota(jnp.int32, (rows, n_keys), 1)
        ok = (member > 0.5) & (kpos <= _row_pos(tq, n_keys, pos_base))
        s_past = jnp.where(ok[:, 0:past], _dot(qs, kst), NEG)
        s_new = jnp.where(ok[:, past:], _dot_nt(qs, new[:, 0:128]), NEG)
        o_slc = _softmax_pv([(s_past, vst, True), (s_new, new[:, 128:256], False)])

        cw = cwin_ref[sq]
        n_wk = win_rows + 128
        kpos_w = (past - win_rows) + lax.broadcasted_iota(jnp.int32, (rows, n_wk), 1)
        dist = _row_pos(tq, n_wk, pos_base) - kpos_w
        okw = (dist >= 0) & (dist < WINDOW)
        s_old = jnp.where(okw[:, 0:win_rows], _dot(qs, cw[0:128].astype(BF16)), NEG)
        s_nw = jnp.where(okw[:, win_rows:], _dot_nt(qs, new[:, 256:384]), NEG)
        o_win = _softmax_pv([(s_old, cw[128:256].astype(BF16), True), (s_nw, new[:, 384:512], False)])

        cols = _gated_sum([o_cmp, o_slc, o_win], gate_ref[sq], tq)
        for c in range(4):
            o_ref[sq, :, c * 128:(c + 1) * 128] = cols[c].astype(BF16)

        wout_ref[sq] = pltpu.roll(cw, win_rows - tq, 1)
        wout_ref[sq, :, win_rows - tq:] = neww_ref[sq]


def _attn_sample_call(pt_flat, cache_t, qq, gates, kc, vc, new_rows, cwin_t, new_win_t, es, ov, win_prev, *,
                      l, depth, n_pages, n_phys, past):
    n_seq, tq, _ = qq.shape
    page = cache_t.shape[2]
    win_rows = cwin_t.shape[2]
    nc = kc.shape[1]
    spb = 2 if n_seq % 2 == 0 else 1
    steps = n_seq // spb
    kern = functools.partial(_attn_sample_kernel, n_pages=n_pages, page=page, tq=tq, past=past, win_rows=win_rows,
                             spb=spb)

    def page_spec(k):
        sq, p = divmod(k, n_pages)
        return pl.BlockSpec((None, 256, page), lambda i, pt: (l * n_phys + pt[(i * spb + sq) * n_pages + p], 1, 0))

    seq_blk = lambda r, w: pl.BlockSpec((spb, r, w), lambda i, pt: (i, 0, 0))
    win_blk = pl.BlockSpec((spb, 256, win_rows), lambda i, pt: (l * steps + i, 0, 0))
    in_specs = [page_spec(k) for k in range(spb * n_pages)] + [
        seq_blk(tq, 1024), seq_blk(tq, 128), seq_blk(nc, 128), seq_blk(nc, 128), seq_blk(tq, 512),
        win_blk, seq_blk(256, tq),
        pl.BlockSpec((128, past + 128), lambda i, pt: (0, 0)),
        pl.BlockSpec((nc, 128), lambda i, pt: (0, 0)),
    ]
    operands = [pt_flat] + [cache_t] * (spb * n_pages) + [qq, gates, kc, vc, new_rows, cwin_t, new_win_t, es, ov]
    aliases = {}
    if win_prev is not None:
        in_specs.append(pl.BlockSpec(memory_space=pl.ANY))
        aliases = {len(operands): 1}
        operands.append(win_prev)
    return pl.pallas_call(
        kern,
        grid_spec=pltpu.PrefetchScalarGridSpec(
            num_scalar_prefetch=1,
            grid=(steps,),
            in_specs=in_specs,
            out_specs=(seq_blk(tq, 512), win_blk),
        ),
        out_shape=(jax.ShapeDtypeStruct((n_seq, tq, 512), BF16),
                   jax.ShapeDtypeStruct((depth * n_seq, 256, win_rows), F32)),
        input_output_aliases=aliases,
        compiler_params=_cparams(("arbitrary",)),
        name="attn_sample",
    )(*operands)


def _mix_kernel(oa_ref, ocp_ref, x_ref, wo_ref, g_ref, b_ref, wr_ref, br_ref, *rest, alpha):
    x1_ref, route_ref, cnt_ref = rest[-3:]
    mix = _dot(oa_ref[...], wo_ref[0:512, :]) + _dot(ocp_ref[...], wo_ref[512:1024, :])
    x1 = _layer_norm(alpha * x_ref[...] + mix, g_ref[...], b_ref[...])
    x1_ref[...] = x1

    tm = x1.shape[0]
    xh = x1.astype(BF16)
    xl = (x1 - xh.astype(F32)).astype(BF16)
    hw = _dot(xh, wr_ref[...])
    logits = hw[:, 0:128] + hw[:, 128:256] + _dot(xl, wr_ref[:, 0:128]) + br_ref[...]
    lane = lax.broadcasted_iota(jnp.int32, (tm, 128), 1).astype(F32)
    big = 1e9
    is_g = (lane >= N_EXPERTS) & (lane < N_EXPERTS + N_EXPERT_GROUPS)
    lg = jnp.where(is_g, logits, NEG)
    ge = jnp.where(is_g, jnp.exp(lg - jnp.max(lg, axis=-1, keepdims=True)), 0.0)
    gp = ge / jnp.sum(ge, axis=-1, keepdims=True)
    gw = jnp.max(gp, axis=-1, keepdims=True)
    gidx = jnp.min(jnp.where(is_g & (gp == gw), lane - N_EXPERTS, big), axis=-1, keepdims=True)
    in_g = (lane >= gidx * EXPERTS_PER_GROUP) & (lane < (gidx + 1.0) * EXPERTS_PER_GROUP)
    le = jnp.where(in_g, logits, NEG)
    ee = jnp.where(in_g, jnp.exp(le - jnp.max(le, axis=-1, keepdims=True)), 0.0)
    ep = ee / jnp.sum(ee, axis=-1, keepdims=True)
    w1 = jnp.max(jnp.where(in_g, ep, -1.0), axis=-1, keepdims=True)
    i1 = jnp.min(jnp.where(in_g & (ep == w1), lane, big), axis=-1, keepdims=True)
    rest = in_g & (lane != i1)
    w2 = jnp.max(jnp.where(rest, ep, -1.0), axis=-1, keepdims=True)
    i2 = jnp.min(jnp.where(rest & (ep == w2), lane, big), axis=-1, keepdims=True)
    den = w1 + w2
    route_ref[...] = jnp.where(lane == 0.0, i1, jnp.where(lane == 1.0, i2, jnp.where(
        lane == 2.0, gw * (w1 / den), jnp.where(lane == 3.0, gw * (w2 / den), 0.0))))
    pairs = jnp.sum(jnp.where((lane == i1) | (lane == i2), 1.0, 0.0), axis=0, keepdims=True)
    cnt_ref[0] = jnp.broadcast_to(pairs, (8, 128))


def _mix_call(oa, ocp, x, wo, g, b, wr, br, *, l, alpha, tm, tile0, n_all, prev):
    n, d = x.shape
    row = lambda w: pl.BlockSpec((tm, w), lambda i: (i, 0))
    out_row = lambda w: pl.BlockSpec((tm, w), lambda i: (i + tile0, 0))
    vec = lambda w: pl.BlockSpec((None, 1, w), lambda i: (l, 0, 0))
    in_specs = [row(512), row(512), row(d),
                pl.BlockSpec((None, 1024, d), lambda i: (l, 0, 0)), vec(d), vec(d),
                pl.BlockSpec((None, d, 256), lambda i: (l, 0, 0)), vec(128)]
    operands = [oa, ocp, x, wo, g, b, wr, br]
    aliases = {}
    if prev is not None:
        aliases = {len(operands) + k: k for k in range(3)}
        in_specs += [pl.BlockSpec(memory_space=pl.ANY)] * 3
        operands += list(prev)
    return pl.pallas_call(
        functools.partial(_mix_kernel, alpha=alpha),
        grid=(n // tm,),
        in_specs=in_specs,
        out_specs=(out_row(d), out_row(128), pl.BlockSpec((1, 8, 128), lambda i: (i + tile0, 0, 0))),
        out_shape=(jax.ShapeDtypeStruct((n_all, d), F32), jax.ShapeDtypeStruct((n_all, 128), F32),
                   jax.ShapeDtypeStruct((n_all // tm, 8, 128), F32)),
        input_output_aliases=aliases,
        compiler_params=_cparams(("arbitrary",)),
        name="mix_router",
    )(*operands)


MOE_CHUNK = 16
MOE_TT = 512
MOE_TE = 256
MOE_SLOTS = 2 * MOE_TT + N_EXPERTS * MOE_CHUNK
MOE_XW = 1024 + LANES


def _chunk_copy(src, s0, dst, d0, sem):
    return pltpu.make_async_copy(src.at[pl.ds(s0, MOE_CHUNK), :], dst.at[pl.ds(d0, MOE_CHUNK), :], sem)


def _for_each_chunk(i, lo_s, dst_s, nch_s, fn):
    def per_expert(e, total):
        k = i * N_EXPERTS + e
        n = nch_s[k]

        def per_chunk(c, carry):
            fn(pl.multiple_of(lo_s[k] + c * MOE_CHUNK, MOE_CHUNK), pl.multiple_of(dst_s[k] + c * MOE_CHUNK, MOE_CHUNK))
            return carry

        lax.fori_loop(0, n, per_chunk, 0)
        return total + n

    return lax.fori_loop(0, N_EXPERTS, per_expert, 0)


def _split3(c):
    h = c.astype(BF16).astype(F32)
    m = (c - h).astype(BF16).astype(F32)
    r = (c - h - m).astype(BF16).astype(F32)
    lane = lax.broadcasted_iota(jnp.int32, (c.shape[0], 128), 1)
    return jnp.where(lane == 0, h, jnp.where(lane == 1, m, jnp.where(lane == 2, r, 0.0)))


def _tile_chunks(i, nch_s):
    return lax.fori_loop(0, N_EXPERTS, lambda e, total: total + nch_s[i * N_EXPERTS + e], 0)


def _wait_chunks(n, src, dst, sem):
    def body(c, carry):
        _chunk_copy(src, 0, dst, 0, sem).wait()
        return carry
    lax.fori_loop(0, n, body, 0)


def _dispatch_kernel(lo_s, dst_s, nch_s, fst_s, fn_s, x1_ref, route_ref, lov_ref, xs_hbm, xs_sc, z_sc, sem):
    i = pl.program_id(0)
    last = pl.num_programs(0) - 1
    tt = MOE_TT
    cur = i % 2
    buf = xs_sc.at[cur]

    @pl.when(i == 0)
    def _():
        z_sc[...] = jnp.zeros_like(z_sc)

        def per_expert(e, total):
            def per_chunk(c, carry):
                _chunk_copy(z_sc, 0, xs_hbm, pl.multiple_of(fst_s[e] + c * MOE_CHUNK, MOE_CHUNK), sem.at[1]).start()
                return carry
            lax.fori_loop(0, fn_s[e], per_chunk, 0)
            return total + fn_s[e]

        _wait_chunks(lax.fori_loop(0, N_EXPERTS, per_expert, 0), z_sc, xs_hbm, sem.at[1])

    @pl.when(i >= 2)
    def _():
        _wait_chunks(_tile_chunks(i - 2, nch_s), buf, xs_hbm, sem.at[cur])

    route = route_ref[...]
    rt = route.T
    eio = lax.broadcasted_iota(jnp.int32, (N_EXPERTS, tt), 0).astype(F32)
    m1 = eio == rt[0:1]
    m2 = eio == rt[1:2]
    before = (lax.broadcasted_iota(jnp.int32, (tt, tt), 0) < lax.broadcasted_iota(jnp.int32, (tt, tt), 1))
    rank = _dot(jnp.where(m1 | m2, 1.0, 0.0).astype(BF16), jnp.where(before, 1.0, 0.0).astype(BF16))
    slot = jnp.concatenate([lov_ref[0]] * (tt // 128), axis=1) + rank
    s1 = jnp.sum(jnp.where(m1, slot, 0.0), axis=0, keepdims=True)
    s2 = jnp.sum(jnp.where(m2, slot, 0.0), axis=0, keepdims=True)
    sio = lax.broadcasted_iota(jnp.int32, (MOE_SLOTS, tt), 0).astype(F32)
    p1 = jnp.where(sio == s1, 1.0, 0.0).astype(BF16)
    p2 = jnp.where(sio == s2, 1.0, 0.0).astype(BF16)
    buf[:, 0:1024] = _dot(p1 + p2, x1_ref[...].astype(BF16)).astype(BF16)
    cw = _dot(p1, _split3(route[:, 2:3]).astype(BF16)) + _dot(p2, _split3(route[:, 3:4]).astype(BF16))
    buf[:, 1024:MOE_XW] = cw.astype(BF16)

    n_out = _for_each_chunk(i, lo_s, dst_s, nch_s,
                            lambda s0, d0: _chunk_copy(buf, s0, xs_hbm, d0, sem.at[cur]).start())

    @pl.when(i == last)
    def _():
        _wait_chunks(n_out, buf, xs_hbm, sem.at[cur])

        @pl.when(i >= 1)
        def _():
            _wait_chunks(_tile_chunks(i - 1, nch_s), xs_sc.at[1 - cur], xs_hbm, sem.at[1 - cur])


def _experts_kernel(te_s, blk_s, nact_s, xs_ref, wg_ref, wu_ref, wd_ref, y_ref, wgu_sc, wd_sc):
    k = pl.program_id(0)
    active = k < nact_s[0]
    fresh = (k == 0) | (te_s[k] != te_s[jnp.maximum(k - 1, 0)])

    @pl.when(active & fresh)
    def _():
        wgu_sc[:, 0:D_EXPERT] = wg_ref[...].astype(BF16)
        wgu_sc[:, D_EXPERT:2 * D_EXPERT] = wu_ref[...].astype(BF16)
        wd_sc[...] = wd_ref[...].astype(BF16)

    @pl.when(active)
    def _():
        cw = xs_ref[:, 1024:MOE_XW].astype(F32)
        c = cw[:, 0:1] + cw[:, 1:2] + cw[:, 2:3]
        hgu = _dot(xs_ref[:, 0:1024], wgu_sc[...])
        hg = hgu[:, 0:D_EXPERT]
        hid = hg * (1.0 / (1.0 + jnp.exp(-hg))) * hgu[:, D_EXPERT:2 * D_EXPERT]
        y_ref[...] = _dot((hid * c).astype(BF16), wd_sc[...]).astype(BF16)


def _combine_kernel(lo_s, dst_s, nch_s, x1_ref, route_ref, lor_ref, g_ref, b_ref, ys_hbm, oa_ref, ob_ref, ys_sc,
                    sem, *, alpha, nt_a):
    i = pl.program_id(0)
    tt = MOE_TT
    cur = i % 2

    def fetch(tile, slot):
        return _for_each_chunk(tile, lo_s, dst_s, nch_s,
                               lambda s0, d0: _chunk_copy(ys_hbm, d0, ys_sc.at[slot], s0, sem.at[slot]).start())

    @pl.when(i == 0)
    def _():
        ys_sc[...] = jnp.zeros_like(ys_sc)
        fetch(0, 0)

    @pl.when(i + 1 < pl.num_programs(0))
    def _():
        fetch(i + 1, 1 - cur)

    route = route_ref[...]
    lane = lax.broadcasted_iota(jnp.int32, (tt, 128), 1).astype(F32)
    m1 = lane == route[:, 0:1]
    m2 = lane == route[:, 1:2]
    before = (lax.broadcasted_iota(jnp.int32, (tt, tt), 1) < lax.broadcasted_iota(jnp.int32, (tt, tt), 0))
    rank = _dot(jnp.where(before, 1.0, 0.0).astype(BF16), jnp.where(m1 | m2, 1.0, 0.0).astype(BF16))
    slot = lor_ref[0, 0:1, :] + rank
    s1 = jnp.sum(jnp.where(m1, slot, 0.0), axis=-1, keepdims=True)
    s2 = jnp.sum(jnp.where(m2, slot, 0.0), axis=-1, keepdims=True)
    sio = lax.broadcasted_iota(jnp.int32, (tt, MOE_SLOTS), 1).astype(F32)
    place = jnp.where((sio == s1) | (sio == s2), 1.0, 0.0).astype(BF16)

    _wait_chunks(_tile_chunks(i, nch_s), ys_hbm, ys_sc.at[cur], sem.at[cur])
    y = _dot(place, ys_sc[cur])
    out = _layer_norm(alpha * x1_ref[...] + y, g_ref[...], b_ref[...])

    @pl.when(i < nt_a)
    def _():
        oa_ref[...] = out

    @pl.when(i >= nt_a)
    def _():
        ob_ref[...] = out


def _moe_call(x1, route, cnt, w_gate, w_up, w_down, g, b, *, l, alpha, nt_a):
    n, d = x1.shape
    tt, te, ch, n_e = MOE_TT, MOE_TE, MOE_CHUNK, N_EXPERTS
    nt = n // tt
    n_et = -(-(2 * n + nt * n_e * (ch - 1)) // te) + n_e
    i32 = jnp.int32

    pairs = cnt[:, 0, :n_e].astype(i32)
    pc = (pairs + ch - 1) // ch * ch
    lo = jnp.cumsum(pc, axis=1) - pc
    tot = jnp.sum(pc, axis=0)
    reg = (tot + te - 1) // te * te
    base = jnp.cumsum(reg) - reg
    dst = base[None, :] + jnp.cumsum(pc, axis=0) - pc
    tiles_e = reg // te
    ends = jnp.cumsum(tiles_e)
    n_act = ends[-1]
    k = jnp.arange(n_et, dtype=i32)
    blk = jnp.minimum(k, n_act - 1)
    tile_e = jnp.minimum(jnp.sum((ends[None, :] <= blk[:, None]).astype(i32), axis=1), n_e - 1)
    flat = lambda a: a.reshape(-1).astype(i32)
    lo_s, dst_s, nch_s = flat(lo), flat(dst), flat(pc // ch)
    lov = jnp.broadcast_to(lo.astype(F32)[:, :, None], (nt, n_e, 128))
    lor = jnp.broadcast_to(jnp.pad(lo.astype(F32), ((0, 0), (0, 128 - n_e)))[:, None, :], (nt, 8, 128))

    xs = pl.pallas_call(
        _dispatch_kernel,
        grid_spec=pltpu.PrefetchScalarGridSpec(
            num_scalar_prefetch=5,
            grid=(nt,),
            in_specs=[pl.BlockSpec((tt, d), lambda i, *_: (i, 0)),
                      pl.BlockSpec((tt, 128), lambda i, *_: (i, 0)),
                      pl.BlockSpec((1, n_e, 128), lambda i, *_: (i, 0, 0))],
            out_specs=pl.BlockSpec(memory_space=pl.ANY),
            scratch_shapes=[pltpu.VMEM((2, MOE_SLOTS, MOE_XW), BF16), pltpu.VMEM((ch, MOE_XW), BF16),
                            pltpu.SemaphoreType.DMA((2,))],
        ),
        out_shape=jax.ShapeDtypeStruct((n_et * te, MOE_XW), BF16),
        compiler_params=_cparams(("arbitrary",)),
        name="moe_dispatch",
    )(lo_s, dst_s, nch_s, flat(base + tot), flat((reg - tot) // ch), x1, route, lov)

    wspec = lambda r, c: pl.BlockSpec((None, None, r, c), lambda k, te_s, blk_s, na: (l, te_s[k], 0, 0))
    ys = pl.pallas_call(
        _experts_kernel,
        grid_spec=pltpu.PrefetchScalarGridSpec(
            num_scalar_prefetch=3,
            grid=(n_et,),
            in_specs=[pl.BlockSpec((te, MOE_XW), lambda k, te_s, blk_s, na: (blk_s[k], 0)),
                      wspec(d, D_EXPERT), wspec(d, D_EXPERT), wspec(D_EXPERT, d)],
            out_specs=pl.BlockSpec((te, d), lambda k, te_s, blk_s, na: (blk_s[k], 0)),
            scratch_shapes=[pltpu.VMEM((d, 2 * D_EXPERT), BF16), pltpu.VMEM((D_EXPERT, d), BF16)],
        ),
        out_shape=jax.ShapeDtypeStruct((n_et * te, d), BF16),
        compiler_params=_cparams(("arbitrary",)),
        name="moe_experts",
    )(tile_e, blk, n_act.reshape(1).astype(i32), xs, w_gate, w_up, w_down)

    vec = pl.BlockSpec((None, 1, d), lambda i, *_: (l, 0, 0))
    return pl.pallas_call(
        functools.partial(_combine_kernel, alpha=alpha, nt_a=nt_a),
        grid_spec=pltpu.PrefetchScalarGridSpec(
            num_scalar_prefetch=3,
            grid=(nt,),
            in_specs=[pl.BlockSpec((tt, d), lambda i, *_: (i, 0)),
                      pl.BlockSpec((tt, 128), lambda i, *_: (i, 0)),
                      pl.BlockSpec((1, 8, 128), lambda i, *_: (i, 0, 0)),
                      vec, vec,
                      pl.BlockSpec(memory_space=pl.ANY)],
            out_specs=(pl.BlockSpec((tt, d), lambda i, *_: (jnp.minimum(i, nt_a - 1), 0)),
                       pl.BlockSpec((tt, d), lambda i, *_: (jnp.maximum(i - nt_a, 0), 0))),
            scratch_shapes=[pltpu.VMEM((2, MOE_SLOTS, d), BF16), pltpu.SemaphoreType.DMA((2,))],
        ),
        out_shape=(jax.ShapeDtypeStruct((nt_a * tt, d), F32), jax.ShapeDtypeStruct((n - nt_a * tt, d), F32)),
        compiler_params=_cparams(("arbitrary",)),
        name="moe_combine",
    )(lo_s, dst_s, nch_s, x1, route, lor, g, b, ys)


def _rope_table(pos):
    half = ROT_DIM // 2
    inv = ROPE_THETA ** (-jnp.arange(0, ROT_DIM, 2, dtype=F32) / ROT_DIM)
    ang = jnp.asarray(pos, F32)[:, None] * inv[None, :]
    cos, sin = jnp.cos(ang), jnp.sin(ang)
    n = ang.shape[0]
    z = jnp.zeros((n, HEAD_DIM - ROT_DIM), F32)
    zh = jnp.zeros((n, half), F32)
    c64 = jnp.concatenate([cos, cos, jnp.ones_like(z)], axis=1)
    s1 = jnp.concatenate([-sin, zh, z], axis=1)
    s2 = jnp.concatenate([zh, sin, z], axis=1)
    return jnp.concatenate([c64, c64, s1, s1, s2, s2], axis=1)


def _overlap_table(n_rows):
    c = np.arange(n_rows)[:, None] * CMP_STRIDE
    s = np.arange(128)[None, :] * SEL_BLOCK
    return jnp.asarray(((c < s + SEL_BLOCK) & (c + CMP_BLOCK > s)).astype(np.float32), BF16)


def _block_indicator(n_keys):
    k = np.arange(n_keys)[None, :] // SEL_BLOCK
    return jnp.asarray((np.arange(128)[:, None] == k).astype(np.float32), BF16)


def _block_diag2(w):
    z = jnp.zeros_like(w)
    return jnp.concatenate([jnp.concatenate([w, z], axis=-1), jnp.concatenate([z, w], axis=-1)], axis=-2)


def kernel(x_prompt, x_sample, cache_kv, cache_win, state_conv, state_pool, page_table, ln1_g, ln1_b, w_in, pe_cmp, w_cmp1, w_cmp2, conv_w, pool_w, pool_scale, w_o, ln2_g, ln2_b, w_rg, b_rg, w_re, b_re, w_gate, w_up, w_down):
    n_b, t_len, d_model = x_prompt.shape
    n_seq, t_dec, _ = x_sample.shape
    depth = w_in.shape[0]
    n_phys, page = cache_kv.shape[1], cache_kv.shape[2]
    n_pages = page_table.shape[1]
    past = n_pages * page
    win_rows = cache_win.shape[2]
    alpha = float((2 * depth) ** 0.25)
    assert d_model == 1024 and t_len % 512 == 0 and t_dec == 8 and past % SEL_BLOCK == 0 and win_rows == WINDOW

    pair_order = [h for c in range(4) for h in (c, c + 4)]
    w_main = jnp.concatenate([w_in[:, :, h * 64:(h + 1) * 64] for h in pair_order]
                             + [w_in[:, :, _C_BG:], w_in[:, :, _C_GL:_C_BG],
                              jnp.zeros((depth, d_model, 128 - (_C_BG - _C_GL)), F32)], axis=2)
    wm_p = jnp.concatenate([w_main, w_in[:, :, _C_KV:_C_KV + 256]], axis=2).astype(BF16)
    wm_s = jnp.concatenate([w_main, w_in[:, :, _C_KV + 256:_C_KV + 768]], axis=2).astype(BF16)
    wk_t = jnp.swapaxes(w_in[:, :, _C_KV:_C_GL], 1, 2).astype(BF16)
    cw8 = jnp.concatenate([conv_w, jnp.zeros((depth, 8 - CONV_K, 256), F32)], axis=1)
    pw_bd = jnp.zeros((depth, 256, 256), F32)
    for g in range(4):
        pw_bd = pw_bd.at[:, g * 64:(g + 1) * 64, g * 64:(g + 1) * 64].set(pool_w[:, g])
    pw_bd = pw_bd.astype(BF16)
    ps3 = pool_scale[:, None, :]

    w1r = w_cmp1.reshape(depth, 2, CMP_BLOCK, HEAD_DIM, CMP_HIDDEN)
    bd_half = lambda w: _block_diag2(w).reshape(depth, 2, CMP_STRIDE * 128, 2 * CMP_HIDDEN).astype(BF16)
    wa = bd_half(w1r[:, :, :CMP_STRIDE])
    wb = bd_half(w1r[:, :, CMP_STRIDE:])
    w2 = _block_diag2(w_cmp2).astype(BF16)
    pe2 = jnp.concatenate([pe_cmp, pe_cmp], axis=-1).reshape(depth, 2, 2, CMP_STRIDE, 128)

    wo_perm = jnp.concatenate([w_o[:, h * 64:(h + 1) * 64] for h in pair_order] + [w_o[:, ATT_WIDTH:]],
                              axis=1).astype(BF16)
    wr32 = jnp.concatenate([w_re, w_rg, jnp.zeros((depth, d_model, 128 - N_EXPERTS - N_EXPERT_GROUPS), F32)], axis=2)
    wr_hi = wr32.astype(BF16)
    wr = jnp.concatenate([wr_hi, (wr32 - wr_hi.astype(F32)).astype(BF16)], axis=2)
    br = jnp.concatenate([b_re, b_rg, jnp.zeros((depth, 128 - N_EXPERTS - N_EXPERT_GROUPS), F32)], axis=1)[:, None, :]
    g1, b1, g2, b2 = ln1_g[:, None, :], ln1_b[:, None, :], ln2_g[:, None, :], ln2_b[:, None, :]

    tab_p = _rope_table(np.arange(t_len))
    tab_s = _rope_table(past + np.repeat(np.arange(t_dec), n_seq))
    tab_pt, tab_st = tab_p.T, tab_s.T
    nc_p = t_len // CMP_STRIDE
    nc_s = past // CMP_STRIDE
    ov_p, ov_s = _overlap_table(nc_p), _overlap_table(nc_s)
    et_p = _block_indicator(t_len)
    es_s = _block_indicator(past + 128)

    cache_t = jnp.transpose(cache_kv, (0, 1, 3, 4, 5, 2)).reshape(depth * n_phys, 512, page)
    cwin_t = jnp.transpose(cache_win, (0, 1, 3, 4, 5, 2)).reshape(depth * n_seq, 256, win_rows)
    pt_flat = page_table.reshape(-1)

    def tm_state(st):
        k = st.shape[2]
        st = jnp.swapaxes(st, 1, 2)
        st = jnp.concatenate([jnp.zeros((depth, CARRY_STEPS - k, n_seq, 256), F32), st], axis=1)
        return st.reshape(depth, CARRY_STEPS * n_seq, 256)

    cst_s, pst_s = tm_state(state_conv), tm_state(state_pool)
    zst_p = jnp.zeros((n_b, CARRY_STEPS, 256), F32)

    tm_p = 1024 if t_len % 1024 == 0 else 512
    tm_row = 512
    seqs_step = 4 if n_seq % 4 == 0 else 1

    xp = x_prompt
    xs = x_sample.reshape(n_seq * t_dec, d_model)
    outs = {k: [] for k in ("kv_s", "win_p", "conv_p", "conv_s", "pool_p", "pool_s")}
    win_s_all = None
    kvt_p_all = None
    for l in range(depth):
        qq, gates, ocp, rows, kvt_p_all, wint, kvtb, cnew, pnew = _proj_call(
            xp, wm_p, wk_t, tab_p, tab_pt, cw8, pw_bd, ps3, zst_p, zst_p,
            l=l, tm=tm_p, rs=1, pos0=0, rows_dtype=F32, rows_rope=(), kv_slabs=depth, kvt_prev=kvt_p_all)
        kc, vc = _compress_rows_call(rows, pe2, wa, wb, w2, l=l)
        oa = _attn_prompt_call(qq, gates, kc, vc, kvtb, et_p, ov_p)
        n_p = n_b * t_len
        n_s = n_seq * t_dec
        routed = _mix_call(oa.reshape(n_p, 512), ocp.reshape(n_p, 512), xp.reshape(n_p, d_model),
                           wo_perm, g1, b1, wr, br, l=l, alpha=alpha, tm=MOE_TT, tile0=0, n_all=n_p + n_s,
                           prev=None)
        outs["win_p"].append(jnp.transpose(wint[:, :, t_len - WINDOW:].reshape(n_b, 2, 2, HEAD_DIM, WINDOW),
                                           (0, 4, 1, 2, 3)))
        outs["conv_p"].append(cnew[:, CARRY_STEPS - (CONV_K - 1):])
        outs["pool_p"].append(pnew[:, CARRY_STEPS - POOL_STATE:])

        xs_tm = jnp.swapaxes(xs.reshape(n_seq, t_dec, d_model), 0, 1).reshape(1, t_dec * n_seq, d_model)
        qq, gates, ocp, rows, kvt, wint, kvtb, cnew, pnew = _proj_call(
            xs_tm, wm_s, wk_t, tab_s, tab_st, cw8, pw_bd, ps3, cst_s[l][None], pst_s[l][None],
            l=l, tm=t_dec * n_seq, rs=n_seq, pos0=past, rows_dtype=BF16, rows_rope=(0, 2))
        seq_major = lambda a: jnp.swapaxes(a.reshape(t_dec, n_seq, a.shape[-1]), 0, 1)
        kc, vc = _compress_pages_call(pt_flat, cache_t, pe2, wa, wb, w2, l=l, n_seq=n_seq, n_pages=n_pages,
                                      n_phys=n_phys, seqs_step=seqs_step)
        kc = kc.reshape(n_seq, nc_s, 128)
        vc = vc.reshape(n_seq, nc_s, 128)
        new_win_t = jnp.transpose(wint.reshape(256, t_dec, n_seq), (2, 0, 1))
        oa, win_s_all = _attn_sample_call(pt_flat, cache_t, seq_major(qq), seq_major(gates), kc, vc,
                                          seq_major(rows), cwin_t, new_win_t, es_s, ov_s, win_s_all,
                                          l=l, depth=depth, n_pages=n_pages, n_phys=n_phys, past=past)
        x1, route, cnt = _mix_call(oa.reshape(n_s, 512), seq_major(ocp).reshape(n_s, 512), xs,
                                   wo_perm, g1, b1, wr, br, l=l, alpha=alpha, tm=MOE_TT, tile0=n_p // MOE_TT,
                                   n_all=n_p + n_s, prev=routed)
        xp, xs = _moe_call(x1, route, cnt, w_gate, w_up, w_down, g2, b2, l=l, alpha=alpha, nt_a=n_p // MOE_TT)
        xp = xp.reshape(n_b, t_len, d_model)
        kvt5 = kvt.reshape(4, 2, HEAD_DIM, t_dec, n_seq)
        outs["kv_s"].append(jnp.transpose(kvt5, (4, 3, 0, 1, 2)))
        st_sm = lambda a, k: jnp.swapaxes(a.reshape(CARRY_STEPS, n_seq, 256)[CARRY_STEPS - k:], 0, 1)
        outs["conv_s"].append(st_sm(cnew, CONV_K - 1))
        outs["pool_s"].append(st_sm(pnew, POOL_STATE))

    st = lambda k: jnp.stack(outs[k])
    win_s = jnp.transpose(win_s_all.reshape(depth, n_seq, 2, 2, HEAD_DIM, win_rows), (0, 1, 5, 2, 3, 4))
    kv_p = jnp.transpose(kvt_p_all.reshape(depth, n_b, 4, 2, HEAD_DIM, t_len), (0, 1, 5, 2, 3, 4))
    return (xp, xs.reshape(n_seq, t_dec, d_model), kv_p, st("kv_s"), st("win_p"), win_s,
            st("conv_p"), st("conv_s"), st("pool_p"), st("pool_s"))
```
